```python
import math
import jax, jax.numpy as jnp
from jax import lax
import numpy as np

D_MODEL = 1024
BATCH = 4
SEQ = 4096
DEPTH = 2

N_HEADS = 16
HEAD_DIM = D_MODEL // N_HEADS
NSA_KV_HEADS = 4
NSA_GROUP = N_HEADS // NSA_KV_HEADS
CMP_BLOCK = 32
CMP_STRIDE = 16
CMP_HIDDEN = 256
SEL_BLOCK = 64
SEL_TOPK = 16
WINDOW = 512
NSA_QBLOCK = 64
FOX_QBLOCK = 128
D_FF = int(round(8 * D_MODEL / 3 / 256)) * 256
N_BUCKETS = 32
MAX_DISTANCE = 128
N_A_LAYERS = DEPTH // 2
N_B_LAYERS = DEPTH - N_A_LAYERS
HD = N_HEADS * HEAD_DIM
KVD = NSA_KV_HEADS * HEAD_DIM
NSA_IN = HD + 6 * KVD + 3 * N_HEADS
EPS = 1e-6
NEG_INF = -1e30
SEL_FORCE = 1e4

kernel_name = 'yoco_nsa_fox_macaron_trunk'


def rms_norm(x, g):
    xf = x.astype(jnp.float32)
    y = xf * lax.rsqrt(jnp.mean(xf * xf, axis=-1, keepdims=True) + EPS)
    return (y * g.astype(jnp.float32)).astype(x.dtype)


def swiglu(h, w_in, w_out):
    a, b = jnp.split(h @ w_in, 2, axis=-1)
    return (jax.nn.silu(a) * b) @ w_out


def t5_bucket(dist):
    dist = jnp.maximum(dist, 0)
    max_exact = N_BUCKETS // 2
    scaled = jnp.log(jnp.maximum(dist, 1).astype(jnp.float32) / max_exact) / math.log(MAX_DISTANCE / max_exact)
    large = jnp.minimum(max_exact + (scaled * (N_BUCKETS - max_exact)).astype(jnp.int32), N_BUCKETS - 1)
    return jnp.where(dist < max_exact, dist, large).astype(jnp.int32)


def masked_softmax(logits, mask):
    p = jax.nn.softmax(jnp.where(mask, logits, NEG_INF), axis=-1)
    return jnp.where(mask, p, 0.0)


def compress_blocks(x_kv, pos, w1, w2):
    B, T, G, dh = x_kv.shape
    n_cmp = (T - CMP_BLOCK) // CMP_STRIDE + 1
    idx = np.arange(n_cmp)[:, None] * CMP_STRIDE + np.arange(CMP_BLOCK)[None, :]
    blocks = x_kv[:, idx] + pos[None, None, :, None, :]
    flat = jnp.moveaxis(blocks, 3, 2).reshape(B, n_cmp, G, CMP_BLOCK * dh)
    return jax.nn.silu(flat @ w1) @ w2


def nsa_mixer(h, w_in, cmp_pos, cmp_w1, cmp_w2, rel_bias, w_out):
    B, T, _ = h.shape
    G, R, dh, QB = NSA_KV_HEADS, NSA_GROUP, HEAD_DIM, NSA_QBLOCK
    scale = dh ** -0.5
    q, kc, vc, ks, vs, kw, vw, gl = jnp.split(h @ w_in, [HD + i * KVD for i in range(7)], axis=-1)
    q = q.reshape(B, T, G, R, dh)
    gl = gl.reshape(B, T, G, R, 3)
    kc, vc, ks, vs, kw, vw = (a.reshape(B, T, G, dh) for a in (kc, vc, ks, vs, kw, vw))

    k_cmp = compress_blocks(kc, cmp_pos[0], cmp_w1[0], cmp_w2[0])
    v_cmp = compress_blocks(vc, cmp_pos[1], cmp_w1[1], cmp_w2[1])
    n_cmp = k_cmp.shape[1]
    cmp_start = np.arange(n_cmp) * CMP_STRIDE
    cmp_end = jnp.asarray(cmp_start + CMP_BLOCK - 1)

    n_sel = T // SEL_BLOCK
    k_sel = min(SEL_TOPK, n_sel)
    sel_start = np.arange(n_sel) * SEL_BLOCK
    overlap = jnp.asarray(((cmp_start[:, None] < sel_start[None, :] + SEL_BLOCK)
                           & (cmp_start[:, None] + CMP_BLOCK > sel_start[None, :])).astype(np.float32))
    ks_blk = ks.reshape(B, n_sel, SEL_BLOCK, G, dh).transpose(0, 3, 1, 2, 4)
    vs_blk = vs.reshape(B, n_sel, SEL_BLOCK, G, dh).transpose(0, 3, 1, 2, 4)

    kw_pad = jnp.pad(kw, ((0, 0), (WINDOW, 0), (0, 0), (0, 0)))
    vw_pad = jnp.pad(vw, ((0, 0), (WINDOW, 0), (0, 0), (0, 0)))
    L_win = WINDOW + QB

    tbl_group = rel_bias.reshape(N_BUCKETS, G, R)
    gather_blocks = jax.vmap(jax.vmap(lambda blk, ix: blk[ix]))
    group_bias = jax.vmap(lambda tb, bk: tb[bk], in_axes=(1, 1), out_axes=1)
    blk_id = jnp.arange(n_sel)
    within = jnp.arange(SEL_BLOCK)

    def head_bias(bucket):
        return jnp.moveaxis(rel_bias[bucket], -1, 0).reshape(G, R, *bucket.shape)

    def chunk(i):
        t0 = i * QB
        tpos = t0 + jnp.arange(QB)
        qc = lax.dynamic_slice_in_dim(q, t0, QB, axis=1)
        gc = jax.nn.sigmoid(lax.dynamic_slice_in_dim(gl, t0, QB, axis=1).astype(jnp.float32))
        dist_c = tpos[:, None] - cmp_end[None, :]
        s_c = jnp.einsum('bqgrd,bngd->bgrqn', qc, k_cmp).astype(jnp.float32) * scale + head_bias(t5_bucket(dist_c))
        p_c = masked_softmax(s_c, dist_c >= 0)
        o_c = jnp.einsum('bgrqn,bngd->bqgrd', p_c.astype(v_cmp.dtype), v_cmp)
        imp = jnp.einsum('bgrqn,ns->bgqs', p_c, overlap)
        cur = (tpos // SEL_BLOCK)[:, None]
        forced = (blk_id == 0) | (blk_id == cur) | (blk_id == cur - 1)
        score = jnp.where(blk_id > cur, -SEL_FORCE, imp + jnp.where(forced, SEL_FORCE, 0.0))
        _, sel = lax.top_k(score, k_sel)
        k_g = gather_blocks(ks_blk, sel).reshape(B, G, QB, k_sel * SEL_BLOCK, dh)
        v_g = gather_blocks(vs_blk, sel).reshape(B, G, QB, k_sel * SEL_BLOCK, dh)
        spos = (sel[..., None] * SEL_BLOCK + within).reshape(B, G, QB, k_sel * SEL_BLOCK)
        dist_s = tpos[:, None] - spos
        bias_s = jnp.moveaxis(group_bias(tbl_group, t5_bucket(dist_s)), -1, 2)
        s_s = jnp.einsum('bqgrd,bgqld->bgrql', qc, k_g).astype(jnp.float32) * scale + bias_s
        p_s = masked_softmax(s_s, (dist_s >= 0)[:, :, None])
        o_s = jnp.einsum('bgrql,bgqld->bqgrd', p_s.astype(v_g.dtype), v_g)
        k_w = lax.dynamic_slice_in_dim(kw_pad, t0, L_win, axis=1)
        v_w = lax.dynamic_slice_in_dim(vw_pad, t0, L_win, axis=1)
        kpos = t0 - WINDOW + jnp.arange(L_win)
        dist_w = tpos[:, None] - kpos[None, :]
        mask_w = (dist_w >= 0) & (dist_w < WINDOW) & (kpos >= 0)[None, :]
        s_w = jnp.einsum('bqgrd,blgd->bgrql', qc, k_w).astype(jnp.float32) * scale + head_bias(t5_bucket(dist_w))
        p_w = masked_softmax(s_w, mask_w)
        o_w = jnp.einsum('bgrql,blgd->bqgrd', p_w.astype(v_w.dtype), v_w)
        gc = gc.astype(o_c.dtype)
        o = gc[..., 0:1] * o_c + gc[..., 1:2] * o_s + gc[..., 2:3] * o_w
        return o.reshape(B, QB, HD)

    out = lax.map(chunk, jnp.arange(T // QB))
    return jnp.moveaxis(out, 0, 1).reshape(B, T, HD) @ w_out


def shared_kv(x, g, w_kvf, b_f):
    B, T, _ = x.shape
    k, v, fl = jnp.split(rms_norm(x, g) @ w_kvf, [HD, 2 * HD], axis=-1)
    log_f = jax.nn.log_sigmoid(fl.astype(jnp.float32) + b_f.astype(jnp.float32))
    c = jnp.cumsum(log_f, axis=1)
    return (k.reshape(B, T, N_HEADS, HEAD_DIM), v.reshape(B, T, N_HEADS, HEAD_DIM), c)


def fox_mixer(h, w_q, k, v, c_cum, w_out):
    B, T, _ = h.shape
    QB = FOX_QBLOCK
    scale = HEAD_DIM ** -0.5
    q = (h @ w_q).reshape(B, T, N_HEADS, HEAD_DIM)
    cT = jnp.transpose(c_cum, (0, 2, 1))
    kpos = jnp.arange(T)

    def chunk(i):
        t0 = i * QB
        tpos = t0 + jnp.arange(QB)
        qc = lax.dynamic_slice_in_dim(q, t0, QB, axis=1)
        cq = lax.dynamic_slice_in_dim(cT, t0, QB, axis=2)
        s = jnp.einsum('bqhd,bkhd->bhqk', qc, k).astype(jnp.float32) * scale + cq[..., None] - cT[:, :, None, :]
        p = masked_softmax(s, kpos[None, :] <= tpos[:, None])
        o = jnp.einsum('bhqk,bkhd->bqhd', p.astype(v.dtype), v)
        return o.reshape(B, QB, HD)

    out = lax.map(chunk, jnp.arange(T // QB))
    return jnp.moveaxis(out, 0, 1).reshape(B, T, HD) @ w_out


def setup_inputs(seed: int = 0) -> dict:
    key = jax.random.key(seed)
    ks = jax.random.split(key, 17)
    nrm = jax.random.normal
    f32 = jnp.float32
    x = nrm(ks[0], (BATCH, SEQ, D_MODEL), f32)
    norm_g = 1.0 + 0.05 * nrm(ks[1], (DEPTH, 3, D_MODEL), f32)
    ffn_w_in = nrm(ks[2], (DEPTH, 2, D_MODEL, 2 * D_FF), f32) * D_MODEL ** -0.5
    ffn_w_out = nrm(ks[3], (DEPTH, 2, D_FF, D_MODEL), f32) * D_FF ** -0.5
    nsa_w_in = nrm(ks[4], (N_A_LAYERS, D_MODEL, NSA_IN), f32) * D_MODEL ** -0.5
    nsa_cmp_pos = 0.5 * nrm(ks[5], (N_A_LAYERS, 2, CMP_BLOCK, HEAD_DIM), f32)
    nsa_cmp_w1 = nrm(ks[6], (N_A_LAYERS, 2, CMP_BLOCK * HEAD_DIM, CMP_HIDDEN), f32) * (CMP_BLOCK * HEAD_DIM) ** -0.5
    nsa_cmp_w2 = nrm(ks[7], (N_A_LAYERS, 2, CMP_HIDDEN, HEAD_DIM), f32) * CMP_HIDDEN ** -0.5
    nsa_w_out = nrm(ks[8], (N_A_LAYERS, HD, D_MODEL), f32) * HD ** -0.5
    rel_bias = 0.5 * nrm(ks[9], (N_BUCKETS, N_HEADS), f32)
    kv_norm_g = 1.0 + 0.05 * nrm(ks[10], (D_MODEL,), f32)
    fox_w_kvf = jnp.concatenate([nrm(ks[11], (D_MODEL, 2 * HD), f32) * D_MODEL ** -0.5,
                                 0.1 * nrm(ks[12], (D_MODEL, N_HEADS), f32) * D_MODEL ** -0.5], axis=1)
    fox_b_f = 4.0 + 0.5 * nrm(ks[13], (N_HEADS,), f32)
    fox_w_q = nrm(ks[14], (N_B_LAYERS, D_MODEL, HD), f32) * D_MODEL ** -0.5
    fox_w_out = nrm(ks[15], (N_B_LAYERS, HD, D_MODEL), f32) * HD ** -0.5
    final_g = 1.0 + 0.05 * nrm(ks[16], (D_MODEL,), f32)
    return {'x': x, 'norm_g': norm_g, 'ffn_w_in': ffn_w_in, 'ffn_w_out': ffn_w_out,
            'nsa_w_in': nsa_w_in, 'nsa_cmp_pos': nsa_cmp_pos, 'nsa_cmp_w1': nsa_cmp_w1,
            'nsa_cmp_w2': nsa_cmp_w2, 'nsa_w_out': nsa_w_out, 'rel_bias': rel_bias,
            'kv_norm_g': kv_norm_g, 'fox_w_kvf': fox_w_kvf, 'fox_b_f': fox_b_f,
            'fox_w_q': fox_w_q, 'fox_w_out': fox_w_out, 'final_g': final_g}


def reference(x, norm_g, ffn_w_in, ffn_w_out, nsa_w_in, nsa_cmp_pos, nsa_cmp_w1, nsa_cmp_w2,
              nsa_w_out, rel_bias, kv_norm_g, fox_w_kvf, fox_b_f, fox_w_q, fox_w_out, final_g):
    k_sh = v_sh = c_sh = None
    for l in range(DEPTH):
        if l == N_A_LAYERS:
            k_sh, v_sh, c_sh = shared_kv(x, kv_norm_g, fox_w_kvf, fox_b_f)
        x = x + 0.5 * swiglu(rms_norm(x, norm_g[l, 0]), ffn_w_in[l, 0], ffn_w_out[l, 0])
        h = rms_norm(x, norm_g[l, 1])
        if l < N_A_LAYERS:
            x = x + nsa_mixer(h, nsa_w_in[l], nsa_cmp_pos[l], nsa_cmp_w1[l], nsa_cmp_w2[l], rel_bias, nsa_w_out[l])
        else:
            j = l - N_A_LAYERS
            x = x + fox_mixer(h, fox_w_q[j], k_sh, v_sh, c_sh, fox_w_out[j])
        x = x + 0.5 * swiglu(rms_norm(x, norm_g[l, 2]), ffn_w_in[l, 1], ffn_w_out[l, 1])
    return rms_norm(x, final_g)
```

```python
import functools
import math

import numpy as np
import jax
import jax.numpy as jnp
from jax import lax
from jax.experimental import pallas as pl
from jax.experimental.pallas import tpu as pltpu

N_HEADS = 16
HEAD_DIM = 64
NSA_KV_HEADS = 4
NSA_GROUP = N_HEADS // NSA_KV_HEADS
CMP_BLOCK = 32
CMP_STRIDE = 16
SEL_BLOCK = 64
SEL_BLOCK_SHIFT = SEL_BLOCK.bit_length() - 1
assert 1 << SEL_BLOCK_SHIFT == SEL_BLOCK
SEL_TOPK = 16
WINDOW = 512
N_BUCKETS = 32
MAX_DISTANCE = 128
EPS = 1e-6
NEG_INF = -1e30
SEL_FORCE = 1e4

F32 = jnp.float32
BF16 = jnp.bfloat16

VMEM_LIMIT_BYTES = 48 * 1024 * 1024

ROW_TILE = 512
FF_TILE = 256
CMP_Q_TILE = 128
ATT_Q_TILE = 256
ATT_K_TILE = 256

_NT = (((1,), (1,)), ((), ()))


def _params(*sem):
    return pltpu.CompilerParams(dimension_semantics=sem, vmem_limit_bytes=VMEM_LIMIT_BYTES)


def _rms(x, g):
    return x * lax.rsqrt(jnp.mean(x * x, axis=-1, keepdims=True) + EPS) * g


def _silu(a):
    return a * jax.nn.sigmoid(a)


def _ffn_kernel(x_ref, g_ref, wa_ref, wb_ref, wo_ref, *rest, final_norm):
    if final_norm:
        fg_ref, o_ref, hn_ref, acc_ref = rest
    else:
        o_ref, hn_ref, acc_ref = rest
    j = pl.program_id(1)

    @pl.when(j == 0)
    def _():
        hn_ref[...] = _rms(x_ref[...], g_ref[...]).astype(BF16)
        acc_ref[...] = jnp.zeros_like(acc_ref)

    hn = hn_ref[...]
    a = jnp.dot(hn, wa_ref[...], preferred_element_type=F32)
    b = jnp.dot(hn, wb_ref[...], preferred_element_type=F32)
    act = (_silu(a) * b).astype(BF16)
    acc_ref[...] += jnp.dot(act, wo_ref[...], preferred_element_type=F32)

    @pl.when(j == pl.num_programs(1) - 1)
    def _():
        y = x_ref[...] + 0.5 * acc_ref[...]
        if final_norm:
            y = _rms(y, fg_ref[...])
        o_ref[...] = y


def _ffn(x, g, w_in, w_out, final_g=None):
    n, d = x.shape
    d_ff = w_out.shape[0]
    nj = d_ff // FF_TILE
    in_specs = [
        pl.BlockSpec((ROW_TILE, d), lambda i, j: (i, 0)),
        pl.BlockSpec((1, d), lambda i, j: (0, 0)),
        pl.BlockSpec((d, FF_TILE), lambda i, j: (0, j)),
        pl.BlockSpec((d, FF_TILE), lambda i, j: (0, j + nj)),
        pl.BlockSpec((FF_TILE, d), lambda i, j: (j, 0)),
    ]
    args = [x, g.reshape(1, d), w_in, w_in, w_out]
    if final_g is not None:
        in_specs.append(pl.BlockSpec((1, d), lambda i, j: (0, 0)))
        args.append(final_g.reshape(1, d))
    return pl.pallas_call(
        functools.partial(_ffn_kernel, final_norm=final_g is not None),
        grid=(n // ROW_TILE, nj),
        in_specs=in_specs,
        out_specs=pl.BlockSpec((ROW_TILE, d), lambda i, j: (i, 0)),
        out_shape=jax.ShapeDtypeStruct((n, d), F32),
        scratch_shapes=[pltpu.VMEM((ROW_TILE, d), BF16), pltpu.VMEM((ROW_TILE, d), F32)],
        compiler_params=_params("parallel", "arbitrary"),
    )(*args)


PROJ_COL_CHUNK = 512


def _norm_proj_kernel(x_ref, g_ref, w1_ref, *rest, has_f32):
    if has_f32:
        w2_ref, o1_ref, o2_ref = rest
    else:
        (o1_ref,) = rest
    hn = _rms(x_ref[...], g_ref[...]).astype(BF16)
    n1 = o1_ref.shape[1]
    for c in range(0, n1, PROJ_COL_CHUNK):
        o1_ref[:, c:c + PROJ_COL_CHUNK] = jnp.dot(
            hn, w1_ref[:, c:c + PROJ_COL_CHUNK], preferred_element_type=F32).astype(BF16)
    if has_f32:
        o2_ref[...] = jnp.dot(hn, w2_ref[...], preferred_element_type=F32)


def _norm_proj(x, g, w1, w2=None):
    n, d = x.shape
    n1 = w1.shape[1]
    in_specs = [
        pl.BlockSpec((ROW_TILE, d), lambda i: (i, 0)),
        pl.BlockSpec((1, d), lambda i: (0, 0)),
        pl.BlockSpec((d, n1), lambda i: (0, 0)),
    ]
    out_specs = [pl.BlockSpec((ROW_TILE, n1), lambda i: (i, 0))]
    out_shape = [jax.ShapeDtypeStruct((n, n1), BF16)]
    args = [x, g.reshape(1, d), w1]
    if w2 is not None:
        n2 = w2.shape[1]
        in_specs.append(pl.BlockSpec((d, n2), lambda i: (0, 0)))
        out_specs.append(pl.BlockSpec((ROW_TILE, n2), lambda i: (i, 0)))
        out_shape.append(jax.ShapeDtypeStruct((n, n2), F32))
        args.append(w2)
    return pl.pallas_call(
        functools.partial(_norm_proj_kernel, has_f32=w2 is not None),
        grid=(n // ROW_TILE,),
        in_specs=in_specs,
        out_specs=out_specs,
        out_shape=out_shape,
        compiler_params=_params("parallel"),
    )(*args)


def _out_proj_kernel(x_ref, o_ref, w_ref, y_ref):
    y_ref[...] = x_ref[...] + jnp.dot(o_ref[...], w_ref[...], preferred_element_type=F32)


def _out_proj(x, o, w):
    n, d = x.shape
    k = o.shape[1]
    return pl.pallas_call(
        _out_proj_kernel,
        grid=(n // ROW_TILE,),
        in_specs=[
            pl.BlockSpec((ROW_TILE, d), lambda i: (i, 0)),
            pl.BlockSpec((ROW_TILE, k), lambda i: (i, 0)),
            pl.BlockSpec((k, d), lambda i: (0, 0)),
        ],
        out_specs=pl.BlockSpec((ROW_TILE, d), lambda i: (i, 0)),
        out_shape=jax.ShapeDtypeStruct((n, d), F32),
        compiler_params=_params("parallel"),
    )(x, o, w)


def _compress_kernel(x_ref, pos_ref, w1_ref, w2_ref, o_ref):
    x = x_ref[0, 0, 0]
    pos = pos_ref[0]
    half = x.shape[1]
    top = jnp.dot((x + pos[0:1]).astype(BF16), w1_ref[0, :half, :], preferred_element_type=F32)
    bot = jnp.dot((x + pos[1:2]).astype(BF16), w1_ref[0, half:, :], preferred_element_type=F32)
    pre = top + pltpu.roll(bot, bot.shape[0] - 1, axis=0)
    hid = _silu(pre).astype(BF16)
    o_ref[0, 0, 0] = jnp.dot(hid, w2_ref[0], preferred_element_type=F32).astype(BF16)


def _compress(xh, pos2, w1, w2):
    b, _, g, nh, hw = xh.shape
    hid = w1.shape[2]
    return pl.pallas_call(
        _compress_kernel,
        grid=(b, 2, g),
        in_specs=[
            pl.BlockSpec((1, 1, 1, nh, hw), lambda i, j, k: (i, j, k, 0, 0)),
            pl.BlockSpec((1, 2, hw), lambda i, j, k: (j, 0, 0)),
            pl.BlockSpec((1, 2 * hw, hid), lambda i, j, k: (j, 0, 0)),
            pl.BlockSpec((1, hid, HEAD_DIM), lambda i, j, k: (j, 0, 0)),
        ],
        out_specs=pl.BlockSpec((1, 1, 1, nh, HEAD_DIM), lambda i, j, k: (i, j, k, 0, 0)),
        out_shape=jax.ShapeDtypeStruct((b, 2, g, nh, HEAD_DIM), BF16),
        compiler_params=_params("parallel", "parallel", "parallel"),
    )(xh, pos2, w1, w2)


def _split3(x):
    p1 = x.astype(BF16)
    r1 = x - p1.astype(F32)
    p2 = r1.astype(BF16)
    p3 = (r1 - p2.astype(F32)).astype(BF16)
    return p1, p2, p3


def _cmp_sel_kernel(q_ref, kc_ref, vct_ref, tb_ref, ovt_ref, oct_ref, sel_ref):
    qi = pl.program_id(2)
    tq = CMP_Q_TILE
    nrep = q_ref.shape[2]
    n_cmp_pad = kc_ref.shape[2]
    scale = HEAD_DIM ** -0.5
    q = (q_ref[0, 0] * scale).reshape(nrep * tq, HEAD_DIM)
    s = lax.dot_general(kc_ref[0, 0], q, _NT, preferred_element_type=F32)
    off = pl.multiple_of(n_cmp_pad - (tq // CMP_STRIDE) * qi, 8)
    bias = jnp.concatenate([tb_ref[r, pl.ds(off, n_cmp_pad), :] for r in range(nrep)], axis=1)
    s = s + bias
    m = jnp.max(s, axis=0, keepdims=True)
    e = jnp.exp(s - m)
    p = jnp.where(bias > 0.5 * NEG_INF, e / jnp.sum(e, axis=0, keepdims=True), 0.0)

    oct = jnp.dot(vct_ref[0, 0], p.astype(BF16), preferred_element_type=F32)
    for r in range(nrep):
        oct_ref[0, 0, r] = oct[:, r * tq:(r + 1) * tq]

    psum = p[:, 0:tq]
    for r in range(1, nrep):
        psum = psum + p[:, r * tq:(r + 1) * tq]
    ovt = ovt_ref[...]
    imp = sum(jnp.dot(ovt, part, preferred_element_type=F32) for part in _split3(psum))

    n_sel = imp.shape[0]
    blk = lax.broadcasted_iota(jnp.int32, (n_sel, tq), 0)
    cur = jnp.right_shift(qi * tq + lax.broadcasted_iota(jnp.int32, (n_sel, tq), 1), SEL_BLOCK_SHIFT)
    forced = (blk == 0) | (blk == cur) | (blk == cur - 1)
    score = jnp.where(blk > cur, -SEL_FORCE, imp + jnp.where(forced, SEL_FORCE, 0.0))
    cnt = jnp.zeros((n_sel, tq), F32)
    for sp in range(n_sel):
        row = score[sp:sp + 1, :]
        tie = jnp.where(blk > sp, 1.0, 0.0)
        cnt = cnt + jnp.where(row > score, 1.0, jnp.where(row == score, tie, 0.0))
    sel_ref[0, 0] = jnp.where(cnt < min(SEL_TOPK, n_sel), 1.0, 0.0).astype(BF16)


def _cmp_sel(q5, kcmp, vcmp_t, tb, ovt):
    b, g, r, t, dh = q5.shape
    n_pad = kcmp.shape[2]
    n_sel = ovt.shape[0]
    tq = CMP_Q_TILE
    return pl.pallas_call(
        _cmp_sel_kernel,
        grid=(b, g, t // tq),
        in_specs=[
            pl.BlockSpec((1, 1, r, tq, dh), lambda i, j, k: (i, j, 0, k, 0)),
            pl.BlockSpec((1, 1, n_pad, dh), lambda i, j, k: (i, j, 0, 0)),
            pl.BlockSpec((1, 1, dh, n_pad), lambda i, j, k: (i, j, 0, 0)),
            pl.BlockSpec((r, 2 * n_pad, tq), lambda i, j, k: (j, 0, 0)),
            pl.BlockSpec((n_sel, n_pad), lambda i, j, k: (0, 0)),
        ],
        out_specs=[
            pl.BlockSpec((1, 1, r, dh, tq), lambda i, j, k: (i, j, 0, 0, k)),
            pl.BlockSpec((1, 1, n_sel, tq), lambda i, j, k: (i, j, 0, k)),
        ],
        out_shape=[
            jax.ShapeDtypeStruct((b, g, r, dh, t), F32),
            jax.ShapeDtypeStruct((b, g, n_sel, t), BF16),
        ],
        compiler_params=_params("parallel", "parallel", "parallel"),
    )(q5, kcmp, vcmp_t, tb, ovt)


def _softmax_pv(s, v_aug):
    m = jnp.max(s, axis=1, keepdims=True)
    p = jnp.exp(s - m).astype(BF16)
    return m, jnp.dot(p, v_aug, preferred_element_type=F32)


def _sel_win_kernel(q_ref, ks_ref, vs_ref, sel_ref, st_ref, kw_ref, vw_ref, wt_ref, gate_ref, oc_ref, o_ref):
    qi = pl.program_id(2)
    tq, tk = ATT_Q_TILE, ATT_K_TILE
    nrep = q_ref.shape[2]
    scale = HEAD_DIM ** -0.5
    near = st_ref.shape[3]
    wlen = wt_ref.shape[3]
    t0 = pl.multiple_of(qi * tq, tq)

    mq = ((sel_ref[0, 0].astype(F32) - 1.0) * (-NEG_INF)).astype(BF16)
    gates = jax.nn.sigmoid(gate_ref[0, 0])

    outs = []
    for r in range(nrep):
        qr = q_ref[0, 0, r] * scale
        qaug = jnp.concatenate([qr, mq], axis=1)

        s = lax.dot_general(qaug, ks_ref[0, 0, pl.ds(t0, near), :], _NT, preferred_element_type=F32)
        m, acc = _softmax_pv(s + st_ref[0, r], vs_ref[0, 0, pl.ds(t0, near), :])

        def far(c, carry):
            m, acc = carry
            start = pl.multiple_of(tk + c * tk, tk)
            s = lax.dot_general(qaug, ks_ref[0, 0, pl.ds(start, tk), :], _NT, preferred_element_type=F32)
            m_new = jnp.maximum(m, jnp.max(s, axis=1, keepdims=True))
            p = jnp.exp(s - m_new).astype(BF16)
            acc = jnp.exp(m - m_new) * acc + jnp.dot(p, vs_ref[0, 0, pl.ds(start, tk), :],
                                                     preferred_element_type=F32)
            return m_new, acc

        m, acc = lax.fori_loop(0, jnp.maximum(qi - 1, 0), far, (m, acc))
        o_s = acc[:, :HEAD_DIM] / acc[:, HEAD_DIM:HEAD_DIM + 1]

        s = lax.dot_general(qr, kw_ref[0, 0, pl.ds(t0, wlen), :], _NT, preferred_element_type=F32)
        _, accw = _softmax_pv(s + wt_ref[0, r], vw_ref[0, 0, pl.ds(t0, wlen), :])
        o_w = accw[:, :HEAD_DIM] / accw[:, HEAD_DIM:HEAD_DIM + 1]

        g3 = gates[:, 3 * r:3 * r + 3]
        outs.append(g3[:, 0:1] * oc_ref[0, 0, r] + g3[:, 1:2] * o_s + g3[:, 2:3] * o_w)
    o_ref[0] = jnp.concatenate(outs, axis=1).astype(BF16)


def _sel_win(q5, ks_aug, vs_aug, sel, st, kw_pad, vw_aug, wt, gates, oc5):
    b, g, r, t, dh = q5.shape
    tq = ATT_Q_TILE
    n_sel = sel.shape[3]
    tks, tkw = ks_aug.shape[2], kw_pad.shape[2]
    near, wlen = st.shape[3], wt.shape[3]
    return pl.pallas_call(
        _sel_win_kernel,
        grid=(b, g, t // tq),
        in_specs=[
            pl.BlockSpec((1, 1, r, tq, dh), lambda i, j, k: (i, j, 0, k, 0)),
            pl.BlockSpec((1, 1, tks, 2 * dh), lambda i, j, k: (i, j, 0, 0)),
            pl.BlockSpec((1, 1, tks, 2 * dh), lambda i, j, k: (i, j, 0, 0)),
            pl.BlockSpec((1, 1, tq, n_sel), lambda i, j, k: (i, j, k, 0)),
            pl.BlockSpec((1, r, tq, near), lambda i, j, k: (jnp.minimum(k, st.shape[0] - 1), j, 0, 0)),
            pl.BlockSpec((1, 1, tkw, dh), lambda i, j, k: (i, j, 0, 0)),
            pl.BlockSpec((1, 1, tkw, 2 * dh), lambda i, j, k: (i, j, 0, 0)),
            pl.BlockSpec((1, r, tq, wlen), lambda i, j, k: (jnp.minimum(k, wt.shape[0] - 1), j, 0, 0)),
            pl.BlockSpec((1, 1, tq, 3 * r), lambda i, j, k: (i, j, k, 0)),
            pl.BlockSpec((1, 1, r, tq, dh), lambda i, j, k: (i, j, 0, k, 0)),
        ],
        out_specs=pl.BlockSpec((1, tq, r * dh), lambda i, j, k: (i, k, j)),
        out_shape=jax.ShapeDtypeStruct((b, t, g * r * dh), BF16),
        compiler_params=_params("parallel", "parallel", "parallel"),
    )(q5, ks_aug, vs_aug, sel, st, kw_pad, vw_aug, wt, gates, oc5)


def _forget_cumsum_kernel(fl_ref, bf_ref, c_ref):
    x = fl_ref[0] + bf_ref[...]
    y = jnp.minimum(x, 0.0) - jnp.log1p(jnp.exp(-jnp.abs(x)))
    t = y.shape[1]
    lane = lax.broadcasted_iota(jnp.int32, y.shape, 1)
    k = 1
    while k < t:
        y = y + jnp.where(lane >= k, pltpu.roll(y, k, axis=1), 0.0)
        k *= 2
    c_ref[0] = y


def _forget_cumsum(fl_t, b_f):
    b, h, t = fl_t.shape
    return pl.pallas_call(
        _forget_cumsum_kernel,
        grid=(b,),
        in_specs=[pl.BlockSpec((1, h, t), lambda i: (i, 0, 0)), pl.BlockSpec((h, 1), lambda i: (0, 0))],
        out_specs=pl.BlockSpec((1, h, t), lambda i: (i, 0, 0)),
        out_shape=jax.ShapeDtypeStruct((b, h, t), F32),
        compiler_params=_params("parallel"),
    )(fl_t, b_f.reshape(h, 1))


FOX_HEADS_PER_STEP = 2


def _fox_kernel(q_ref, k_ref, v_ref, cq_ref, ck_ref, o_ref):
    qi = pl.program_id(2)
    tq, tk = ATT_Q_TILE, ATT_K_TILE
    scale = HEAD_DIM ** -0.5
    t0 = pl.multiple_of(qi * tq, tq)
    causal = (lax.broadcasted_iota(jnp.int32, (tq, tk), 0) >= lax.broadcasted_iota(jnp.int32, (tq, tk), 1))
    outs = []
    for hh in range(FOX_HEADS_PER_STEP):
        qh = q_ref[0, hh] * scale
        cq = cq_ref[0, hh]

        def logits(start, c):
            s = lax.dot_general(qh, k_ref[0, hh, pl.ds(start, tk), :], _NT, preferred_element_type=F32)
            return s + cq - ck_ref[0, hh, c]

        s = jnp.where(causal, logits(t0, qi), NEG_INF)
        m, acc = _softmax_pv(s, v_ref[0, hh, pl.ds(t0, tk), :])

        def far(c, carry):
            m, acc = carry
            start = pl.multiple_of(c * tk, tk)
            s = logits(start, c)
            m_new = jnp.maximum(m, jnp.max(s, axis=1, keepdims=True))
            p = jnp.exp(s - m_new).astype(BF16)
            acc = jnp.exp(m - m_new) * acc + jnp.dot(p, v_ref[0, hh, pl.ds(start, tk), :],
                                                     preferred_element_type=F32)
            return m_new, acc

        m, acc = lax.fori_loop(0, qi, far, (m, acc))
        outs.append(acc[:, :HEAD_DIM] / acc[:, HEAD_DIM:HEAD_DIM + 1])
    o_ref[0] = jnp.concatenate(outs, axis=1).astype(BF16)


def _fox_attention(q4, k4, v_aug, cq, ck):
    b, h, t, dh = q4.shape
    tq, hp = ATT_Q_TILE, FOX_HEADS_PER_STEP
    nkt = ck.shape[2]
    return pl.pallas_call(
        _fox_kernel,
        grid=(b, h // hp, t // tq),
        in_specs=[
            pl.BlockSpec((1, hp, tq, dh), lambda i, j, k: (i, j, k, 0)),
            pl.BlockSpec((1, hp, t, dh), lambda i, j, k: (i, j, 0, 0)),
            pl.BlockSpec((1, hp, t, 2 * dh), lambda i, j, k: (i, j, 0, 0)),
            pl.BlockSpec((1, hp, tq, 1), lambda i, j, k: (i, j, k, 0)),
            pl.BlockSpec((1, hp, nkt, 1, ATT_K_TILE), lambda i, j, k: (i, j, 0, 0, 0)),
        ],
        out_specs=pl.BlockSpec((1, tq, hp * dh), lambda i, j, k: (i, k, j)),
        out_shape=jax.ShapeDtypeStruct((b, t, h * dh), BF16),
        compiler_params=_params("parallel", "parallel", "parallel"),
    )(q4, k4, v_aug, cq, ck)


def _t5_bucket_np(dist):
    dist = np.maximum(dist, 0)
    max_exact = N_BUCKETS // 2
    ratio = np.maximum(dist, 1).astype(np.float32) / np.float32(max_exact)
    scaled = np.log(ratio) / np.float32(math.log(MAX_DISTANCE / max_exact))
    large = np.minimum(max_exact + (scaled * np.float32(N_BUCKETS - max_exact)).astype(np.int32), N_BUCKETS - 1)
    return np.where(dist < max_exact, dist, large).astype(np.int32)


_MASKED = MAX_DISTANCE + 1


def _dist_index(dist, valid):
    return np.where(valid, np.minimum(dist, MAX_DISTANCE), _MASKED).astype(np.int32)


def _cmp_table_index(n_pad, tq):
    u = np.arange(2 * n_pad)[:, None] - n_pad
    dist = np.arange(tq)[None, :] - CMP_STRIDE * u - (CMP_BLOCK - 1)
    return _dist_index(dist, dist >= 0)


def _near_table_index(tq, tk):
    i = np.arange(tq)[:, None]
    j = np.arange(tk + tq)[None, :]
    dist = i + tk - j
    first = (dist >= 0) & (j >= tk)
    return np.stack([_dist_index(dist, first), _dist_index(dist, dist >= 0)])


def _window_table_index(tq):
    i = np.arange(tq)[:, None]
    j = np.arange(WINDOW + tq)[None, :]
    dist = i + WINDOW - j
    band = (dist >= 0) & (dist < WINDOW)
    n_var = WINDOW // tq + 1
    return np.stack([_dist_index(dist, band & (j >= WINDOW - v * tq)) for v in range(n_var)])


def _bias_tables(rel_bias, n_pad):
    bucket = _t5_bucket_np(np.arange(MAX_DISTANCE + 1))
    by_dist = jnp.concatenate([rel_bias[bucket], jnp.full((1, N_HEADS), NEG_INF, F32)], axis=0)
    shifted = jnp.concatenate([by_dist[:-1] - by_dist[MAX_DISTANCE], by_dist[-1:]], axis=0)
    tb = jnp.moveaxis(by_dist[_cmp_table_index(n_pad, CMP_Q_TILE)], -1, 0)
    st = jnp.moveaxis(shifted[_near_table_index(ATT_Q_TILE, ATT_K_TILE)], -1, 1)
    wt = jnp.moveaxis(by_dist[_window_table_index(ATT_Q_TILE)], -1, 1)
    return tb, st, wt


def _overlap_t(n_pad, n_sel):
    n_cmp = n_pad - 1
    cmp_start = np.arange(n_pad) * CMP_STRIDE
    sel_start = np.arange(n_sel) * SEL_BLOCK
    ov = ((cmp_start[None, :] < sel_start[:, None] + SEL_BLOCK)
          & (cmp_start[None, :] + CMP_BLOCK > sel_start[:, None])
          & (np.arange(n_pad)[None, :] < n_cmp))
    return jnp.asarray(ov.astype(np.float32), BF16)


def _append_ones(v):
    one = jnp.ones(v.shape[:-1] + (1,), v.dtype)
    zero = jnp.zeros(v.shape[:-1] + (v.shape[-1] - 1,), v.dtype)
    return jnp.concatenate([v, one, zero], axis=-1)


def _pad_front(a, n):
    return jnp.pad(a, ((0, 0), (0, 0), (n, 0), (0, 0)))


def _nsa_mixer(x, g, w_in, cmp_pos, cmp_w1, cmp_w2, rel_bias, w_out, batch, seq):
    n, d = x.shape
    G, R, dh = NSA_KV_HEADS, NSA_GROUP, HEAD_DIM
    hd, kvd = N_HEADS * dh, G * dh
    cols = [hd + i * kvd for i in range(7)]
    wq, wkc, wvc, wks, wvs, wkw, wvw, wgl = jnp.split(w_in, cols, axis=1)
    n_gate = wgl.shape[1]
    w1 = jnp.concatenate([wq, wks, wvs, wkw, wvw], axis=1).astype(BF16)
    w2 = jnp.concatenate([wkc, wvc, wgl, jnp.zeros((d, -n_gate % 128), w_in.dtype)], axis=1).astype(BF16)
    a, f = _norm_proj(x, g, w1, w2)

    q5 = a[:, :hd].reshape(batch, seq, G, R, dh).transpose(0, 2, 3, 1, 4)
    ks, vs, kw, vw = (a[:, hd + i * kvd:hd + (i + 1) * kvd].reshape(batch, seq, G, dh).transpose(0, 2, 1, 3)
                      for i in range(4))
    n_sel = seq // SEL_BLOCK
    blk_onehot = jnp.asarray(np.arange(seq)[:, None] // SEL_BLOCK == np.arange(n_sel)[None, :], BF16)
    ks_aug = _pad_front(jnp.concatenate([ks, jnp.broadcast_to(blk_onehot, ks.shape[:2] + blk_onehot.shape)],
                                        axis=-1), ATT_K_TILE)
    vs_aug = _pad_front(_append_ones(vs), ATT_K_TILE)
    kw_pad = _pad_front(kw, WINDOW)
    vw_aug = _pad_front(_append_ones(vw), WINDOW)
    gates = f[:, 2 * kvd:2 * kvd + n_gate].reshape(batch, seq, G, R * 3).transpose(0, 2, 1, 3)

    nh = seq // CMP_STRIDE
    xh = f[:, :2 * kvd].reshape(batch, nh, CMP_STRIDE, 2, G, dh).transpose(0, 3, 4, 1, 2, 5)
    xh = xh.reshape(batch, 2, G, nh, CMP_STRIDE * dh)
    cmp = _compress(xh, cmp_pos.reshape(2, 2, CMP_STRIDE * dh), cmp_w1.astype(BF16), cmp_w2.astype(BF16))
    kcmp = cmp[:, 0]
    vcmp_t = cmp[:, 1].transpose(0, 1, 3, 2)

    tb, st, wt = _bias_tables(rel_bias, nh)
    oct, sel_t = _cmp_sel(q5, kcmp, vcmp_t, tb, _overlap_t(nh, n_sel))
    oc5 = oct.transpose(0, 1, 2, 4, 3)
    sel = sel_t.transpose(0, 1, 3, 2)
    o = _sel_win(q5, ks_aug, vs_aug, sel, st, kw_pad, vw_aug, wt, gates, oc5)
    return _out_proj(x, o.reshape(n, hd), w_out.astype(BF16))


def _shared_kv(x, g, w_kvf, b_f, batch, seq):
    hd = N_HEADS * HEAD_DIM
    w1 = w_kvf[:, :2 * hd].astype(BF16)
    wf = w_kvf[:, 2 * hd:]
    w2 = jnp.concatenate([wf, jnp.zeros((wf.shape[0], -wf.shape[1] % 128), wf.dtype)], axis=1).astype(BF16)
    a, f = _norm_proj(x, g, w1, w2)
    k4, v4 = (a[:, i * hd:(i + 1) * hd].reshape(batch, seq, N_HEADS, HEAD_DIM).transpose(0, 2, 1, 3)
              for i in range(2))
    fl_t = f[:, :N_HEADS].reshape(batch, seq, N_HEADS).transpose(0, 2, 1)
    c_t = _forget_cumsum(fl_t, b_f.astype(F32))
    cq = c_t[..., None]
    ck = c_t.reshape(batch, N_HEADS, seq // ATT_K_TILE, 1, ATT_K_TILE)
    return k4, _append_ones(v4), cq, ck


def _fox_mixer(x, g, w_q, kv, w_out, batch, seq):
    n, _ = x.shape
    hd = N_HEADS * HEAD_DIM
    (q,) = _norm_proj(x, g, w_q.astype(BF16))
    q4 = q.reshape(batch, seq, N_HEADS, HEAD_DIM).transpose(0, 2, 1, 3)
    o = _fox_attention(q4, *kv)
    return _out_proj(x, o.reshape(n, hd), w_out.astype(BF16))


def kernel(x, norm_g, ffn_w_in, ffn_w_out, nsa_w_in, nsa_cmp_pos, nsa_cmp_w1, nsa_cmp_w2, nsa_w_out, rel_bias,
           kv_norm_g, fox_w_kvf, fox_b_f, fox_w_q, fox_w_out, final_g):
    batch, seq, d = x.shape
    depth = norm_g.shape[0]
    n_a = nsa_w_in.shape[0]
    xf = x.reshape(batch * seq, d)
    w_in = ffn_w_in.astype(BF16)
    w_out = ffn_w_out.astype(BF16)
    kv = None
    for l in range(depth):
        if l == n_a:
            kv = _shared_kv(xf, kv_norm_g, fox_w_kvf, fox_b_f, batch, seq)
        xf = _ffn(xf, norm_g[l, 0], w_in[l, 0], w_out[l, 0])
        if l < n_a:
            xf = _nsa_mixer(xf, norm_g[l, 1], nsa_w_in[l], nsa_cmp_pos[l], nsa_cmp_w1[l], nsa_cmp_w2[l],
                            rel_bias, nsa_w_out[l], batch, seq)
        else:
            xf = _fox_mixer(xf, norm_g[l, 1], fox_w_q[l - n_a], kv, fox_w_out[l - n_a], batch, seq)
        xf = _ffn(xf, norm_g[l, 2], w_in[l, 1], w_out[l, 1], final_g if l == depth - 1 else None)
    return xf.reshape(batch, seq, d)
```

```python
import functools
import math

import numpy as np
import jax
import jax.numpy as jnp
from jax import lax
from jax.experimental import pallas as pl
from jax.experimental.pallas import tpu as pltpu

N_HEADS = 16
HEAD_DIM = 64
NSA_KV_HEADS = 4
NSA_GROUP = N_HEADS // NSA_KV_HEADS
CMP_BLOCK = 32
CMP_STRIDE = 16
SEL_BLOCK = 64
SEL_BLOCK_SHIFT = SEL_BLOCK.bit_length() - 1
assert 1 << SEL_BLOCK_SHIFT == SEL_BLOCK
SEL_TOPK = 16
WINDOW = 512
N_BUCKETS = 32
MAX_DISTANCE = 128
EPS = 1e-6
NEG_INF = -1e30
SEL_FORCE = 1e4

F32 = jnp.float32
BF16 = jnp.bfloat16

VMEM_LIMIT_BYTES = 48 * 1024 * 1024

ROW_TILE = 512
FF_TILE = 256
CMP_Q_TILE = 128
SEQ_TILE = 256
FEAT_ROWS = 16
ACC_ROWS = HEAD_DIM + FEAT_ROWS
KEY_FEATS = 128

_NT = (((1,), (1,)), ((), ()))
_TN = (((0,), (0,)), ((), ()))


def _params(*sem):
    return pltpu.CompilerParams(dimension_semantics=sem, vmem_limit_bytes=VMEM_LIMIT_BYTES)


def _rms(x, g):
    return x * lax.rsqrt(jnp.mean(x * x, axis=-1, keepdims=True) + EPS) * g


def _silu(a):
    return a * jax.nn.sigmoid(a)


def _ffn_kernel(x_ref, g_ref, wa_ref, wb_ref, wo_ref, *rest, final_norm):
    if final_norm:
        fg_ref, o_ref, hn_ref, acc_ref = rest
    else:
        o_ref, hn_ref, acc_ref = rest
    j = pl.program_id(1)

    @pl.when(j == 0)
    def _():
        hn_ref[...] = _rms(x_ref[...], g_ref[...]).astype(BF16)
        acc_ref[...] = jnp.zeros_like(acc_ref)

    hn = hn_ref[...]
    a = jnp.dot(hn, wa_ref[...], preferred_element_type=F32)
    b = jnp.dot(hn, wb_ref[...], preferred_element_type=F32)
    act = (_silu(a) * b).astype(BF16)
    acc_ref[...] += jnp.dot(act, wo_ref[...], preferred_element_type=F32)

    @pl.when(j == pl.num_programs(1) - 1)
    def _():
        y = x_ref[...] + 0.5 * acc_ref[...]
        if final_norm:
            y = _rms(y, fg_ref[...])
        o_ref[...] = y


def _ffn(x, g, w_in, w_out, final_g=None):
    n, d = x.shape
    d_ff = w_out.shape[0]
    nj = d_ff // FF_TILE
    in_specs = [
        pl.BlockSpec((ROW_TILE, d), lambda i, j: (i, 0)),
        pl.BlockSpec((1, d), lambda i, j: (0, 0)),
        pl.BlockSpec((d, FF_TILE), lambda i, j: (0, j)),
        pl.BlockSpec((d, FF_TILE), lambda i, j: (0, j + nj)),
        pl.BlockSpec((FF_TILE, d), lambda i, j: (j, 0)),
    ]
    args = [x, g.reshape(1, d), w_in, w_in, w_out]
    if final_g is not None:
        in_specs.append(pl.BlockSpec((1, d), lambda i, j: (0, 0)))
        args.append(final_g.reshape(1, d))
    return pl.pallas_call(
        functools.partial(_ffn_kernel, final_norm=final_g is not None),
        grid=(n // ROW_TILE, nj),
        in_specs=in_specs,
        out_specs=pl.BlockSpec((ROW_TILE, d), lambda i, j: (i, 0)),
        out_shape=jax.ShapeDtypeStruct((n, d), F32),
        scratch_shapes=[pltpu.VMEM((ROW_TILE, d), BF16), pltpu.VMEM((ROW_TILE, d), F32)],
        name="ffn",
        compiler_params=_params("parallel", "arbitrary"),
    )(*args)


PROJ_CHUNK = 512


def _norm_proj_kernel(x_ref, g_ref, wt_ref, *rest, has_rows):
    if has_rows:
        w2_ref, ot_ref, o2_ref = rest
    else:
        (ot_ref,) = rest
    hn = _rms(x_ref[...], g_ref[...]).astype(BF16)
    channels = ot_ref.shape[2]
    for c in range(0, channels, PROJ_CHUNK):
        r = lax.dot_general(wt_ref[c:c + PROJ_CHUNK, :], hn, _NT, preferred_element_type=F32).astype(BF16)
        for j in range(ot_ref.shape[1]):
            ot_ref[0, j, c:c + PROJ_CHUNK, :] = r[:, j * SEQ_TILE:(j + 1) * SEQ_TILE]
    if has_rows:
        o2_ref[...] = jnp.dot(hn, w2_ref[...], preferred_element_type=F32)


def _norm_proj(x, g, wt, w2, batch, seq):
    n, d = x.shape
    channels = wt.shape[0]
    per_seq = seq // ROW_TILE
    sub = ROW_TILE // SEQ_TILE
    in_specs = [
        pl.BlockSpec((ROW_TILE, d), lambda i: (i, 0)),
        pl.BlockSpec((1, d), lambda i: (0, 0)),
        pl.BlockSpec((channels, d), lambda i: (0, 0)),
    ]
    out_specs = [pl.BlockSpec((1, sub, channels, SEQ_TILE), lambda i: (i // per_seq, i % per_seq, 0, 0))]
    out_shape = [jax.ShapeDtypeStruct((batch, seq // SEQ_TILE, channels, SEQ_TILE), BF16)]
    args = [x, g.reshape(1, d), wt]
    if w2 is not None:
        n2 = w2.shape[1]
        in_specs.append(pl.BlockSpec((d, n2), lambda i: (0, 0)))
        out_specs.append(pl.BlockSpec((ROW_TILE, n2), lambda i: (i, 0)))
        out_shape.append(jax.ShapeDtypeStruct((n, n2), F32))
        args.append(w2)
    return pl.pallas_call(
        functools.partial(_norm_proj_kernel, has_rows=w2 is not None),
        grid=(n // ROW_TILE,),
        in_specs=in_specs,
        out_specs=out_specs,
        out_shape=out_shape,
        name="norm_proj",
        compiler_params=_params("parallel"),
    )(*args)


def _out_proj_kernel(x_ref, o_ref, w_ref, y_ref):
    y_ref[...] = x_ref[...] + jnp.dot(o_ref[...], w_ref[...], preferred_element_type=F32)


def _out_proj(x, o, w):
    n, d = x.shape
    k = o.shape[1]
    return pl.pallas_call(
        _out_proj_kernel,
        grid=(n // ROW_TILE,),
        in_specs=[
            pl.BlockSpec((ROW_TILE, d), lambda i: (i, 0)),
            pl.BlockSpec((ROW_TILE, k), lambda i: (i, 0)),
            pl.BlockSpec((k, d), lambda i: (0, 0)),
        ],
        out_specs=pl.BlockSpec((ROW_TILE, d), lambda i: (i, 0)),
        out_shape=jax.ShapeDtypeStruct((n, d), F32),
        name="out_proj",
        compiler_params=_params("parallel"),
    )(x, o, w)


def _compress_kernel(x_ref, pos_ref, w1_ref, w2_ref, o_ref):
    x = x_ref[0, 0, 0]
    pos = pos_ref[0]
    half = x.shape[1]
    top = jnp.dot((x + pos[0:1]).astype(BF16), w1_ref[0, :half, :], preferred_element_type=F32)
    bot = jnp.dot((x + pos[1:2]).astype(BF16), w1_ref[0, half:, :], preferred_element_type=F32)
    pre = top + pltpu.roll(bot, bot.shape[0] - 1, axis=0)
    hid = _silu(pre).astype(BF16)
    o_ref[0, 0, 0] = jnp.dot(hid, w2_ref[0], preferred_element_type=F32).astype(BF16)


def _compress(xh, pos2, w1, w2):
    b, _, g, nh, hw = xh.shape
    hid = w1.shape[2]
    return pl.pallas_call(
        _compress_kernel,
        grid=(b, 2, g),
        in_specs=[
            pl.BlockSpec((1, 1, 1, nh, hw), lambda i, j, k: (i, j, k, 0, 0)),
            pl.BlockSpec((1, 2, hw), lambda i, j, k: (j, 0, 0)),
            pl.BlockSpec((1, 2 * hw, hid), lambda i, j, k: (j, 0, 0)),
            pl.BlockSpec((1, hid, HEAD_DIM), lambda i, j, k: (j, 0, 0)),
        ],
        out_specs=pl.BlockSpec((1, 1, 1, nh, HEAD_DIM), lambda i, j, k: (i, j, k, 0, 0)),
        out_shape=jax.ShapeDtypeStruct((b, 2, g, nh, HEAD_DIM), BF16),
        name="compress",
        compiler_params=_params("parallel", "parallel", "parallel"),
    )(xh, pos2, w1, w2)


def _split3(x):
    p1 = x.astype(BF16)
    r1 = x - p1.astype(F32)
    p2 = r1.astype(BF16)
    p3 = (r1 - p2.astype(F32)).astype(BF16)
    return p1, p2, p3


def _cmp_sel_kernel(q_ref, kc_ref, vct_ref, tb_ref, ovt_ref, oct_ref, sel_ref):
    qi = pl.program_id(2)
    tq = CMP_Q_TILE
    nrep = NSA_GROUP
    n_cmp_pad = kc_ref.shape[2]
    scale = HEAD_DIM ** -0.5
    qt = jnp.concatenate([q_ref[0, 0, r * HEAD_DIM:(r + 1) * HEAD_DIM, :] for r in range(nrep)], axis=1) * scale
    s = jnp.dot(kc_ref[0, 0], qt, preferred_element_type=F32)
    off = pl.multiple_of(n_cmp_pad - (tq // CMP_STRIDE) * qi, 8)
    bias = jnp.concatenate([tb_ref[r, pl.ds(off, n_cmp_pad), :] for r in range(nrep)], axis=1)
    s = s + bias
    m = jnp.max(s, axis=0, keepdims=True)
    e = jnp.exp(s - m)
    p = jnp.where(bias > 0.5 * NEG_INF, e / jnp.sum(e, axis=0, keepdims=True), 0.0)

    oct = jnp.dot(vct_ref[0, 0], p.astype(BF16), preferred_element_type=F32)
    for r in range(nrep):
        oct_ref[0, 0, r] = oct[:, r * tq:(r + 1) * tq]

    psum = p[:, 0:tq]
    for r in range(1, nrep):
        psum = psum + p[:, r * tq:(r + 1) * tq]
    ovt = ovt_ref[...]
    imp = sum(jnp.dot(ovt, part, preferred_element_type=F32) for part in _split3(psum))

    n_sel = imp.shape[0]
    blk = lax.broadcasted_iota(jnp.int32, (n_sel, tq), 0)
    cur = jnp.right_shift(qi * tq + lax.broadcasted_iota(jnp.int32, (n_sel, tq), 1), SEL_BLOCK_SHIFT)
    forced = (blk == 0) | (blk == cur) | (blk == cur - 1)
    score = jnp.where(blk > cur, -SEL_FORCE, imp + jnp.where(forced, SEL_FORCE, 0.0))
    cnt = jnp.zeros((n_sel, tq), F32)
    for sp in range(n_sel):
        row = score[sp:sp + 1, :]
        tie = jnp.where(blk > sp, 1.0, 0.0)
        cnt = cnt + jnp.where(row > score, 1.0, jnp.where(row == score, tie, 0.0))
    sel_ref[0, 0] = jnp.where(cnt < min(SEL_TOPK, n_sel), 1.0, 0.0).astype(BF16)


def _cmp_sel(qkv_t, kcmp, vcmp_t, tb, ovt):
    b, nt, _, _ = qkv_t.shape
    g, r, dh = NSA_KV_HEADS, NSA_GROUP, HEAD_DIM
    t = nt * SEQ_TILE
    n_pad = kcmp.shape[2]
    n_sel = ovt.shape[0]
    tq = CMP_Q_TILE
    sub = SEQ_TILE // tq
    return pl.pallas_call(
        _cmp_sel_kernel,
        grid=(b, g, t // tq),
        in_specs=[
            pl.BlockSpec((1, 1, r * dh, tq), lambda i, j, k: (i, k // sub, j, k % sub)),
            pl.BlockSpec((1, 1, n_pad, dh), lambda i, j, k: (i, j, 0, 0)),
            pl.BlockSpec((1, 1, dh, n_pad), lambda i, j, k: (i, j, 0, 0)),
            pl.BlockSpec((r, 2 * n_pad, tq), lambda i, j, k: (j, 0, 0)),
            pl.BlockSpec((n_sel, n_pad), lambda i, j, k: (0, 0)),
        ],
        out_specs=[
            pl.BlockSpec((1, 1, r, dh, tq), lambda i, j, k: (i, j, 0, 0, k)),
            pl.BlockSpec((1, 1, n_sel, tq), lambda i, j, k: (i, j, 0, k)),
        ],
        out_shape=[
            jax.ShapeDtypeStruct((b, g, r, dh, t), F32),
            jax.ShapeDtypeStruct((b, g, n_sel, t), BF16),
        ],
        name="cmp_sel",
        compiler_params=_params("parallel", "parallel", "parallel"),
    )(qkv_t, kcmp, vcmp_t, tb, ovt)


def _ones_rows(width):
    return jnp.ones((FEAT_ROWS, width), BF16)


def _with_ones(v_t):
    return jnp.concatenate([v_t, _ones_rows(v_t.shape[1])], axis=0)


def _online_update(m_ref, acc_ref, slot, tiles, first):
    mt = None
    for s, _ in tiles:
        cm = jnp.max(s, axis=0, keepdims=True)
        mt = cm if mt is None else jnp.maximum(mt, cm)
    m_old = None if first else m_ref[slot]
    m_new = mt if first else jnp.maximum(m_old, mt)
    pv = None
    for s, va in tiles:
        d = jnp.dot(va, jnp.exp(s - m_new).astype(BF16), preferred_element_type=F32)
        pv = d if pv is None else pv + d
    acc_ref[slot] = pv if first else jnp.exp(m_old - m_new) * acc_ref[slot] + pv
    m_ref[slot] = m_new


def _pipelined_updates(m_ref, acc_ref, jobs, first=False):
    pending = jobs[0][1]()
    for i, (slot, _, values_fn) in enumerate(jobs):
        current = pending
        if i + 1 < len(jobs):
            pending = jobs[i + 1][1]()
        _online_update(m_ref, acc_ref, slot, list(zip(current, values_fn())), first)


def _normalised(acc_ref, slot):
    acc = acc_ref[slot]
    return acc[:HEAD_DIM] / acc[HEAD_DIM:HEAD_DIM + 1]


def _sel_win_kernel(q_ref, ks_ref, vs_ref, kw_ref, vw_ref, sel_ref, oh_ref, tab_ref, far2_ref, gate_ref, oc_ref,
                    o_ref, ksrm_ref, kwrm_ref, m_ref, acc_ref):
    qi = pl.program_id(2)
    tq = SEQ_TILE
    nrep = NSA_GROUP
    scale = HEAD_DIM ** -0.5
    pad = jnp.zeros((KEY_FEATS - HEAD_DIM, tq), BF16)

    @pl.when(qi == 0)
    def _():
        def to_rows(c, carry):
            ksa = jnp.concatenate([ks_ref[0, c], oh_ref[c]], axis=0)
            ksrm_ref[c] = ksa.astype(F32).T.astype(BF16)
            kwa = jnp.concatenate([kw_ref[0, c], pad], axis=0)
            kwrm_ref[c] = kwa.astype(F32).T.astype(BF16)
            return carry

        lax.fori_loop(0, ks_ref.shape[1], to_rows, 0)

    mq = ((sel_ref[0, 0].astype(F32) - 1.0) * (-NEG_INF)).astype(BF16)
    qs = [q_ref[0, 0, r * HEAD_DIM:(r + 1) * HEAD_DIM, :] * scale for r in range(nrep)]
    q_sel = [jnp.concatenate([q, mq], axis=0) for q in qs]
    q_win = [jnp.concatenate([q, pad], axis=0) for q in qs]

    def sel_jobs(cs, bias=None):
        def logits(r):
            tiles = [jnp.dot(ksrm_ref[c], q_sel[r], preferred_element_type=F32) for c in cs]
            return tiles if bias is None else [s + bias(r) for s in tiles]
        return [(r, functools.partial(logits, r), lambda: [_with_ones(vs_ref[0, c]) for c in cs])
                for r in range(nrep)]

    def win_jobs(c, bias):
        def logits(r):
            return [jnp.dot(kwrm_ref[c], q_win[r], preferred_element_type=F32) + bias(r)]
        return [(nrep + r, functools.partial(logits, r), lambda: [_with_ones(vw_ref[0, c])])
                for r in range(nrep)]

    diag = lambda r: tab_ref[r, 1]
    prev = lambda r: tab_ref[r, 0]
    _pipelined_updates(m_ref, acc_ref, sel_jobs([qi], diag) + win_jobs(qi, diag), first=True)

    @pl.when(qi >= 1)
    def _():
        _pipelined_updates(m_ref, acc_ref, sel_jobs([qi - 1], prev) + win_jobs(qi - 1, prev))

    @pl.when(qi >= 2)
    def _():
        _pipelined_updates(m_ref, acc_ref, win_jobs(qi - 2, lambda r: far2_ref[...]))

    n_far = jnp.maximum(qi - 1, 0)

    @pl.when(n_far % 2 == 1)
    def _():
        _pipelined_updates(m_ref, acc_ref, sel_jobs([n_far - 1]))

    def far_pair(c2, carry):
        _pipelined_updates(m_ref, acc_ref, sel_jobs([2 * c2, 2 * c2 + 1]))
        return carry

    lax.fori_loop(0, n_far // 2, far_pair, 0)

    gates = jax.nn.sigmoid(gate_ref[0, 0])
    outs = []
    for r in range(nrep):
        outs.append(gates[3 * r:3 * r + 1] * oc_ref[0, 0, r]
                    + gates[3 * r + 1:3 * r + 2] * _normalised(acc_ref, r)
                    + gates[3 * r + 2:3 * r + 3] * _normalised(acc_ref, nrep + r))
    o_ref[0] = jnp.concatenate(outs, axis=0).T.astype(BF16)


def _sel_win(qkv_t, sel_t, onehot_t, tab, far2, gates_t, oc_t):
    b, nt, channels, tq = qkv_t.shape
    g, r, dh = NSA_KV_HEADS, NSA_GROUP, HEAD_DIM
    hd = g * r * dh
    n_sel = sel_t.shape[2]
    kv_block = lambda which: pl.BlockSpec((1, nt, dh, tq), lambda i, j, k: (i, 0, hd // dh + which * g + j, 0))
    return pl.pallas_call(
        _sel_win_kernel,
        grid=(b, g, nt),
        in_specs=[
            pl.BlockSpec((1, 1, r * dh, tq), lambda i, j, k: (i, k, j, 0)),
            kv_block(0), kv_block(1), kv_block(2), kv_block(3),
            pl.BlockSpec((1, 1, n_sel, tq), lambda i, j, k: (i, j, 0, k)),
            pl.BlockSpec((nt, n_sel, tq), lambda i, j, k: (0, 0, 0)),
            pl.BlockSpec((r, 2, tq, tq), lambda i, j, k: (j, 0, 0, 0)),
            pl.BlockSpec((tq, tq), lambda i, j, k: (0, 0)),
            pl.BlockSpec((1, 1, FEAT_ROWS, tq), lambda i, j, k: (i, j, 0, k)),
            pl.BlockSpec((1, 1, r, dh, tq), lambda i, j, k: (i, j, 0, 0, k)),
        ],
        out_specs=pl.BlockSpec((1, tq, r * dh), lambda i, j, k: (i, k, j)),
        out_shape=jax.ShapeDtypeStruct((b, nt * tq, hd), BF16),
        scratch_shapes=[pltpu.VMEM((nt, tq, KEY_FEATS), BF16), pltpu.VMEM((nt, tq, KEY_FEATS), BF16),
                        pltpu.VMEM((2 * r, 1, tq), F32), pltpu.VMEM((2 * r, ACC_ROWS, tq), F32)],
        name="sel_win",
        compiler_params=_params("parallel", "parallel", "arbitrary"),
    )(qkv_t, qkv_t, qkv_t, qkv_t, qkv_t, sel_t, onehot_t, tab, far2, gates_t, oc_t)


N_DECAY_PIECES = 3


def _forget_cumsum_kernel(fl_ref, bf_ref, piece_ref):
    x = fl_ref[0] + bf_ref[...]
    y = jnp.minimum(x, 0.0) - jnp.log1p(jnp.exp(-jnp.abs(x)))
    t = y.shape[1]
    lane = lax.broadcasted_iota(jnp.int32, y.shape, 1)
    k = 1
    while k < t:
        y = y + jnp.where(lane >= k, pltpu.roll(y, k, axis=1), 0.0)
        k *= 2
    for i, piece in enumerate(_split3(-y)):
        piece_ref[0, i] = piece


def _forget_cumsum(fl_t, b_f):
    b, h, t = fl_t.shape
    return pl.pallas_call(
        _forget_cumsum_kernel,
        grid=(b,),
        in_specs=[pl.BlockSpec((1, h, t), lambda i: (i, 0, 0)), pl.BlockSpec((h, 1), lambda i: (0, 0))],
        out_specs=pl.BlockSpec((1, N_DECAY_PIECES, h, t), lambda i: (i, 0, 0, 0)),
        out_shape=jax.ShapeDtypeStruct((b, N_DECAY_PIECES, h, t), BF16),
        name="forget_cumsum",
        compiler_params=_params("parallel"),
    )(fl_t, b_f.reshape(h, 1))


FOX_HEADS_PER_STEP = 4


def _fox_kernel(q_ref, k_ref, v_ref, kf_ref, tri_ref, o_ref, krm_ref, m_ref, acc_ref):
    qi = pl.program_id(2)
    tq = SEQ_TILE
    hp = FOX_HEADS_PER_STEP
    scale = HEAD_DIM ** -0.5
    n_pieces = N_DECAY_PIECES
    pad = jnp.zeros((KEY_FEATS - HEAD_DIM - FEAT_ROWS, tq), BF16)

    @pl.when(qi == 0)
    def _():
        def to_rows(c, carry):
            for h in range(hp):
                ka = jnp.concatenate([k_ref[0, c, h * HEAD_DIM:(h + 1) * HEAD_DIM, :], kf_ref[0, c, h], pad], axis=0)
                krm_ref[h, c] = ka.astype(F32).T.astype(BF16)
            return carry

        lax.fori_loop(0, k_ref.shape[1], to_rows, 0)

    q_feat = jnp.where(lax.broadcasted_iota(jnp.int32, (FEAT_ROWS, tq), 0) < n_pieces, 1.0, 0.0).astype(BF16)
    qa = [jnp.concatenate([q_ref[0, 0, h * HEAD_DIM:(h + 1) * HEAD_DIM, :] * scale, q_feat, pad], axis=0)
          for h in range(hp)]

    def step(cs, first=False, bias=None):
        def logits(h):
            tiles = [jnp.dot(krm_ref[h, c], qa[h], preferred_element_type=F32) for c in cs]
            return tiles if bias is None else [s + bias() for s in tiles]

        def values(h):
            return [_with_ones(v_ref[0, c, h * HEAD_DIM:(h + 1) * HEAD_DIM, :]) for c in cs]

        jobs = [(h, functools.partial(logits, h), functools.partial(values, h)) for h in range(hp)]
        _pipelined_updates(m_ref, acc_ref, jobs, first)

    step([qi], first=True, bias=lambda: tri_ref[...])

    @pl.when(qi % 2 == 1)
    def _():
        step([qi - 1])

    def far_pair(c2, carry):
        step([2 * c2, 2 * c2 + 1])
        return carry

    lax.fori_loop(0, qi // 2, far_pair, 0)
    o_ref[0] = jnp.concatenate([_normalised(acc_ref, h) for h in range(hp)], axis=0).T.astype(BF16)


def _fox_attention(q_t, kv_t, kfeat_t, tri):
    b, nt, hd, tq = q_t.shape
    hp, dh = FOX_HEADS_PER_STEP, HEAD_DIM
    n_groups = hd // (hp * dh)
    return pl.pallas_call(
        _fox_kernel,
        grid=(b, n_groups, nt),
        in_specs=[
            pl.BlockSpec((1, 1, hp * dh, tq), lambda i, j, k: (i, k, j, 0)),
            pl.BlockSpec((1, nt, hp * dh, tq), lambda i, j, k: (i, 0, j, 0)),
            pl.BlockSpec((1, nt, hp * dh, tq), lambda i, j, k: (i, 0, n_groups + j, 0)),
            pl.BlockSpec((1, nt, hp, FEAT_ROWS, tq), lambda i, j, k: (i, 0, j, 0, 0)),
            pl.BlockSpec((tq, tq), lambda i, j, k: (0, 0)),
        ],
        out_specs=pl.BlockSpec((1, tq, hp * dh), lambda i, j, k: (i, k, j)),
        out_shape=jax.ShapeDtypeStruct((b, nt * tq, hd), BF16),
        scratch_shapes=[pltpu.VMEM((hp, nt, tq, KEY_FEATS), BF16),
                        pltpu.VMEM((hp, 1, tq), F32), pltpu.VMEM((hp, ACC_ROWS, tq), F32)],
        name="fox_attention",
        compiler_params=_params("parallel", "parallel", "arbitrary"),
    )(q_t, kv_t, kv_t, kfeat_t, tri)


def _t5_bucket_np(dist):
    dist = np.maximum(dist, 0)
    max_exact = N_BUCKETS // 2
    ratio = np.maximum(dist, 1).astype(np.float32) / np.float32(max_exact)
    scaled = np.log(ratio) / np.float32(math.log(MAX_DISTANCE / max_exact))
    large = np.minimum(max_exact + (scaled * np.float32(N_BUCKETS - max_exact)).astype(np.int32), N_BUCKETS - 1)
    return np.where(dist < max_exact, dist, large).astype(np.int32)


_MASKED = MAX_DISTANCE + 1


def _dist_index(dist, valid):
    return np.where(valid, np.minimum(dist, MAX_DISTANCE), _MASKED).astype(np.int32)


def _toeplitz(f_ext, n):
    period = 2 * n
    flat = jnp.tile(f_ext, n)[..., :n * (period - 1)]
    return flat.reshape(f_ext.shape[:-1] + (n, period - 1))[..., :n]


def _key_query_delta(n):
    p = np.arange(2 * n)
    return np.where(p < n, p, p - 2 * n)


def _bias_tables(rel_bias, n_pad):
    bucket = _t5_bucket_np(np.arange(MAX_DISTANCE + 1))
    by_dist = jnp.concatenate([rel_bias[bucket], jnp.full((1, N_HEADS), NEG_INF, F32)], axis=0).T
    far_const = by_dist[:, MAX_DISTANCE:MAX_DISTANCE + 1]

    tq = CMP_Q_TILE
    u = np.arange(2 * n_pad) - n_pad
    dist = np.arange(tq)[None, :] - CMP_STRIDE * u[:, None] - (CMP_BLOCK - 1)
    varying = np.nonzero((dist.max(axis=1) >= 0) & (dist.min(axis=1) < MAX_DISTANCE))[0]
    lo, hi = int(varying[0]), int(varying[-1]) + 1
    strip = by_dist[:, _dist_index(dist[lo:hi], dist[lo:hi] >= 0)]
    tb = jnp.concatenate([
        jnp.broadcast_to(far_const[:, :, None], (N_HEADS, lo, tq)),
        strip,
        jnp.full((N_HEADS, 2 * n_pad - hi, tq), NEG_INF, F32)], axis=1)

    shifted = jnp.concatenate([by_dist[:, :-1] - far_const, by_dist[:, -1:]], axis=1)
    n = SEQ_TILE
    delta = _key_query_delta(n)
    prev = _toeplitz(shifted[:, _dist_index(delta + n, delta + n >= 0)], n)
    diag = _toeplitz(shifted[:, _dist_index(delta, delta >= 0)], n)
    tab = jnp.stack([prev, diag], axis=1)
    return tb, tab


def _static_tiles(seq):
    n = SEQ_TILE
    key = np.arange(n)[:, None]
    query = np.arange(n)[None, :]
    tri = np.where(key <= query, 0.0, NEG_INF).astype(np.float32)
    far2 = np.where(query - key + 2 * n < WINDOW, 0.0, NEG_INF).astype(np.float32)
    assert WINDOW == 2 * n
    n_sel = seq // SEL_BLOCK
    pos = np.arange(seq).reshape(seq // n, 1, n)
    onehot = (pos // SEL_BLOCK == np.arange(n_sel)[None, :, None]).astype(np.float32)
    return jnp.asarray(tri), jnp.asarray(far2), jnp.asarray(onehot, BF16)


def _overlap_t(n_pad, n_sel):
    n_cmp = n_pad - 1
    cmp_start = np.arange(n_pad) * CMP_STRIDE
    sel_start = np.arange(n_sel) * SEL_BLOCK
    ov = ((cmp_start[None, :] < sel_start[:, None] + SEL_BLOCK)
          & (cmp_start[None, :] + CMP_BLOCK > sel_start[:, None])
          & (np.arange(n_pad)[None, :] < n_cmp))
    return jnp.asarray(ov.astype(np.float32), BF16)


def _nsa_mixer(x, g, w_in, cmp_pos, cmp_w1, cmp_w2, rel_bias, w_out, batch, seq):
    n, d = x.shape
    G, R, dh = NSA_KV_HEADS, NSA_GROUP, HEAD_DIM
    hd, kvd = N_HEADS * dh, G * dh
    cols = [hd + i * kvd for i in range(7)]
    wq, wkc, wvc, wks, wvs, wkw, wvw, wgl = jnp.split(w_in, cols, axis=1)
    n_gate = wgl.shape[1]
    wt = jnp.concatenate([wq, wks, wvs, wkw, wvw], axis=1).T.astype(BF16)
    w2 = jnp.concatenate([wkc, wvc, wgl, jnp.zeros((d, -n_gate % 128), w_in.dtype)], axis=1).astype(BF16)
    qkv_t, f = _norm_proj(x, g, wt, w2, batch, seq)

    gates_t = f[:, 2 * kvd:2 * kvd + n_gate].reshape(batch, seq, G, R * 3).transpose(0, 2, 3, 1)
    gates_t = jnp.pad(gates_t, ((0, 0), (0, 0), (0, FEAT_ROWS - R * 3), (0, 0)))

    nh = seq // CMP_STRIDE
    xh = f[:, :2 * kvd].reshape(batch, nh, CMP_STRIDE, 2, G, dh).transpose(0, 3, 4, 1, 2, 5)
    xh = xh.reshape(batch, 2, G, nh, CMP_STRIDE * dh)
    cmp = _compress(xh, cmp_pos.reshape(2, 2, CMP_STRIDE * dh), cmp_w1.astype(BF16), cmp_w2.astype(BF16))
    kcmp = cmp[:, 0]
    vcmp_t = cmp[:, 1].transpose(0, 1, 3, 2)

    n_sel = seq // SEL_BLOCK
    tb, tab = _bias_tables(rel_bias, nh)
    _, far2, onehot_t = _static_tiles(seq)
    oc_t, sel_t = _cmp_sel(qkv_t, kcmp, vcmp_t, tb, _overlap_t(nh, n_sel))
    o = _sel_win(qkv_t, sel_t, onehot_t, tab, far2, gates_t, oc_t)
    return _out_proj(x, o.reshape(n, hd), w_out.astype(BF16))


def _shared_kv(x, g, w_kvf, b_f, batch, seq):
    hd = N_HEADS * HEAD_DIM
    wt = w_kvf[:, :2 * hd].T.astype(BF16)
    wf = w_kvf[:, 2 * hd:]
    w2 = jnp.concatenate([wf, jnp.zeros((wf.shape[0], -wf.shape[1] % 128), wf.dtype)], axis=1).astype(BF16)
    kv_t, f = _norm_proj(x, g, wt, w2, batch, seq)
    fl_t = f[:, :N_HEADS].reshape(batch, seq, N_HEADS).transpose(0, 2, 1)
    pieces = _forget_cumsum(fl_t, b_f.astype(F32)).transpose(0, 2, 1, 3)
    pieces = jnp.pad(pieces, ((0, 0), (0, 0), (0, FEAT_ROWS - N_DECAY_PIECES), (0, 0)))
    kfeat_t = pieces.reshape(batch, N_HEADS, FEAT_ROWS, seq // SEQ_TILE, SEQ_TILE).transpose(0, 3, 1, 2, 4)
    return kv_t, kfeat_t


def _fox_mixer(x, g, w_q, kv, w_out, batch, seq):
    n, _ = x.shape
    hd = N_HEADS * HEAD_DIM
    (q_t,) = _norm_proj(x, g, w_q.T.astype(BF16), None, batch, seq)
    tri, _, _ = _static_tiles(seq)
    o = _fox_attention(q_t, *kv, tri)
    return _out_proj(x, o.reshape(n, hd), w_out.astype(BF16))


def kernel(x, norm_g, ffn_w_in, ffn_w_out, nsa_w_in, nsa_cmp_pos, nsa_cmp_w1, nsa_cmp_w2, nsa_w_out, rel_bias,
           kv_norm_g, fox_w_kvf, fox_b_f, fox_w_q, fox_w_out, final_g):
    batch, seq, d = x.shape
    depth = norm_g.shape[0]
    n_a = nsa_w_in.shape[0]
    xf = x.reshape(batch * seq, d)
    w_in = ffn_w_in.astype(BF16)
    w_out = ffn_w_out.astype(BF16)
    kv = None
    for l in range(depth):
        if l == n_a:
            kv = _shared_kv(xf, kv_norm_g, fox_w_kvf, fox_b_f, batch, seq)
        xf = _ffn(xf, norm_g[l, 0], w_in[l, 0], w_out[l, 0])
        if l < n_a:
            xf = _nsa_mixer(xf, norm_g[l, 1], nsa_w_in[l], nsa_cmp_pos[l], nsa_cmp_w1[l], nsa_cmp_w2[l],
                            rel_bias, nsa_w_out[l], batch, seq)
        else:
            xf = _fox_mixer(xf, norm_g[l, 1], fox_w_q[l - n_a], kv, fox_w_out[l - n_a], batch, seq)
        xf = _ffn(xf, norm_g[l, 2], w_in[l, 1], w_out[l, 1], final_g if l == depth - 1 else None)
    return xf.reshape(batch, seq, d)
```

```python
import functools
import math

import numpy as np
import jax
import jax.numpy as jnp
from jax import lax
from jax.experimental import pallas as pl
from jax.experimental.pallas import tpu as pltpu

N_HEADS = 16
HEAD_DIM = 64
NSA_KV_HEADS = 4
NSA_GROUP = N_HEADS // NSA_KV_HEADS
CMP_BLOCK = 32
CMP_STRIDE = 16
SEL_BLOCK = 64
SEL_BLOCK_SHIFT = SEL_BLOCK.bit_length() - 1
assert 1 << SEL_BLOCK_SHIFT == SEL_BLOCK
SEL_TOPK = 16
WINDOW = 512
N_BUCKETS = 32
MAX_DISTANCE = 128
EPS = 1e-6
NEG_INF = -1e30
LOG2E = math.log2(math.e)
Q_PRESCALE = LOG2E * HEAD_DIM ** -0.5
SEL_FORCE = 1e4

F32 = jnp.float32
BF16 = jnp.bfloat16

VMEM_LIMIT_BYTES = 48 * 1024 * 1024

ROW_TILE = 512
FF_TILE = 256
SEQ_TILE = 256
FEAT_ROWS = 16
ACC_ROWS = HEAD_DIM + FEAT_ROWS
QK_LOOKAHEAD = 5
KEY_FEATS = 128

_NT = (((1,), (1,)), ((), ()))
_TN = (((0,), (0,)), ((), ()))


def _params(*sem):
    return pltpu.CompilerParams(dimension_semantics=sem, vmem_limit_bytes=VMEM_LIMIT_BYTES)


def _rms(x, g):
    return x * lax.rsqrt(jnp.mean(x * x, axis=-1, keepdims=True) + EPS) * g


def _silu(a):
    return a * jax.nn.sigmoid(a)


def _ffn_kernel(x_ref, g_ref, wi_ref, wo_ref, *rest, final_norm):
    if final_norm:
        fg_ref, o_ref = rest
    else:
        (o_ref,) = rest
    x = x_ref[...]
    hn = _rms(x, g_ref[...]).astype(BF16)
    acc = None
    for j in range(wo_ref.shape[0]):
        a = jnp.dot(hn, wi_ref[0, j], preferred_element_type=F32)
        b = jnp.dot(hn, wi_ref[1, j], preferred_element_type=F32)
        part = jnp.dot((_silu(a) * b).astype(BF16), wo_ref[j], preferred_element_type=F32)
        acc = part if acc is None else acc + part
    y = x + 0.5 * acc
    if final_norm:
        y = _rms(y, fg_ref[...])
    o_ref[...] = y


def _resident(shape):
    return pl.BlockSpec(shape, lambda *_: (0,) * len(shape), pipeline_mode=pl.Buffered(1))


def _ffn(x, g, w_in, w_out, final_g=None):
    n, d = x.shape
    in_specs = [
        pl.BlockSpec((ROW_TILE, d), lambda i: (i, 0)),
        _resident((1, d)),
        _resident(w_in.shape),
        _resident(w_out.shape),
    ]
    args = [x, g.reshape(1, d), w_in, w_out]
    if final_g is not None:
        in_specs.append(_resident((1, d)))
        args.append(final_g.reshape(1, d))
    return pl.pallas_call(
        functools.partial(_ffn_kernel, final_norm=final_g is not None),
        grid=(n // ROW_TILE,),
        in_specs=in_specs,
        out_specs=pl.BlockSpec((ROW_TILE, d), lambda i: (i, 0)),
        out_shape=jax.ShapeDtypeStruct((n, d), F32),
        name="ffn",
        compiler_params=_params("parallel"),
    )(*args)


PROJ_CHUNK = 512


def _norm_proj_kernel(x_ref, g_ref, wt_ref, *rest, has_rows):
    if has_rows:
        w2_ref, ot_ref, o2_ref = rest
    else:
        (ot_ref,) = rest
    hn = _rms(x_ref[...], g_ref[...]).astype(BF16)
    channels = ot_ref.shape[2]
    for c in range(0, channels, PROJ_CHUNK):
        r = lax.dot_general(wt_ref[c:c + PROJ_CHUNK, :], hn, _NT, preferred_element_type=F32).astype(BF16)
        for j in range(ot_ref.shape[1]):
            ot_ref[0, j, c:c + PROJ_CHUNK, :] = r[:, j * SEQ_TILE:(j + 1) * SEQ_TILE]
    if has_rows:
        o2_ref[...] = jnp.dot(hn, w2_ref[...], preferred_element_type=F32)


def _norm_proj(x, g, wt, w2, batch, seq):
    n, d = x.shape
    channels = wt.shape[0]
    per_seq = seq // ROW_TILE
    sub = ROW_TILE // SEQ_TILE
    in_specs = [
        pl.BlockSpec((ROW_TILE, d), lambda i: (i, 0)),
        pl.BlockSpec((1, d), lambda i: (0, 0)),
        pl.BlockSpec((channels, d), lambda i: (0, 0)),
    ]
    out_specs = [pl.BlockSpec((1, sub, channels, SEQ_TILE), lambda i: (i // per_seq, i % per_seq, 0, 0))]
    out_shape = [jax.ShapeDtypeStruct((batch, seq // SEQ_TILE, channels, SEQ_TILE), BF16)]
    args = [x, g.reshape(1, d), wt]
    if w2 is not None:
        n2 = w2.shape[1]
        in_specs.append(pl.BlockSpec((d, n2), lambda i: (0, 0)))
        out_specs.append(pl.BlockSpec((ROW_TILE, n2), lambda i: (i, 0)))
        out_shape.append(jax.ShapeDtypeStruct((n, n2), F32))
        args.append(w2)
    return pl.pallas_call(
        functools.partial(_norm_proj_kernel, has_rows=w2 is not None),
        grid=(n // ROW_TILE,),
        in_specs=in_specs,
        out_specs=out_specs,
        out_shape=out_shape,
        name="norm_proj",
        compiler_params=_params("parallel"),
    )(*args)


def _out_proj_kernel(x_ref, o_ref, w_ref, y_ref):
    y_ref[...] = x_ref[...] + jnp.dot(o_ref[...], w_ref[...], preferred_element_type=F32)


def _out_proj(x, o, w):
    n, d = x.shape
    k = o.shape[1]
    return pl.pallas_call(
        _out_proj_kernel,
        grid=(n // ROW_TILE,),
        in_specs=[
            pl.BlockSpec((ROW_TILE, d), lambda i: (i, 0)),
            pl.BlockSpec((ROW_TILE, k), lambda i: (i, 0)),
            pl.BlockSpec((k, d), lambda i: (0, 0)),
        ],
        out_specs=pl.BlockSpec((ROW_TILE, d), lambda i: (i, 0)),
        out_shape=jax.ShapeDtypeStruct((n, d), F32),
        name="out_proj",
        compiler_params=_params("parallel"),
    )(x, o, w)


def _compress_kernel(x_ref, pos_ref, w1_ref, w2_ref, o_ref):
    x = x_ref[0, 0, 0]
    pos = pos_ref[0]
    half = x.shape[1]
    top = jnp.dot((x + pos[0:1]).astype(BF16), w1_ref[0, :half, :], preferred_element_type=F32)
    bot = jnp.dot((x + pos[1:2]).astype(BF16), w1_ref[0, half:, :], preferred_element_type=F32)
    pre = top + pltpu.roll(bot, bot.shape[0] - 1, axis=0)
    hid = _silu(pre).astype(BF16)
    o_ref[0, 0, 0] = jnp.dot(hid, w2_ref[0], preferred_element_type=F32).astype(BF16)


def _compress(xh, pos2, w1, w2):
    b, _, g, nh, hw = xh.shape
    hid = w1.shape[2]
    return pl.pallas_call(
        _compress_kernel,
        grid=(b, 2, g),
        in_specs=[
            pl.BlockSpec((1, 1, 1, nh, hw), lambda i, j, k: (i, j, k, 0, 0)),
            pl.BlockSpec((1, 2, hw), lambda i, j, k: (j, 0, 0)),
            pl.BlockSpec((1, 2 * hw, hid), lambda i, j, k: (j, 0, 0)),
            pl.BlockSpec((1, hid, HEAD_DIM), lambda i, j, k: (j, 0, 0)),
        ],
        out_specs=pl.BlockSpec((1, 1, 1, nh, HEAD_DIM), lambda i, j, k: (i, j, k, 0, 0)),
        out_shape=jax.ShapeDtypeStruct((b, 2, g, nh, HEAD_DIM), BF16),
        name="compress",
        compiler_params=_params("parallel", "parallel", "parallel"),
    )(xh, pos2, w1, w2)


def _split3(x):
    p1 = x.astype(BF16)
    r1 = x - p1.astype(F32)
    p2 = r1.astype(BF16)
    p3 = (r1 - p2.astype(F32)).astype(BF16)
    return p1, p2, p3


def _cmp_sel_kernel(q_ref, kc_ref, vct_ref, tb_ref, ovt_ref, oct_ref, sel_ref):
    qi = pl.program_id(2)
    tq = SEQ_TILE
    nrep = NSA_GROUP
    n_cmp_pad = kc_ref.shape[2]
    qt = jnp.concatenate([q_ref[0, 0, r * HEAD_DIM:(r + 1) * HEAD_DIM, :] for r in range(nrep)], axis=1)
    s = jnp.dot(kc_ref[0, 0], qt, preferred_element_type=F32)
    off = pl.multiple_of(n_cmp_pad - (tq // CMP_STRIDE) * qi, 8)
    bias = jnp.concatenate([tb_ref[r, pl.ds(off, n_cmp_pad), :] for r in range(nrep)], axis=1)
    s = s + bias
    m = jnp.max(s, axis=0, keepdims=True)
    e = jnp.exp2(s - m)
    p = jnp.where(bias > 0.5 * NEG_INF, e / jnp.sum(e, axis=0, keepdims=True), 0.0)

    oct = jnp.dot(vct_ref[0, 0], p.astype(BF16), preferred_element_type=F32)
    for r in range(nrep):
        oct_ref[0, 0, r] = oct[:, r * tq:(r + 1) * tq]

    psum = p[:, 0:tq]
    for r in range(1, nrep):
        psum = psum + p[:, r * tq:(r + 1) * tq]
    ovt = ovt_ref[...]
    imp = sum(jnp.dot(ovt, part, preferred_element_type=F32) for part in _split3(psum))

    n_sel = imp.shape[0]
    blk = lax.broadcasted_iota(jnp.int32, (n_sel, tq), 0)
    cur = jnp.right_shift(qi * tq + lax.broadcasted_iota(jnp.int32, (n_sel, tq), 1), SEL_BLOCK_SHIFT)
    forced = (blk == 0) | (blk == cur) | (blk == cur - 1)
    score = jnp.where(blk > cur, -SEL_FORCE, imp + jnp.where(forced, SEL_FORCE, 0.0))
    rows = 8
    groups = [score[g * rows:(g + 1) * rows] for g in range(n_sel // rows)]
    blk_in_group = lax.broadcasted_iota(jnp.int32, (rows, tq), 0)
    counts = [jnp.zeros((rows, tq), F32) for _ in groups]
    for sp in range(n_sel):
        row = jnp.broadcast_to(score[sp:sp + 1, :], (rows, tq))
        for g, sg in enumerate(groups):
            if g * rows > sp:
                beats = row >= sg
            elif g * rows + rows - 1 < sp:
                beats = row > sg
            else:
                tie = jnp.where(blk_in_group + g * rows > sp, 1.0, 0.0)
                counts[g] = counts[g] + jnp.where(row > sg, 1.0, jnp.where(row == sg, tie, 0.0))
                continue
            counts[g] = counts[g] + jnp.where(beats, 1.0, 0.0)
    cnt = jnp.concatenate(counts, axis=0)
    sel_ref[0, 0] = jnp.where(cnt < min(SEL_TOPK, n_sel), 1.0, 0.0).astype(BF16)


def _cmp_sel(qkv_t, kcmp, vcmp_t, tb, ovt):
    b, nt, _, _ = qkv_t.shape
    g, r, dh = NSA_KV_HEADS, NSA_GROUP, HEAD_DIM
    t = nt * SEQ_TILE
    n_pad = kcmp.shape[2]
    n_sel = ovt.shape[0]
    tq = SEQ_TILE
    return pl.pallas_call(
        _cmp_sel_kernel,
        grid=(b, g, t // tq),
        in_specs=[
            pl.BlockSpec((1, 1, r * dh, tq), lambda i, j, k: (i, k, j, 0)),
            pl.BlockSpec((1, 1, n_pad, dh), lambda i, j, k: (i, j, 0, 0)),
            pl.BlockSpec((1, 1, dh, n_pad), lambda i, j, k: (i, j, 0, 0)),
            pl.BlockSpec((r, 2 * n_pad, tq), lambda i, j, k: (j, 0, 0)),
            pl.BlockSpec((n_sel, n_pad), lambda i, j, k: (0, 0)),
        ],
        out_specs=[
            pl.BlockSpec((1, 1, r, dh, tq), lambda i, j, k: (i, j, 0, 0, k)),
            pl.BlockSpec((1, 1, n_sel, tq), lambda i, j, k: (i, j, 0, k)),
        ],
        out_shape=[
            jax.ShapeDtypeStruct((b, g, r, dh, t), F32),
            jax.ShapeDtypeStruct((b, g, n_sel, t), BF16),
        ],
        name="cmp_sel",
        compiler_params=_params("parallel", "parallel", "parallel"),
    )(qkv_t, kcmp, vcmp_t, tb, ovt)


def _ones_rows(width):
    return jnp.ones((FEAT_ROWS, width), BF16)


def _with_ones(v_t):
    return jnp.concatenate([v_t, _ones_rows(v_t.shape[1])], axis=0)


def _online_update(m_ref, acc_ref, slot, tiles, first):
    mt = None
    for s, _ in tiles:
        cm = jnp.max(s, axis=0, keepdims=True)
        mt = cm if mt is None else jnp.maximum(mt, cm)
    m_old = None if first else m_ref[slot]
    m_new = mt if first else jnp.maximum(m_old, mt)
    pv = None
    for s, va in tiles:
        d = jnp.dot(va, jnp.exp2(s - m_new).astype(BF16), preferred_element_type=F32)
        pv = d if pv is None else pv + d
    acc_ref[slot] = pv if first else jnp.exp2(m_old - m_new) * acc_ref[slot] + pv
    m_ref[slot] = m_new


def _pipelined_updates(m_ref, acc_ref, jobs):
    pending = [job[1]() for job in jobs[:QK_LOOKAHEAD]]
    for i, (slot, _, values_fn, first) in enumerate(jobs):
        current = pending.pop(0)
        if i + QK_LOOKAHEAD < len(jobs):
            pending.append(jobs[i + QK_LOOKAHEAD][1]())
        _online_update(m_ref, acc_ref, slot, list(zip(current, values_fn())), first)


def _for_far_tiles(n_far, run):
    n_quads = n_far // 4

    def quad(i, carry):
        run([[4 * i, 4 * i + 1], [4 * i + 2, 4 * i + 3]])
        return carry

    lax.fori_loop(0, n_quads, quad, 0)

    @pl.when((n_far & 2) != 0)
    def _():
        run([[4 * n_quads, 4 * n_quads + 1]])

    @pl.when((n_far & 1) != 0)
    def _():
        run([[n_far - 1]])


def _normalised(acc_ref, slot):
    acc = acc_ref[slot]
    return acc[:HEAD_DIM] / acc[HEAD_DIM:HEAD_DIM + 1]


def _sel_win_kernel(q_ref, ks_ref, vs_ref, kw_ref, vw_ref, sel_ref, oh_ref, tab_ref, far2_ref, gate_ref, oc_ref,
                    o_ref, ksrm_ref, kwrm_ref, m_ref, acc_ref):
    qi = pl.program_id(2)
    tq = SEQ_TILE
    nrep = NSA_GROUP
    pad = jnp.zeros((KEY_FEATS - HEAD_DIM, tq), BF16)

    @pl.when(qi == 0)
    def _():
        def to_rows(c, carry):
            ksa = jnp.concatenate([ks_ref[0, c], oh_ref[c]], axis=0)
            ksrm_ref[c] = ksa.astype(F32).T.astype(BF16)
            kwa = jnp.concatenate([kw_ref[0, c], pad], axis=0)
            kwrm_ref[c] = kwa.astype(F32).T.astype(BF16)
            return carry

        lax.fori_loop(0, ks_ref.shape[1], to_rows, 0)

    mq = ((sel_ref[0, 0].astype(F32) - 1.0) * (-NEG_INF)).astype(BF16)
    qs = [q_ref[0, 0, r * HEAD_DIM:(r + 1) * HEAD_DIM, :] for r in range(nrep)]
    q_sel = [jnp.concatenate([q, mq], axis=0) for q in qs]
    q_win = [jnp.concatenate([q, pad], axis=0) for q in qs]

    def sel_jobs(cs, bias=None, first=False):
        def logits(r):
            tiles = [jnp.dot(ksrm_ref[c], q_sel[r], preferred_element_type=F32) for c in cs]
            return tiles if bias is None else [s + bias(r) for s in tiles]
        return [(r, functools.partial(logits, r), lambda: [_with_ones(vs_ref[0, c]) for c in cs], first)
                for r in range(nrep)]

    def win_jobs(c, bias, first=False):
        def logits(r):
            return [jnp.dot(kwrm_ref[c], q_win[r], preferred_element_type=F32) + bias(r)]
        return [(nrep + r, functools.partial(logits, r), lambda: [_with_ones(vw_ref[0, c])], first)
                for r in range(nrep)]

    diag = lambda r: tab_ref[0, r, 1]
    prev = lambda r: tab_ref[0, r, 0]
    edge = lambda r: far2_ref[0]
    c_prev = jnp.maximum(qi - 1, 0)
    c_edge = jnp.maximum(qi - 2, 0)
    _pipelined_updates(m_ref, acc_ref,
                       sel_jobs([qi], diag, first=True) + win_jobs(qi, diag, first=True)
                       + sel_jobs([c_prev], prev) + win_jobs(c_prev, prev) + win_jobs(c_edge, edge))

    _for_far_tiles(jnp.maximum(qi - 1, 0), lambda groups: _pipelined_updates(
        m_ref, acc_ref, [job for cs in groups for job in sel_jobs(cs)]))

    gates = jax.nn.sigmoid(gate_ref[0, 0])
    outs = []
    for r in range(nrep):
        outs.append(gates[3 * r:3 * r + 1] * oc_ref[0, 0, r]
                    + gates[3 * r + 1:3 * r + 2] * _normalised(acc_ref, r)
                    + gates[3 * r + 2:3 * r + 3] * _normalised(acc_ref, nrep + r))
    o_ref[0] = jnp.concatenate(outs, axis=0).T.astype(BF16)


def _sel_win(qkv_t, sel_t, onehot_t, tab, far2, gates_t, oc_t):
    b, nt, channels, tq = qkv_t.shape
    g, r, dh = NSA_KV_HEADS, NSA_GROUP, HEAD_DIM
    hd = g * r * dh
    n_sel = sel_t.shape[2]
    kv_block = lambda which: pl.BlockSpec((1, nt, dh, tq), lambda i, j, k: (i, 0, hd // dh + which * g + j, 0))
    return pl.pallas_call(
        _sel_win_kernel,
        grid=(b, g, nt),
        in_specs=[
            pl.BlockSpec((1, 1, r * dh, tq), lambda i, j, k: (i, k, j, 0)),
            kv_block(0), kv_block(1), kv_block(2), kv_block(3),
            pl.BlockSpec((1, 1, n_sel, tq), lambda i, j, k: (i, j, 0, k)),
            pl.BlockSpec((nt, n_sel, tq), lambda i, j, k: (0, 0, 0)),
            pl.BlockSpec((1, r, 2, tq, tq), lambda i, j, k: (jnp.minimum(k, tab.shape[0] - 1), j, 0, 0, 0)),
            pl.BlockSpec((1, tq, tq), lambda i, j, k: (jnp.minimum(k, far2.shape[0] - 1), 0, 0)),
            pl.BlockSpec((1, 1, FEAT_ROWS, tq), lambda i, j, k: (i, j, 0, k)),
            pl.BlockSpec((1, 1, r, dh, tq), lambda i, j, k: (i, j, 0, 0, k)),
        ],
        out_specs=pl.BlockSpec((1, tq, r * dh), lambda i, j, k: (i, k, j)),
        out_shape=jax.ShapeDtypeStruct((b, nt * tq, hd), BF16),
        scratch_shapes=[pltpu.VMEM((nt, tq, KEY_FEATS), BF16), pltpu.VMEM((nt, tq, KEY_FEATS), BF16),
                        pltpu.VMEM((2 * r, 1, tq), F32), pltpu.VMEM((2 * r, ACC_ROWS, tq), F32)],
        name="sel_win",
        compiler_params=_params("parallel", "parallel", "arbitrary"),
    )(qkv_t, qkv_t, qkv_t, qkv_t, qkv_t, sel_t, onehot_t, tab, far2, gates_t, oc_t)


N_DECAY_PIECES = 3


def _forget_cumsum_kernel(fl_ref, bf_ref, piece_ref):
    x = fl_ref[0] + bf_ref[...]
    y = jnp.minimum(x, 0.0) - jnp.log1p(jnp.exp(-jnp.abs(x)))
    t = y.shape[1]
    lane = lax.broadcasted_iota(jnp.int32, y.shape, 1)
    k = 1
    while k < t:
        y = y + jnp.where(lane >= k, pltpu.roll(y, k, axis=1), 0.0)
        k *= 2
    for i, piece in enumerate(_split3(-LOG2E * y)):
        piece_ref[0, i] = piece


def _forget_cumsum(fl_t, b_f):
    b, h, t = fl_t.shape
    return pl.pallas_call(
        _forget_cumsum_kernel,
        grid=(b,),
        in_specs=[pl.BlockSpec((1, h, t), lambda i: (i, 0, 0)), pl.BlockSpec((h, 1), lambda i: (0, 0))],
        out_specs=pl.BlockSpec((1, N_DECAY_PIECES, h, t), lambda i: (i, 0, 0, 0)),
        out_shape=jax.ShapeDtypeStruct((b, N_DECAY_PIECES, h, t), BF16),
        name="forget_cumsum",
        compiler_params=_params("parallel"),
    )(fl_t, b_f.reshape(h, 1))


FOX_HEADS_PER_STEP = 4


def _fox_kernel(q_ref, k_ref, v_ref, kf_ref, tri_ref, o_ref, krm_ref, m_ref, acc_ref):
    qi = pl.program_id(2)
    tq = SEQ_TILE
    hp = FOX_HEADS_PER_STEP
    n_pieces = N_DECAY_PIECES
    pad = jnp.zeros((KEY_FEATS - HEAD_DIM - FEAT_ROWS, tq), BF16)

    @pl.when(qi == 0)
    def _():
        def to_rows(c, carry):
            for h in range(hp):
                ka = jnp.concatenate([k_ref[0, c, h * HEAD_DIM:(h + 1) * HEAD_DIM, :], kf_ref[0, c, h], pad], axis=0)
                krm_ref[h, c] = ka.astype(F32).T.astype(BF16)
            return carry

        lax.fori_loop(0, k_ref.shape[1], to_rows, 0)

    q_feat = jnp.where(lax.broadcasted_iota(jnp.int32, (FEAT_ROWS, tq), 0) < n_pieces, 1.0, 0.0).astype(BF16)
    qa = [jnp.concatenate([q_ref[0, 0, h * HEAD_DIM:(h + 1) * HEAD_DIM, :], q_feat, pad], axis=0)
          for h in range(hp)]

    def step(groups, first=False, bias=None):
        def logits(h, cs):
            tiles = [jnp.dot(krm_ref[h, c], qa[h], preferred_element_type=F32) for c in cs]
            return tiles if bias is None else [s + bias() for s in tiles]

        def values(h, cs):
            return [_with_ones(v_ref[0, c, h * HEAD_DIM:(h + 1) * HEAD_DIM, :]) for c in cs]

        jobs = [(h, functools.partial(logits, h, cs), functools.partial(values, h, cs), first)
                for cs in groups for h in range(hp)]
        _pipelined_updates(m_ref, acc_ref, jobs)

    step([[qi]], first=True, bias=lambda: tri_ref[...])
    _for_far_tiles(qi, step)
    o_ref[0] = jnp.concatenate([_normalised(acc_ref, h) for h in range(hp)], axis=0).T.astype(BF16)


def _fox_attention(q_t, kv_t, kfeat_t, tri):
    b, nt, hd, tq = q_t.shape
    hp, dh = FOX_HEADS_PER_STEP, HEAD_DIM
    n_groups = hd // (hp * dh)
    return pl.pallas_call(
        _fox_kernel,
        grid=(b, n_groups, nt),
        in_specs=[
            pl.BlockSpec((1, 1, hp * dh, tq), lambda i, j, k: (i, k, j, 0)),
            pl.BlockSpec((1, nt, hp * dh, tq), lambda i, j, k: (i, 0, j, 0)),
            pl.BlockSpec((1, nt, hp * dh, tq), lambda i, j, k: (i, 0, n_groups + j, 0)),
            pl.BlockSpec((1, nt, hp, FEAT_ROWS, tq), lambda i, j, k: (i, 0, j, 0, 0)),
            pl.BlockSpec((tq, tq), lambda i, j, k: (0, 0)),
        ],
        out_specs=pl.BlockSpec((1, tq, hp * dh), lambda i, j, k: (i, k, j)),
        out_shape=jax.ShapeDtypeStruct((b, nt * tq, hd), BF16),
        scratch_shapes=[pltpu.VMEM((hp, nt, tq, KEY_FEATS), BF16),
                        pltpu.VMEM((hp, 1, tq), F32), pltpu.VMEM((hp, ACC_ROWS, tq), F32)],
        name="fox_attention",
        compiler_params=_params("parallel", "parallel", "arbitrary"),
    )(q_t, kv_t, kv_t, kfeat_t, tri)


def _t5_bucket_np(dist):
    dist = np.maximum(dist, 0)
    max_exact = N_BUCKETS // 2
    ratio = np.maximum(dist, 1).astype(np.float32) / np.float32(max_exact)
    scaled = np.log(ratio) / np.float32(math.log(MAX_DISTANCE / max_exact))
    large = np.minimum(max_exact + (scaled * np.float32(N_BUCKETS - max_exact)).astype(np.int32), N_BUCKETS - 1)
    return np.where(dist < max_exact, dist, large).astype(np.int32)


_MASKED = MAX_DISTANCE + 1


def _dist_index(dist, valid):
    return np.where(valid, np.minimum(dist, MAX_DISTANCE), _MASKED).astype(np.int32)


def _toeplitz(f_ext, n):
    period = 2 * n
    flat = jnp.tile(f_ext, n)[..., :n * (period - 1)]
    return flat.reshape(f_ext.shape[:-1] + (n, period - 1))[..., :n]


def _key_query_delta(n):
    p = np.arange(2 * n)
    return np.where(p < n, p, p - 2 * n)


def _bias_tables(rel_bias, n_pad):
    bucket = _t5_bucket_np(np.arange(MAX_DISTANCE + 1))
    by_dist = jnp.concatenate([rel_bias[bucket], jnp.full((1, N_HEADS), NEG_INF, F32)], axis=0).T
    far_const = by_dist[:, MAX_DISTANCE:MAX_DISTANCE + 1]

    tq = SEQ_TILE
    u = np.arange(2 * n_pad) - n_pad
    dist = np.arange(tq)[None, :] - CMP_STRIDE * u[:, None] - (CMP_BLOCK - 1)
    varying = np.nonzero((dist.max(axis=1) >= 0) & (dist.min(axis=1) < MAX_DISTANCE))[0]
    lo, hi = int(varying[0]), int(varying[-1]) + 1
    strip = by_dist[:, _dist_index(dist[lo:hi], dist[lo:hi] >= 0)]
    tb = jnp.concatenate([
        jnp.broadcast_to(far_const[:, :, None], (N_HEADS, lo, tq)),
        strip,
        jnp.full((N_HEADS, 2 * n_pad - hi, tq), NEG_INF, F32)], axis=1)

    shifted = jnp.concatenate([by_dist[:, :-1] - far_const, by_dist[:, -1:]], axis=1)
    n = SEQ_TILE
    delta = _key_query_delta(n)
    prev = _toeplitz(shifted[:, _dist_index(delta + n, delta + n >= 0)], n)
    diag = _toeplitz(shifted[:, _dist_index(delta, delta >= 0)], n)
    tab = jnp.stack([prev, diag], axis=1)
    no_prev = jnp.stack([jnp.full_like(prev, NEG_INF), diag], axis=1)
    return LOG2E * tb, LOG2E * jnp.stack([no_prev, tab])


def _static_tiles(seq):
    n = SEQ_TILE
    key = np.arange(n)[:, None]
    query = np.arange(n)[None, :]
    tri = np.where(key <= query, 0.0, NEG_INF).astype(np.float32)
    far2 = np.where(query - key + 2 * n < WINDOW, 0.0, NEG_INF).astype(np.float32)
    assert WINDOW == 2 * n
    far2 = np.stack([np.full_like(far2, NEG_INF)] * 2 + [far2])
    n_sel = seq // SEL_BLOCK
    pos = np.arange(seq).reshape(seq // n, 1, n)
    onehot = (pos // SEL_BLOCK == np.arange(n_sel)[None, :, None]).astype(np.float32)
    return jnp.asarray(tri), jnp.asarray(far2), jnp.asarray(onehot, BF16)


def _overlap_t(n_pad, n_sel):
    n_cmp = n_pad - 1
    cmp_start = np.arange(n_pad) * CMP_STRIDE
    sel_start = np.arange(n_sel) * SEL_BLOCK
    ov = ((cmp_start[None, :] < sel_start[:, None] + SEL_BLOCK)
          & (cmp_start[None, :] + CMP_BLOCK > sel_start[:, None])
          & (np.arange(n_pad)[None, :] < n_cmp))
    return jnp.asarray(ov.astype(np.float32), BF16)


def _nsa_mixer(x, g, w_in, cmp_pos, cmp_w1, cmp_w2, rel_bias, w_out, batch, seq):
    n, d = x.shape
    G, R, dh = NSA_KV_HEADS, NSA_GROUP, HEAD_DIM
    hd, kvd = N_HEADS * dh, G * dh
    cols = [hd + i * kvd for i in range(7)]
    wq, wkc, wvc, wks, wvs, wkw, wvw, wgl = jnp.split(w_in, cols, axis=1)
    n_gate = wgl.shape[1]
    wt = jnp.concatenate([wq * Q_PRESCALE, wks, wvs, wkw, wvw], axis=1).T.astype(BF16)
    w2 = jnp.concatenate([wkc, wvc, wgl, jnp.zeros((d, -n_gate % 128), w_in.dtype)], axis=1).astype(BF16)
    qkv_t, f = _norm_proj(x, g, wt, w2, batch, seq)

    gates_t = f[:, 2 * kvd:2 * kvd + n_gate].reshape(batch, seq, G, R * 3).transpose(0, 2, 3, 1)
    gates_t = jnp.pad(gates_t, ((0, 0), (0, 0), (0, FEAT_ROWS - R * 3), (0, 0)))

    nh = seq // CMP_STRIDE
    xh = f[:, :2 * kvd].reshape(batch, nh, CMP_STRIDE, 2, G, dh).transpose(0, 3, 4, 1, 2, 5)
    xh = xh.reshape(batch, 2, G, nh, CMP_STRIDE * dh)
    cmp = _compress(xh, cmp_pos.reshape(2, 2, CMP_STRIDE * dh), cmp_w1.astype(BF16), cmp_w2.astype(BF16))
    kcmp = cmp[:, 0]
    vcmp_t = cmp[:, 1].transpose(0, 1, 3, 2)

    n_sel = seq // SEL_BLOCK
    tb, tab = _bias_tables(rel_bias, nh)
    _, far2, onehot_t = _static_tiles(seq)
    oc_t, sel_t = _cmp_sel(qkv_t, kcmp, vcmp_t, tb, _overlap_t(nh, n_sel))
    o = _sel_win(qkv_t, sel_t, onehot_t, tab, far2, gates_t, oc_t)
    return _out_proj(x, o.reshape(n, hd), w_out.astype(BF16))


def _shared_kv(x, g, w_kvf, b_f, batch, seq):
    hd = N_HEADS * HEAD_DIM
    wt = w_kvf[:, :2 * hd].T.astype(BF16)
    wf = w_kvf[:, 2 * hd:]
    w2 = jnp.concatenate([wf, jnp.zeros((wf.shape[0], -wf.shape[1] % 128), wf.dtype)], axis=1).astype(BF16)
    kv_t, f = _norm_proj(x, g, wt, w2, batch, seq)
    fl_t = f[:, :N_HEADS].reshape(batch, seq, N_HEADS).transpose(0, 2, 1)
    pieces = _forget_cumsum(fl_t, b_f.astype(F32)).transpose(0, 2, 1, 3)
    pieces = jnp.pad(pieces, ((0, 0), (0, 0), (0, FEAT_ROWS - N_DECAY_PIECES), (0, 0)))
    kfeat_t = pieces.reshape(batch, N_HEADS, FEAT_ROWS, seq // SEQ_TILE, SEQ_TILE).transpose(0, 3, 1, 2, 4)
    return kv_t, kfeat_t


def _fox_mixer(x, g, w_q, kv, w_out, batch, seq):
    n, _ = x.shape
    hd = N_HEADS * HEAD_DIM
    (q_t,) = _norm_proj(x, g, (w_q * Q_PRESCALE).T.astype(BF16), None, batch, seq)
    tri, _, _ = _static_tiles(seq)
    o = _fox_attention(q_t, *kv, tri)
    return _out_proj(x, o.reshape(n, hd), w_out.astype(BF16))


def kernel(x, norm_g, ffn_w_in, ffn_w_out, nsa_w_in, nsa_cmp_pos, nsa_cmp_w1, nsa_cmp_w2, nsa_w_out, rel_bias,
           kv_norm_g, fox_w_kvf, fox_b_f, fox_w_q, fox_w_out, final_g):
    batch, seq, d = x.shape
    depth = norm_g.shape[0]
    n_a = nsa_w_in.shape[0]
    xf = x.reshape(batch * seq, d)
    d_ff = ffn_w_out.shape[2]
    chunks = d_ff // FF_TILE
    w_in = ffn_w_in.astype(BF16).reshape(depth, 2, d, 2, chunks, FF_TILE).transpose(0, 1, 3, 4, 2, 5)
    w_out = ffn_w_out.astype(BF16).reshape(depth, 2, chunks, FF_TILE, d)
    kv = None
    for l in range(depth):
        if l == n_a:
            kv = _shared_kv(xf, kv_norm_g, fox_w_kvf, fox_b_f, batch, seq)
        xf = _ffn(xf, norm_g[l, 0], w_in[l, 0], w_out[l, 0])
        if l < n_a:
            xf = _nsa_mixer(xf, norm_g[l, 1], nsa_w_in[l], nsa_cmp_pos[l], nsa_cmp_w1[l], nsa_cmp_w2[l],
                            rel_bias, nsa_w_out[l], batch, seq)
        else:
            xf = _fox_mixer(xf, norm_g[l, 1], fox_w_q[l - n_a], kv, fox_w_out[l - n_a], batch, seq)
        xf = _ffn(xf, norm_g[l, 2], w_in[l, 1], w_out[l, 1], final_g if l == depth - 1 else None)
    return xf.reshape(batch, seq, d)
```

```python
import functools
import math

import numpy as np
import jax
import jax.numpy as jnp
from jax import lax
from jax.experimental import pallas as pl
from jax.experimental.pallas import tpu as pltpu

N_HEADS = 16
HEAD_DIM = 64
NSA_KV_HEADS = 4
NSA_GROUP = N_HEADS // NSA_KV_HEADS
NSA_GROUPS_PER_STEP = 2
CMP_BLOCK = 32
CMP_STRIDE = 16
SEL_BLOCK = 64
SEL_BLOCK_SHIFT = SEL_BLOCK.bit_length() - 1
assert 1 << SEL_BLOCK_SHIFT == SEL_BLOCK
SEL_TOPK = 16
WINDOW = 512
N_BUCKETS = 32
MAX_DISTANCE = 128
EPS = 1e-6
NEG_INF = -1e30
LOG2E = math.log2(math.e)
Q_PRESCALE = LOG2E * HEAD_DIM ** -0.5
SEL_FORCE = 1e4

F32 = jnp.float32
BF16 = jnp.bfloat16

VMEM_LIMIT_BYTES = 48 * 1024 * 1024

ROW_TILE = 512
FF_TILE = 256
SEQ_TILE = 256
FEAT_ROWS = 16
ACC_ROWS = HEAD_DIM + FEAT_ROWS
QK_LOOKAHEAD = 5
KEY_FEATS = 128

_NT = (((1,), (1,)), ((), ()))
_TN = (((0,), (0,)), ((), ()))


def _params(*sem):
    return pltpu.CompilerParams(dimension_semantics=sem, vmem_limit_bytes=VMEM_LIMIT_BYTES)


def _rms(x, g):
    return x * lax.rsqrt(jnp.mean(x * x, axis=-1, keepdims=True) + EPS) * g


def _silu(a):
    return a * jax.nn.sigmoid(a)


def _ffn_kernel(x_ref, g_ref, wi_ref, wo_ref, *rest, has_mixer, final_norm):
    rest = list(rest)
    mix_ref, wm_ref = (rest.pop(0), rest.pop(0)) if has_mixer else (None, None)
    fg_ref = rest.pop(0) if final_norm else None
    (o_ref,) = rest
    x = x_ref[...]
    if has_mixer:
        x = x + jnp.dot(mix_ref[...], wm_ref[...], preferred_element_type=F32)
    hn = _rms(x, g_ref[...]).astype(BF16)
    acc = None
    for j in range(wo_ref.shape[0]):
        a = jnp.dot(hn, wi_ref[0, j], preferred_element_type=F32)
        b = jnp.dot(hn, wi_ref[1, j], preferred_element_type=F32)
        part = jnp.dot((_silu(a) * b).astype(BF16), wo_ref[j], preferred_element_type=F32)
        acc = part if acc is None else acc + part
    y = x + 0.5 * acc
    if final_norm:
        y = _rms(y, fg_ref[...])
    o_ref[...] = y


def _resident(shape):
    return pl.BlockSpec(shape, lambda *_: (0,) * len(shape), pipeline_mode=pl.Buffered(1))


def _ffn(x, g, w_in, w_out, mixer=None, final_g=None):
    n, d = x.shape
    in_specs = [
        pl.BlockSpec((ROW_TILE, d), lambda i: (i, 0)),
        _resident((1, d)),
        _resident(w_in.shape),
        _resident(w_out.shape),
    ]
    args = [x, g.reshape(1, d), w_in, w_out]
    if mixer is not None:
        o, w_mix = mixer
        in_specs += [pl.BlockSpec((ROW_TILE, o.shape[1]), lambda i: (i, 0)), _resident(w_mix.shape)]
        args += [o, w_mix]
    if final_g is not None:
        in_specs.append(_resident((1, d)))
        args.append(final_g.reshape(1, d))
    return pl.pallas_call(
        functools.partial(_ffn_kernel, has_mixer=mixer is not None, final_norm=final_g is not None),
        grid=(n // ROW_TILE,),
        in_specs=in_specs,
        out_specs=pl.BlockSpec((ROW_TILE, d), lambda i: (i, 0)),
        out_shape=jax.ShapeDtypeStruct((n, d), F32),
        name="ffn",
        compiler_params=_params("parallel"),
    )(*args)


PROJ_CHUNK = 512


def _norm_proj_kernel(x_ref, g_ref, wt_ref, *rest, has_rows):
    if has_rows:
        w2_ref, ot_ref, o2_ref = rest
    else:
        (ot_ref,) = rest
    hn = _rms(x_ref[...], g_ref[...]).astype(BF16)
    channels = ot_ref.shape[2]
    for c in range(0, channels, PROJ_CHUNK):
        r = lax.dot_general(wt_ref[c:c + PROJ_CHUNK, :], hn, _NT, preferred_element_type=F32).astype(BF16)
        for j in range(ot_ref.shape[1]):
            ot_ref[0, j, c:c + PROJ_CHUNK, :] = r[:, j * SEQ_TILE:(j + 1) * SEQ_TILE]
    if has_rows:
        o2_ref[...] = jnp.dot(hn, w2_ref[...], preferred_element_type=F32)


def _norm_proj(x, g, wt, w2, batch, seq):
    n, d = x.shape
    channels = wt.shape[0]
    per_seq = seq // ROW_TILE
    sub = ROW_TILE // SEQ_TILE
    in_specs = [
        pl.BlockSpec((ROW_TILE, d), lambda i: (i, 0)),
        pl.BlockSpec((1, d), lambda i: (0, 0)),
        pl.BlockSpec((channels, d), lambda i: (0, 0)),
    ]
    out_specs = [pl.BlockSpec((1, sub, channels, SEQ_TILE), lambda i: (i // per_seq, i % per_seq, 0, 0))]
    out_shape = [jax.ShapeDtypeStruct((batch, seq // SEQ_TILE, channels, SEQ_TILE), BF16)]
    args = [x, g.reshape(1, d), wt]
    if w2 is not None:
        n2 = w2.shape[1]
        in_specs.append(pl.BlockSpec((d, n2), lambda i: (0, 0)))
        out_specs.append(pl.BlockSpec((ROW_TILE, n2), lambda i: (i, 0)))
        out_shape.append(jax.ShapeDtypeStruct((n, n2), F32))
        args.append(w2)
    return pl.pallas_call(
        functools.partial(_norm_proj_kernel, has_rows=w2 is not None),
        grid=(n // ROW_TILE,),
        in_specs=in_specs,
        out_specs=out_specs,
        out_shape=out_shape,
        name="norm_proj",
        compiler_params=_params("parallel"),
    )(*args)


def _compress_kernel(x_ref, pos_ref, w1_ref, w2_ref, o_ref):
    x = x_ref[0, 0, 0]
    pos = pos_ref[0]
    half = x.shape[1]
    top = jnp.dot((x + pos[0:1]).astype(BF16), w1_ref[0, :half, :], preferred_element_type=F32)
    bot = jnp.dot((x + pos[1:2]).astype(BF16), w1_ref[0, half:, :], preferred_element_type=F32)
    pre = top + pltpu.roll(bot, bot.shape[0] - 1, axis=0)
    hid = _silu(pre).astype(BF16)
    o_ref[0, 0, 0] = jnp.dot(hid, w2_ref[0], preferred_element_type=F32).astype(BF16)


def _compress(xh, pos2, w1, w2):
    b, _, g, nh, hw = xh.shape
    hid = w1.shape[2]
    return pl.pallas_call(
        _compress_kernel,
        grid=(b, 2, g),
        in_specs=[
            pl.BlockSpec((1, 1, 1, nh, hw), lambda i, j, k: (i, j, k, 0, 0)),
            pl.BlockSpec((1, 2, hw), lambda i, j, k: (j, 0, 0)),
            pl.BlockSpec((1, 2 * hw, hid), lambda i, j, k: (j, 0, 0)),
            pl.BlockSpec((1, hid, HEAD_DIM), lambda i, j, k: (j, 0, 0)),
        ],
        out_specs=pl.BlockSpec((1, 1, 1, nh, HEAD_DIM), lambda i, j, k: (i, j, k, 0, 0)),
        out_shape=jax.ShapeDtypeStruct((b, 2, g, nh, HEAD_DIM), BF16),
        name="compress",
        compiler_params=_params("parallel", "parallel", "parallel"),
    )(xh, pos2, w1, w2)


def _split3(x):
    p1 = x.astype(BF16)
    r1 = x - p1.astype(F32)
    p2 = r1.astype(BF16)
    p3 = (r1 - p2.astype(F32)).astype(BF16)
    return p1, p2, p3


def _cmp_sel_kernel(q_ref, kc_ref, vct_ref, tb_ref, ovt_ref, oct_ref, sel_ref):
    qi = pl.program_id(2)
    tq = SEQ_TILE
    nrep = NSA_GROUP
    n_cmp_pad = kc_ref.shape[2]
    qt = jnp.concatenate([q_ref[0, 0, r * HEAD_DIM:(r + 1) * HEAD_DIM, :] for r in range(nrep)], axis=1)
    s = jnp.dot(kc_ref[0, 0], qt, preferred_element_type=F32)
    off = pl.multiple_of(n_cmp_pad - (tq // CMP_STRIDE) * qi, 8)
    bias = jnp.concatenate([tb_ref[r, pl.ds(off, n_cmp_pad), :] for r in range(nrep)], axis=1)
    s = s + bias
    m = jnp.max(s, axis=0, keepdims=True)
    e = jnp.exp2(s - m)
    assert tq & (tq - 1) == 0
    t = qi * tq + (lax.broadcasted_iota(jnp.int32, (1, nrep * tq), 1) & (tq - 1))
    p = e * jnp.where(t >= CMP_BLOCK - 1, 1.0 / jnp.sum(e, axis=0, keepdims=True), 0.0)

    oct = jnp.dot(vct_ref[0, 0], p.astype(BF16), preferred_element_type=F32)
    for r in range(nrep):
        oct_ref[0, 0, r] = oct[:, r * tq:(r + 1) * tq]

    psum = p[:, 0:tq]
    for r in range(1, nrep):
        psum = psum + p[:, r * tq:(r + 1) * tq]
    ovt = ovt_ref[...]
    imp = sum(jnp.dot(ovt, part, preferred_element_type=F32) for part in _split3(psum))

    n_sel = imp.shape[0]
    blk = lax.broadcasted_iota(jnp.int32, (n_sel, tq), 0)
    cur = jnp.right_shift(qi * tq + lax.broadcasted_iota(jnp.int32, (n_sel, tq), 1), SEL_BLOCK_SHIFT)
    forced = (blk == 0) | (blk == cur) | (blk == cur - 1)
    score = jnp.where(blk > cur, -SEL_FORCE, imp + jnp.where(forced, SEL_FORCE, 0.0))
    rows = 8
    groups = [score[g * rows:(g + 1) * rows] for g in range(n_sel // rows)]
    blk_in_group = lax.broadcasted_iota(jnp.int32, (rows, tq), 0)
    counts = [jnp.zeros((rows, tq), F32) for _ in groups]
    for sp in range(n_sel):
        row = jnp.broadcast_to(score[sp:sp + 1, :], (rows, tq))
        for g, sg in enumerate(groups):
            if g * rows > sp:
                beats = row >= sg
            elif g * rows + rows - 1 < sp:
                beats = row > sg
            else:
                tie = jnp.where(blk_in_group + g * rows > sp, 1.0, 0.0)
                counts[g] = counts[g] + jnp.where(row > sg, 1.0, jnp.where(row == sg, tie, 0.0))
                continue
            counts[g] = counts[g] + jnp.where(beats, 1.0, 0.0)
    cnt = jnp.concatenate(counts, axis=0)
    sel_ref[0, 0] = jnp.where(cnt < min(SEL_TOPK, n_sel), 1.0, 0.0).astype(BF16)


def _cmp_sel(qkv_t, kcmp, vcmp_t, tb, ovt):
    b, nt, _, _ = qkv_t.shape
    g, r, dh = NSA_KV_HEADS, NSA_GROUP, HEAD_DIM
    t = nt * SEQ_TILE
    n_pad = kcmp.shape[2]
    n_sel = ovt.shape[0]
    tq = SEQ_TILE
    return pl.pallas_call(
        _cmp_sel_kernel,
        grid=(b, g, t // tq),
        in_specs=[
            pl.BlockSpec((1, 1, r * dh, tq), lambda i, j, k: (i, k, j, 0)),
            pl.BlockSpec((1, 1, n_pad, dh), lambda i, j, k: (i, j, 0, 0)),
            pl.BlockSpec((1, 1, dh, n_pad), lambda i, j, k: (i, j, 0, 0)),
            pl.BlockSpec((r, 2 * n_pad, tq), lambda i, j, k: (j, 0, 0)),
            pl.BlockSpec((n_sel, n_pad), lambda i, j, k: (0, 0)),
        ],
        out_specs=[
            pl.BlockSpec((1, 1, r, dh, tq), lambda i, j, k: (i, j, 0, 0, k)),
            pl.BlockSpec((1, 1, n_sel, tq), lambda i, j, k: (i, j, 0, k)),
        ],
        out_shape=[
            jax.ShapeDtypeStruct((b, g, r, dh, t), F32),
            jax.ShapeDtypeStruct((b, g, n_sel, t), BF16),
        ],
        name="cmp_sel",
        compiler_params=_params("parallel", "parallel", "parallel"),
    )(qkv_t, kcmp, vcmp_t, tb, ovt)


def _ones_rows(width):
    return jnp.ones((FEAT_ROWS, width), BF16)


def _with_ones(v_t):
    return jnp.concatenate([v_t, _ones_rows(v_t.shape[1])], axis=0)


def _online_update(m_ref, acc_ref, slot, tiles, first):
    mt = None
    for s, _ in tiles:
        cm = jnp.max(s, axis=0, keepdims=True)
        mt = cm if mt is None else jnp.maximum(mt, cm)
    m_old = None if first else m_ref[slot]
    m_new = mt if first else jnp.maximum(m_old, mt)
    pv = None
    for s, va in tiles:
        d = jnp.dot(va, jnp.exp2(s - m_new).astype(BF16), preferred_element_type=F32)
        pv = d if pv is None else pv + d
    acc_ref[slot] = pv if first else jnp.exp2(m_old - m_new) * acc_ref[slot] + pv
    m_ref[slot] = m_new


def _pipelined_updates(m_ref, acc_ref, jobs):
    pending = [job[1]() for job in jobs[:QK_LOOKAHEAD]]
    for i, (slot, _, values_fn, first) in enumerate(jobs):
        current = pending.pop(0)
        if i + QK_LOOKAHEAD < len(jobs):
            pending.append(jobs[i + QK_LOOKAHEAD][1]())
        _online_update(m_ref, acc_ref, slot, list(zip(current, values_fn())), first)


def _for_far_tiles(n_far, run):
    n_quads = n_far // 4

    def quad(i, carry):
        run([[4 * i, 4 * i + 1], [4 * i + 2, 4 * i + 3]])
        return carry

    lax.fori_loop(0, n_quads, quad, 0)

    @pl.when((n_far & 2) != 0)
    def _():
        run([[4 * n_quads, 4 * n_quads + 1]])

    @pl.when((n_far & 1) != 0)
    def _():
        run([[n_far - 1]])


def _normalised(acc_ref, slot):
    acc = acc_ref[slot]
    return acc[:HEAD_DIM] / acc[HEAD_DIM:HEAD_DIM + 1]


def _sel_win_kernel(q_ref, ks_ref, vs_ref, kw_ref, vw_ref, sel_ref, oh_ref, tab_ref, far2_ref, gate_ref, oc_ref,
                    o_ref, ksrm_ref, kwrm_ref, m_ref, acc_ref):
    qi = pl.program_id(2)
    tq = SEQ_TILE
    nrep = NSA_GROUP
    gps = NSA_GROUPS_PER_STEP
    dh = HEAD_DIM
    pad = jnp.zeros((KEY_FEATS - dh, tq), BF16)

    @pl.when(qi == 0)
    def _():
        def to_rows(c, carry):
            for gg in range(gps):
                ksa = jnp.concatenate([ks_ref[0, c, gg * dh:(gg + 1) * dh, :], oh_ref[c]], axis=0)
                ksrm_ref[gg, c] = ksa.astype(F32).T.astype(BF16)
                kwa = jnp.concatenate([kw_ref[0, c, gg * dh:(gg + 1) * dh, :], pad], axis=0)
                kwrm_ref[gg, c] = kwa.astype(F32).T.astype(BF16)
            return carry

        lax.fori_loop(0, ks_ref.shape[1], to_rows, 0)

    q_sel, q_win = [], []
    for gg in range(gps):
        mq = ((sel_ref[0, gg].astype(F32) - 1.0) * (-NEG_INF)).astype(BF16)
        for r in range(nrep):
            q = q_ref[0, 0, (gg * nrep + r) * dh:(gg * nrep + r + 1) * dh, :]
            q_sel.append(jnp.concatenate([q, mq], axis=0))
            q_win.append(jnp.concatenate([q, pad], axis=0))

    def jobs(branch, cs, bias=None, first=False):
        k_ref, v_ref, qa = (ksrm_ref, vs_ref, q_sel) if branch == 0 else (kwrm_ref, vw_ref, q_win)

        def logits(gg, r):
            head = gg * nrep + r
            tiles = [jnp.dot(k_ref[gg, c], qa[head], preferred_element_type=F32) for c in cs]
            return tiles if bias is None else [s + bias(head) for s in tiles]

        def values(gg):
            return [_with_ones(v_ref[0, c, gg * dh:(gg + 1) * dh, :]) for c in cs]

        return [((gg * 2 + branch) * nrep + r, functools.partial(logits, gg, r), functools.partial(values, gg), first)
                for gg in range(gps) for r in range(nrep)]

    diag = lambda head: tab_ref[0, head, 1]
    prev = lambda head: tab_ref[0, head, 0]
    edge = lambda head: far2_ref[0]
    c_prev = jnp.maximum(qi - 1, 0)
    c_edge = jnp.maximum(qi - 2, 0)
    _pipelined_updates(m_ref, acc_ref,
                       jobs(0, [qi], diag, first=True) + jobs(1, [qi], diag, first=True)
                       + jobs(0, [c_prev], prev) + jobs(1, [c_prev], prev) + jobs(1, [c_edge], edge))

    _for_far_tiles(jnp.maximum(qi - 1, 0), lambda groups: _pipelined_updates(
        m_ref, acc_ref, [job for cs in groups for job in jobs(0, cs)]))

    outs = []
    for gg in range(gps):
        gates = jax.nn.sigmoid(gate_ref[0, gg])
        for r in range(nrep):
            outs.append(gates[3 * r:3 * r + 1] * oc_ref[0, gg, r]
                        + gates[3 * r + 1:3 * r + 2] * _normalised(acc_ref, (gg * 2) * nrep + r)
                        + gates[3 * r + 2:3 * r + 3] * _normalised(acc_ref, (gg * 2 + 1) * nrep + r))
    o_ref[0] = jnp.concatenate(outs, axis=0).T.astype(BF16)


def _sel_win(qkv_t, sel_t, onehot_t, tab, far2, gates_t, oc_t):
    b, nt, channels, tq = qkv_t.shape
    g, r, dh = NSA_KV_HEADS, NSA_GROUP, HEAD_DIM
    gps = NSA_GROUPS_PER_STEP
    hd = g * r * dh
    n_sel = sel_t.shape[2]
    kv_block = lambda which: pl.BlockSpec(
        (1, nt, gps * dh, tq), lambda i, j, k: (i, 0, (hd + which * g * dh) // (gps * dh) + j, 0))
    return pl.pallas_call(
        _sel_win_kernel,
        grid=(b, g // gps, nt),
        in_specs=[
            pl.BlockSpec((1, 1, gps * r * dh, tq), lambda i, j, k: (i, k, j, 0)),
            kv_block(0), kv_block(1), kv_block(2), kv_block(3),
            pl.BlockSpec((1, gps, n_sel, tq), lambda i, j, k: (i, j, 0, k)),
            pl.BlockSpec((nt, n_sel, tq), lambda i, j, k: (0, 0, 0)),
            pl.BlockSpec((1, gps * r, 2, tq, tq), lambda i, j, k: (jnp.minimum(k, tab.shape[0] - 1), j, 0, 0, 0)),
            pl.BlockSpec((1, tq, tq), lambda i, j, k: (jnp.minimum(k, far2.shape[0] - 1), 0, 0)),
            pl.BlockSpec((1, gps, FEAT_ROWS, tq), lambda i, j, k: (i, j, 0, k)),
            pl.BlockSpec((1, gps, r, dh, tq), lambda i, j, k: (i, j, 0, 0, k)),
        ],
        out_specs=pl.BlockSpec((1, tq, gps * r * dh), lambda i, j, k: (i, k, j)),
        out_shape=jax.ShapeDtypeStruct((b, nt * tq, hd), BF16),
        scratch_shapes=[pltpu.VMEM((gps, nt, tq, KEY_FEATS), BF16), pltpu.VMEM((gps, nt, tq, KEY_FEATS), BF16),
                        pltpu.VMEM((gps * 2 * r, 1, tq), F32), pltpu.VMEM((gps * 2 * r, ACC_ROWS, tq), F32)],
        name="sel_win",
        compiler_params=_params("parallel", "parallel", "arbitrary"),
    )(qkv_t, qkv_t, qkv_t, qkv_t, qkv_t, sel_t, onehot_t, tab, far2, gates_t, oc_t)


N_DECAY_PIECES = 3


def _forget_cumsum_kernel(fl_ref, bf_ref, piece_ref):
    x = fl_ref[0] + bf_ref[...]
    y = jnp.minimum(x, 0.0) - jnp.log1p(jnp.exp(-jnp.abs(x)))
    t = y.shape[1]
    lane = lax.broadcasted_iota(jnp.int32, y.shape, 1)
    k = 1
    while k < t:
        y = y + jnp.where(lane >= k, pltpu.roll(y, k, axis=1), 0.0)
        k *= 2
    for i, piece in enumerate(_split3(-LOG2E * y)):
        piece_ref[0, i] = piece


def _forget_cumsum(fl_t, b_f):
    b, h, t = fl_t.shape
    return pl.pallas_call(
        _forget_cumsum_kernel,
        grid=(b,),
        in_specs=[pl.BlockSpec((1, h, t), lambda i: (i, 0, 0)), pl.BlockSpec((h, 1), lambda i: (0, 0))],
        out_specs=pl.BlockSpec((1, N_DECAY_PIECES, h, t), lambda i: (i, 0, 0, 0)),
        out_shape=jax.ShapeDtypeStruct((b, N_DECAY_PIECES, h, t), BF16),
        name="forget_cumsum",
        compiler_params=_params("parallel"),
    )(fl_t, b_f.reshape(h, 1))


FOX_HEADS_PER_STEP = 8


def _fox_kernel(q_ref, k_ref, v_ref, kf_ref, tri_ref, o_ref, krm_ref, m_ref, acc_ref):
    qi = pl.program_id(2)
    tq = SEQ_TILE
    hp = FOX_HEADS_PER_STEP
    n_pieces = N_DECAY_PIECES
    pad = jnp.zeros((KEY_FEATS - HEAD_DIM - FEAT_ROWS, tq), BF16)

    @pl.when(qi == 0)
    def _():
        def to_rows(c, carry):
            for h in range(hp):
                ka = jnp.concatenate([k_ref[0, c, h * HEAD_DIM:(h + 1) * HEAD_DIM, :], kf_ref[0, c, h], pad], axis=0)
                krm_ref[h, c] = ka.astype(F32).T.astype(BF16)
            return carry

        lax.fori_loop(0, k_ref.shape[1], to_rows, 0)

    q_feat = jnp.where(lax.broadcasted_iota(jnp.int32, (FEAT_ROWS, tq), 0) < n_pieces, 1.0, 0.0).astype(BF16)
    qa = [jnp.concatenate([q_ref[0, 0, h * HEAD_DIM:(h + 1) * HEAD_DIM, :], q_feat, pad], axis=0)
          for h in range(hp)]

    def step(groups, first=False, bias=None):
        def logits(h, cs):
            tiles = [jnp.dot(krm_ref[h, c], qa[h], preferred_element_type=F32) for c in cs]
            return tiles if bias is None else [s + bias() for s in tiles]

        def values(h, cs):
            return [_with_ones(v_ref[0, c, h * HEAD_DIM:(h + 1) * HEAD_DIM, :]) for c in cs]

        jobs = [(h, functools.partial(logits, h, cs), functools.partial(values, h, cs), first)
                for cs in groups for h in range(hp)]
        _pipelined_updates(m_ref, acc_ref, jobs)

    step([[qi]], first=True, bias=lambda: tri_ref[...])
    _for_far_tiles(qi, step)
    o_ref[0] = jnp.concatenate([_normalised(acc_ref, h) for h in range(hp)], axis=0).T.astype(BF16)


def _fox_attention(q_t, kv_t, kfeat_t, tri):
    b, nt, hd, tq = q_t.shape
    hp, dh = FOX_HEADS_PER_STEP, HEAD_DIM
    n_groups = hd // (hp * dh)
    return pl.pallas_call(
        _fox_kernel,
        grid=(b, n_groups, nt),
        in_specs=[
            pl.BlockSpec((1, 1, hp * dh, tq), lambda i, j, k: (i, k, j, 0)),
            pl.BlockSpec((1, nt, hp * dh, tq), lambda i, j, k: (i, 0, j, 0)),
            pl.BlockSpec((1, nt, hp * dh, tq), lambda i, j, k: (i, 0, n_groups + j, 0)),
            pl.BlockSpec((1, nt, hp, FEAT_ROWS, tq), lambda i, j, k: (i, 0, j, 0, 0)),
            pl.BlockSpec((tq, tq), lambda i, j, k: (0, 0)),
        ],
        out_specs=pl.BlockSpec((1, tq, hp * dh), lambda i, j, k: (i, k, j)),
        out_shape=jax.ShapeDtypeStruct((b, nt * tq, hd), BF16),
        scratch_shapes=[pltpu.VMEM((hp, nt, tq, KEY_FEATS), BF16),
                        pltpu.VMEM((hp, 1, tq), F32), pltpu.VMEM((hp, ACC_ROWS, tq), F32)],
        name="fox_attention",
        compiler_params=_params("parallel", "parallel", "arbitrary"),
    )(q_t, kv_t, kv_t, kfeat_t, tri)


def _t5_bucket_np(dist):
    dist = np.maximum(dist, 0)
    max_exact = N_BUCKETS // 2
    ratio = np.maximum(dist, 1).astype(np.float32) / np.float32(max_exact)
    scaled = np.log(ratio) / np.float32(math.log(MAX_DISTANCE / max_exact))
    large = np.minimum(max_exact + (scaled * np.float32(N_BUCKETS - max_exact)).astype(np.int32), N_BUCKETS - 1)
    return np.where(dist < max_exact, dist, large).astype(np.int32)


_MASKED = MAX_DISTANCE + 1


def _dist_index(dist, valid):
    return np.where(valid, np.minimum(dist, MAX_DISTANCE), _MASKED).astype(np.int32)


def _toeplitz(f_ext, n):
    period = 2 * n
    flat = jnp.tile(f_ext, n)[..., :n * (period - 1)]
    return flat.reshape(f_ext.shape[:-1] + (n, period - 1))[..., :n]


def _key_query_delta(n):
    p = np.arange(2 * n)
    return np.where(p < n, p, p - 2 * n)


def _bias_tables(rel_bias, n_pad):
    bucket = _t5_bucket_np(np.arange(MAX_DISTANCE + 1))
    by_dist = jnp.concatenate([rel_bias[bucket], jnp.full((1, N_HEADS), NEG_INF, F32)], axis=0).T
    far_const = by_dist[:, MAX_DISTANCE:MAX_DISTANCE + 1]

    tq = SEQ_TILE
    u = np.arange(2 * n_pad) - n_pad
    dist = np.arange(tq)[None, :] - CMP_STRIDE * u[:, None] - (CMP_BLOCK - 1)
    varying = np.nonzero((dist.max(axis=1) >= 0) & (dist.min(axis=1) < MAX_DISTANCE))[0]
    lo, hi = int(varying[0]), int(varying[-1]) + 1
    strip = by_dist[:, _dist_index(dist[lo:hi], dist[lo:hi] >= 0)]
    tb = jnp.concatenate([
        jnp.broadcast_to(far_const[:, :, None], (N_HEADS, lo, tq)),
        strip,
        jnp.full((N_HEADS, 2 * n_pad - hi, tq), NEG_INF, F32)], axis=1)

    shifted = jnp.concatenate([by_dist[:, :-1] - far_const, by_dist[:, -1:]], axis=1)
    n = SEQ_TILE
    delta = _key_query_delta(n)
    prev = _toeplitz(shifted[:, _dist_index(delta + n, delta + n >= 0)], n)
    diag = _toeplitz(shifted[:, _dist_index(delta, delta >= 0)], n)
    tab = jnp.stack([prev, diag], axis=1)
    no_prev = jnp.stack([jnp.full_like(prev, NEG_INF), diag], axis=1)
    return LOG2E * tb, LOG2E * jnp.stack([no_prev, tab])


def _static_tiles(seq):
    n = SEQ_TILE
    key = np.arange(n)[:, None]
    query = np.arange(n)[None, :]
    tri = np.where(key <= query, 0.0, NEG_INF).astype(np.float32)
    far2 = np.where(query - key + 2 * n < WINDOW, 0.0, NEG_INF).astype(np.float32)
    assert WINDOW == 2 * n
    far2 = np.stack([np.full_like(far2, NEG_INF)] * 2 + [far2])
    n_sel = seq // SEL_BLOCK
    pos = np.arange(seq).reshape(seq // n, 1, n)
    onehot = (pos // SEL_BLOCK == np.arange(n_sel)[None, :, None]).astype(np.float32)
    return jnp.asarray(tri), jnp.asarray(far2), jnp.asarray(onehot, BF16)


def _overlap_t(n_pad, n_sel):
    n_cmp = n_pad - 1
    cmp_start = np.arange(n_pad) * CMP_STRIDE
    sel_start = np.arange(n_sel) * SEL_BLOCK
    ov = ((cmp_start[None, :] < sel_start[:, None] + SEL_BLOCK)
          & (cmp_start[None, :] + CMP_BLOCK > sel_start[:, None])
          & (np.arange(n_pad)[None, :] < n_cmp))
    return jnp.asarray(ov.astype(np.float32), BF16)


def _nsa_mixer(x, g, w_in, cmp_pos, cmp_w1, cmp_w2, rel_bias, batch, seq):
    n, d = x.shape
    G, R, dh = NSA_KV_HEADS, NSA_GROUP, HEAD_DIM
    hd, kvd = N_HEADS * dh, G * dh
    cols = [hd + i * kvd for i in range(7)]
    wq, wkc, wvc, wks, wvs, wkw, wvw, wgl = jnp.split(w_in, cols, axis=1)
    n_gate = wgl.shape[1]
    wt = jnp.concatenate([wq * Q_PRESCALE, wks, wvs, wkw, wvw], axis=1).T.astype(BF16)
    w2 = jnp.concatenate([wkc, wvc, wgl, jnp.zeros((d, -n_gate % 128), w_in.dtype)], axis=1).astype(BF16)
    qkv_t, f = _norm_proj(x, g, wt, w2, batch, seq)

    gates_t = f[:, 2 * kvd:2 * kvd + n_gate].reshape(batch, seq, G, R * 3).transpose(0, 2, 3, 1)
    gates_t = jnp.pad(gates_t, ((0, 0), (0, 0), (0, FEAT_ROWS - R * 3), (0, 0)))

    nh = seq // CMP_STRIDE
    xh = f[:, :2 * kvd].reshape(batch, nh, CMP_STRIDE, 2, G, dh).transpose(0, 3, 4, 1, 2, 5)
    xh = xh.reshape(batch, 2, G, nh, CMP_STRIDE * dh)
    cmp = _compress(xh, cmp_pos.reshape(2, 2, CMP_STRIDE * dh), cmp_w1.astype(BF16), cmp_w2.astype(BF16))
    kcmp = cmp[:, 0]
    vcmp_t = cmp[:, 1].transpose(0, 1, 3, 2)

    n_sel = seq // SEL_BLOCK
    tb, tab = _bias_tables(rel_bias, nh)
    _, far2, onehot_t = _static_tiles(seq)
    oc_t, sel_t = _cmp_sel(qkv_t, kcmp, vcmp_t, tb, _overlap_t(nh, n_sel))
    o = _sel_win(qkv_t, sel_t, onehot_t, tab, far2, gates_t, oc_t)
    return o.reshape(n, hd)


def _shared_kv(x, g, w_kvf, b_f, batch, seq):
    hd = N_HEADS * HEAD_DIM
    wt = w_kvf[:, :2 * hd].T.astype(BF16)
    wf = w_kvf[:, 2 * hd:]
    w2 = jnp.concatenate([wf, jnp.zeros((wf.shape[0], -wf.shape[1] % 128), wf.dtype)], axis=1).astype(BF16)
    kv_t, f = _norm_proj(x, g, wt, w2, batch, seq)
    fl_t = f[:, :N_HEADS].reshape(batch, seq, N_HEADS).transpose(0, 2, 1)
    pieces = _forget_cumsum(fl_t, b_f.astype(F32)).transpose(0, 2, 1, 3)
    pieces = jnp.pad(pieces, ((0, 0), (0, 0), (0, FEAT_ROWS - N_DECAY_PIECES), (0, 0)))
    kfeat_t = pieces.reshape(batch, N_HEADS, FEAT_ROWS, seq // SEQ_TILE, SEQ_TILE).transpose(0, 3, 1, 2, 4)
    return kv_t, kfeat_t


def _fox_mixer(x, g, w_q, kv, batch, seq):
    n, _ = x.shape
    hd = N_HEADS * HEAD_DIM
    (q_t,) = _norm_proj(x, g, (w_q * Q_PRESCALE).T.astype(BF16), None, batch, seq)
    tri, _, _ = _static_tiles(seq)
    o = _fox_attention(q_t, *kv, tri)
    return o.reshape(n, hd)


def kernel(x, norm_g, ffn_w_in, ffn_w_out, nsa_w_in, nsa_cmp_pos, nsa_cmp_w1, nsa_cmp_w2, nsa_w_out, rel_bias,
           kv_norm_g, fox_w_kvf, fox_b_f, fox_w_q, fox_w_out, final_g):
    batch, seq, d = x.shape
    depth = norm_g.shape[0]
    n_a = nsa_w_in.shape[0]
    xf = x.reshape(batch * seq, d)
    d_ff = ffn_w_out.shape[2]
    chunks = d_ff // FF_TILE
    w_in = ffn_w_in.astype(BF16).reshape(depth, 2, d, 2, chunks, FF_TILE).transpose(0, 1, 3, 4, 2, 5)
    w_out = ffn_w_out.astype(BF16).reshape(depth, 2, chunks, FF_TILE, d)
    kv = None
    for l in range(depth):
        if l == n_a:
            kv = _shared_kv(xf, kv_norm_g, fox_w_kvf, fox_b_f, batch, seq)
        xf = _ffn(xf, norm_g[l, 0], w_in[l, 0], w_out[l, 0])
        if l < n_a:
            o = _nsa_mixer(xf, norm_g[l, 1], nsa_w_in[l], nsa_cmp_pos[l], nsa_cmp_w1[l], nsa_cmp_w2[l],
                           rel_bias, batch, seq)
            w_mix = nsa_w_out[l]
        else:
            o = _fox_mixer(xf, norm_g[l, 1], fox_w_q[l - n_a], kv, batch, seq)
            w_mix = fox_w_out[l - n_a]
        xf = _ffn(xf, norm_g[l, 2], w_in[l, 1], w_out[l, 1], mixer=(o, w_mix.astype(BF16)),
                  final_g=final_g if l == depth - 1 else None)
    return xf.reshape(batch, seq, d)
```

```python
import functools
import math

import numpy as np
import jax
import jax.numpy as jnp
from jax import lax
from jax.experimental import pallas as pl
from jax.experimental.pallas import tpu as pltpu

N_HEADS = 16
HEAD_DIM = 64
NSA_KV_HEADS = 4
NSA_GROUP = N_HEADS // NSA_KV_HEADS
NSA_GROUPS_PER_STEP = 2
CMP_BLOCK = 32
CMP_STRIDE = 16
SEL_BLOCK = 64
SEL_BLOCK_SHIFT = SEL_BLOCK.bit_length() - 1
assert 1 << SEL_BLOCK_SHIFT == SEL_BLOCK
SEL_TOPK = 16
WINDOW = 512
N_BUCKETS = 32
MAX_DISTANCE = 128
EPS = 1e-6
NEG_INF = -1e30
LOG2E = math.log2(math.e)
Q_PRESCALE = LOG2E * HEAD_DIM ** -0.5
SEL_FORCE = 1e4

F32 = jnp.float32
BF16 = jnp.bfloat16

VMEM_LIMIT_BYTES = 48 * 1024 * 1024

ROW_TILE = 512
FF_TILE = 256
SEQ_TILE = 256
FEAT_ROWS = 16
ACC_ROWS = HEAD_DIM + FEAT_ROWS
QK_LOOKAHEAD = 5
KEY_FEATS = 128

_NT = (((1,), (1,)), ((), ()))
_TN = (((0,), (0,)), ((), ()))


def _params(*sem):
    return pltpu.CompilerParams(dimension_semantics=sem, vmem_limit_bytes=VMEM_LIMIT_BYTES)


def _rms(x, g):
    return x * lax.rsqrt(jnp.mean(x * x, axis=-1, keepdims=True) + EPS) * g


def _silu(a):
    return a * jax.nn.sigmoid(a)


def _ffn_kernel(x_ref, g_ref, wi_ref, wo_ref, *rest, has_mixer, final_norm):
    d_ff = wo_ref.shape[2]
    rest = list(rest)
    mix_ref, wm_ref = (rest.pop(0), rest.pop(0)) if has_mixer else (None, None)
    fg_ref = rest.pop(0) if final_norm else None
    (o_ref,) = rest
    x = x_ref[...]
    if has_mixer:
        x = x + jnp.dot(mix_ref[...], wm_ref[...], preferred_element_type=F32)
    hn = _rms(x, g_ref[...]).astype(BF16)
    acc = None
    for c in range(0, d_ff, FF_TILE):
        a = jnp.dot(hn, wi_ref[0, 0, :, c:c + FF_TILE], preferred_element_type=F32)
        b = jnp.dot(hn, wi_ref[0, 0, :, d_ff + c:d_ff + c + FF_TILE], preferred_element_type=F32)
        part = jnp.dot((_silu(a) * b).astype(BF16), wo_ref[0, 0, c:c + FF_TILE, :], preferred_element_type=F32)
        acc = part if acc is None else acc + part
    y = x + 0.5 * acc
    if final_norm:
        y = _rms(y, fg_ref[...])
    o_ref[...] = y


def _resident(shape, index=None):
    index = (0,) * len(shape) if index is None else index
    return pl.BlockSpec(shape, lambda *_: index, pipeline_mode=pl.Buffered(1))


def _ffn(x, g, w_in, w_out, which, mixer=None, final_g=None):
    n, d = x.shape
    in_specs = [
        pl.BlockSpec((ROW_TILE, d), lambda i: (i, 0)),
        _resident((1, d)),
        _resident((1, 1) + w_in.shape[2:], which + (0, 0)),
        _resident((1, 1) + w_out.shape[2:], which + (0, 0)),
    ]
    args = [x, g.reshape(1, d), w_in, w_out]
    if mixer is not None:
        o, w_mix = mixer
        in_specs += [pl.BlockSpec((ROW_TILE, o.shape[1]), lambda i: (i, 0)), _resident(w_mix.shape)]
        args += [o, w_mix]
    if final_g is not None:
        in_specs.append(_resident((1, d)))
        args.append(final_g.reshape(1, d))
    return pl.pallas_call(
        functools.partial(_ffn_kernel, has_mixer=mixer is not None, final_norm=final_g is not None),
        grid=(n // ROW_TILE,),
        in_specs=in_specs,
        out_specs=pl.BlockSpec((ROW_TILE, d), lambda i: (i, 0)),
        out_shape=jax.ShapeDtypeStruct((n, d), F32),
        name="ffn",
        compiler_params=_params("parallel"),
    )(*args)


PROJ_CHUNK = 512


def _norm_proj_kernel(x_ref, g_ref, wt_ref, *rest, has_rows):
    if has_rows:
        w2_ref, ot_ref, o2_ref = rest
    else:
        (ot_ref,) = rest
    hn = _rms(x_ref[...], g_ref[...]).astype(BF16)
    channels = ot_ref.shape[2]
    for c in range(0, channels, PROJ_CHUNK):
        r = lax.dot_general(wt_ref[c:c + PROJ_CHUNK, :], hn, _NT, preferred_element_type=F32).astype(BF16)
        for j in range(ot_ref.shape[1]):
            ot_ref[0, j, c:c + PROJ_CHUNK, :] = r[:, j * SEQ_TILE:(j + 1) * SEQ_TILE]
    if has_rows:
        o2_ref[...] = jnp.dot(hn, w2_ref[...], preferred_element_type=F32)


def _norm_proj(x, g, wt, w2, batch, seq):
    n, d = x.shape
    channels = wt.shape[0]
    per_seq = seq // ROW_TILE
    sub = ROW_TILE // SEQ_TILE
    in_specs = [
        pl.BlockSpec((ROW_TILE, d), lambda i: (i, 0)),
        pl.BlockSpec((1, d), lambda i: (0, 0)),
        pl.BlockSpec((channels, d), lambda i: (0, 0)),
    ]
    out_specs = [pl.BlockSpec((1, sub, channels, SEQ_TILE), lambda i: (i // per_seq, i % per_seq, 0, 0))]
    out_shape = [jax.ShapeDtypeStruct((batch, seq // SEQ_TILE, channels, SEQ_TILE), BF16)]
    args = [x, g.reshape(1, d), wt]
    if w2 is not None:
        n2 = w2.shape[1]
        in_specs.append(pl.BlockSpec((d, n2), lambda i: (0, 0)))
        out_specs.append(pl.BlockSpec((ROW_TILE, n2), lambda i: (i, 0)))
        out_shape.append(jax.ShapeDtypeStruct((n, n2), F32))
        args.append(w2)
    return pl.pallas_call(
        functools.partial(_norm_proj_kernel, has_rows=w2 is not None),
        grid=(n // ROW_TILE,),
        in_specs=in_specs,
        out_specs=out_specs,
        out_shape=out_shape,
        name="norm_proj",
        compiler_params=_params("parallel"),
    )(*args)


def _compress_kernel(*refs):
    *f_refs, pos_ref, w1_ref, w2_ref, o_ref = refs
    nh = f_refs[0].shape[0] // CMP_STRIDE
    dh = HEAD_DIM
    per_chunk = f_refs[0].shape[1] // dh
    for chunk, f_ref in enumerate(f_refs):
        rows = [f_ref[pl.ds(j, nh, stride=CMP_STRIDE), :] for j in range(CMP_STRIDE)]
        for sub in range(per_chunk):
            kv, g = divmod(chunk * per_chunk + sub, NSA_KV_HEADS)
            x = jnp.concatenate([r[:, sub * dh:(sub + 1) * dh] for r in rows], axis=1)
            pos = pos_ref[kv]
            half = x.shape[1]
            top = jnp.dot((x + pos[0:1]).astype(BF16), w1_ref[kv, :half, :], preferred_element_type=F32)
            bot = jnp.dot((x + pos[1:2]).astype(BF16), w1_ref[kv, half:, :], preferred_element_type=F32)
            pre = top + pltpu.roll(bot, nh - 1, axis=0)
            hid = _silu(pre).astype(BF16)
            o_ref[0, kv, g] = jnp.dot(hid, w2_ref[kv], preferred_element_type=F32).astype(BF16)


def _compress(f, pos2, w1, w2, batch, seq):
    g, dh = NSA_KV_HEADS, HEAD_DIM
    nh = seq // CMP_STRIDE
    lanes = 128
    n_chunks = 2 * g * dh // lanes
    return pl.pallas_call(
        _compress_kernel,
        grid=(batch,),
        in_specs=[pl.BlockSpec((seq, lanes), functools.partial(lambda c, i: (i, c), c)) for c in range(n_chunks)] + [
            _resident(pos2.shape),
            _resident(w1.shape),
            _resident(w2.shape),
        ],
        out_specs=pl.BlockSpec((1, 2, g, nh, dh), lambda i: (i, 0, 0, 0, 0)),
        out_shape=jax.ShapeDtypeStruct((batch, 2, g, nh, dh), BF16),
        name="compress",
        compiler_params=_params("parallel"),
    )(*([f] * n_chunks), pos2, w1, w2)


def _split3(x):
    p1 = x.astype(BF16)
    r1 = x - p1.astype(F32)
    p2 = r1.astype(BF16)
    p3 = (r1 - p2.astype(F32)).astype(BF16)
    return p1, p2, p3


def _cmp_sel_kernel(q_ref, kc_ref, vct_ref, tb_ref, ovt_ref, oct_ref, sel_ref, cnt_ref):
    qi = pl.program_id(2)
    tq = SEQ_TILE
    nrep = NSA_GROUP
    n_cmp_pad = kc_ref.shape[2]
    qt = jnp.concatenate([q_ref[0, 0, r * HEAD_DIM:(r + 1) * HEAD_DIM, :] for r in range(nrep)], axis=1)
    s = jnp.dot(kc_ref[0, 0], qt, preferred_element_type=F32)
    off = pl.multiple_of(n_cmp_pad - (tq // CMP_STRIDE) * qi, 8)
    bias = jnp.concatenate([tb_ref[r, pl.ds(off, n_cmp_pad), :] for r in range(nrep)], axis=1)
    s = s + bias
    m = jnp.max(s, axis=0, keepdims=True)
    e = jnp.exp2(s - m)
    assert tq & (tq - 1) == 0
    t = qi * tq + (lax.broadcasted_iota(jnp.int32, (1, nrep * tq), 1) & (tq - 1))
    p = e * jnp.where(t >= CMP_BLOCK - 1, 1.0 / jnp.sum(e, axis=0, keepdims=True), 0.0)

    oct = jnp.dot(vct_ref[0, 0], p.astype(BF16), preferred_element_type=F32)
    for r in range(nrep):
        oct_ref[0, 0, r] = oct[:, r * tq:(r + 1) * tq]

    psum = p[:, 0:tq]
    for r in range(1, nrep):
        psum = psum + p[:, r * tq:(r + 1) * tq]
    ovt = ovt_ref[...]
    imp = sum(jnp.dot(ovt, part, preferred_element_type=F32) for part in _split3(psum))

    n_sel = imp.shape[0]
    blk = lax.broadcasted_iota(jnp.int32, (n_sel, tq), 0)
    cur = jnp.right_shift(qi * tq + lax.broadcasted_iota(jnp.int32, (n_sel, tq), 1), SEL_BLOCK_SHIFT)
    forced = (blk == 0) | (blk == cur) | (blk == cur - 1)
    score = jnp.where(blk > cur, -SEL_FORCE, imp + jnp.where(forced, SEL_FORCE, 0.0))
    rows = 8
    n_groups = n_sel // rows
    groups = [score[g * rows:(g + 1) * rows] for g in range(n_groups)]
    blk_in_group = lax.broadcasted_iota(jnp.int32, (rows, tq), 0)
    last_group = jnp.right_shift(qi * tq + tq - 1, SEL_BLOCK_SHIFT) // rows
    cnt_ref[...] = jnp.zeros_like(cnt_ref)

    def count(k, g):
        sg = groups[g]
        total = None
        for sp in range(k * rows, (k + 1) * rows):
            row = jnp.broadcast_to(score[sp:sp + 1, :], (rows, tq))
            if g > k:
                one = jnp.where(row >= sg, 1.0, 0.0)
            elif g < k:
                one = jnp.where(row > sg, 1.0, 0.0)
            else:
                tie = jnp.where(blk_in_group + g * rows > sp, 1.0, 0.0)
                one = jnp.where(row > sg, 1.0, jnp.where(row == sg, tie, 0.0))
            total = one if total is None else total + one
        return total

    for level in range(n_groups):
        @pl.when(level <= last_group)
        def _():
            for g in range(level + 1):
                cnt_ref[g * rows:(g + 1) * rows, :] += count(level, g)
            extra = None
            for k in range(level):
                c = count(k, level)
                extra = c if extra is None else extra + c
            if extra is not None:
                cnt_ref[level * rows:(level + 1) * rows, :] += extra

    sel_ref[0, 0] = jnp.where(cnt_ref[...] < min(SEL_TOPK, n_sel), 1.0, 0.0).astype(BF16)


def _cmp_sel(qkv_t, kcmp, vcmp_t, tb, ovt):
    b, nt, _, _ = qkv_t.shape
    g, r, dh = NSA_KV_HEADS, NSA_GROUP, HEAD_DIM
    t = nt * SEQ_TILE
    n_pad = kcmp.shape[2]
    n_sel = ovt.shape[0]
    tq = SEQ_TILE
    return pl.pallas_call(
        _cmp_sel_kernel,
        grid=(b, g, t // tq),
        in_specs=[
            pl.BlockSpec((1, 1, r * dh, tq), lambda i, j, k: (i, k, j, 0)),
            pl.BlockSpec((1, 1, n_pad, dh), lambda i, j, k: (i, j, 0, 0)),
            pl.BlockSpec((1, 1, dh, n_pad), lambda i, j, k: (i, j, 0, 0)),
            pl.BlockSpec((r, 2 * n_pad, tq), lambda i, j, k: (j, 0, 0)),
            pl.BlockSpec((n_sel, n_pad), lambda i, j, k: (0, 0)),
        ],
        out_specs=[
            pl.BlockSpec((1, 1, r, dh, tq), lambda i, j, k: (i, j, 0, 0, k)),
            pl.BlockSpec((1, 1, n_sel, tq), lambda i, j, k: (i, j, 0, k)),
        ],
        out_shape=[
            jax.ShapeDtypeStruct((b, g, r, dh, t), F32),
            jax.ShapeDtypeStruct((b, g, n_sel, t), BF16),
        ],
        scratch_shapes=[pltpu.VMEM((n_sel, tq), F32)],
        name="cmp_sel",
        compiler_params=_params("parallel", "parallel", "parallel"),
    )(qkv_t, kcmp, vcmp_t, tb, ovt)


def _ones_rows(width):
    return jnp.ones((FEAT_ROWS, width), BF16)


def _with_ones(v_t):
    return jnp.concatenate([v_t, _ones_rows(v_t.shape[1])], axis=0)


def _online_update(m_ref, acc_ref, slot, tiles, first):
    mt = None
    for s, _ in tiles:
        cm = jnp.max(s, axis=0, keepdims=True)
        mt = cm if mt is None else jnp.maximum(mt, cm)
    m_old = None if first else m_ref[slot]
    m_new = mt if first else jnp.maximum(m_old, mt)
    pv = None
    for s, va in tiles:
        d = jnp.dot(va, jnp.exp2(s - m_new).astype(BF16), preferred_element_type=F32)
        pv = d if pv is None else pv + d
    acc_ref[slot] = pv if first else jnp.exp2(m_old - m_new) * acc_ref[slot] + pv
    m_ref[slot] = m_new


def _pipelined_updates(m_ref, acc_ref, jobs):
    pending = [job[1]() for job in jobs[:QK_LOOKAHEAD]]
    for i, (slot, _, values_fn, first) in enumerate(jobs):
        current = pending.pop(0)
        if i + QK_LOOKAHEAD < len(jobs):
            pending.append(jobs[i + QK_LOOKAHEAD][1]())
        _online_update(m_ref, acc_ref, slot, list(zip(current, values_fn())), first)


def _for_far_tiles(n_far, run):
    n_quads = n_far // 4

    def quad(i, carry):
        run([[4 * i, 4 * i + 1], [4 * i + 2, 4 * i + 3]])
        return carry

    lax.fori_loop(0, n_quads, quad, 0)

    @pl.when((n_far & 2) != 0)
    def _():
        run([[4 * n_quads, 4 * n_quads + 1]])

    @pl.when((n_far & 1) != 0)
    def _():
        run([[n_far - 1]])


def _normalised(acc_ref, slot):
    acc = acc_ref[slot]
    return acc[:HEAD_DIM] / acc[HEAD_DIM:HEAD_DIM + 1]


def _sel_win_kernel(q_ref, ks_ref, vs_ref, kw_ref, vw_ref, sel_ref, oh_ref, tab_ref, far2_ref, gate_ref, oc_ref,
                    o_ref, ksrm_ref, kwrm_ref, m_ref, acc_ref):
    qi = pl.program_id(2)
    tq = SEQ_TILE
    nrep = NSA_GROUP
    gps = NSA_GROUPS_PER_STEP
    dh = HEAD_DIM
    pad = jnp.zeros((KEY_FEATS - dh, tq), BF16)

    @pl.when(qi == 0)
    def _():
        def to_rows(c, carry):
            for gg in range(gps):
                ksa = jnp.concatenate([ks_ref[0, c, gg * dh:(gg + 1) * dh, :], oh_ref[c]], axis=0)
                ksrm_ref[gg, c] = ksa.astype(F32).T.astype(BF16)
                kwa = jnp.concatenate([kw_ref[0, c, gg * dh:(gg + 1) * dh, :], pad], axis=0)
                kwrm_ref[gg, c] = kwa.astype(F32).T.astype(BF16)
            return carry

        lax.fori_loop(0, ks_ref.shape[1], to_rows, 0)

    q_sel, q_win = [], []
    for gg in range(gps):
        mq = ((sel_ref[0, gg].astype(F32) - 1.0) * (-NEG_INF)).astype(BF16)
        for r in range(nrep):
            q = q_ref[0, 0, (gg * nrep + r) * dh:(gg * nrep + r + 1) * dh, :]
            q_sel.append(jnp.concatenate([q, mq], axis=0))
            q_win.append(jnp.concatenate([q, pad], axis=0))

    def jobs(branch, cs, bias=None, first=False):
        k_ref, v_ref, qa = (ksrm_ref, vs_ref, q_sel) if branch == 0 else (kwrm_ref, vw_ref, q_win)

        def logits(gg, r):
            head = gg * nrep + r
            tiles = [jnp.dot(k_ref[gg, c], qa[head], preferred_element_type=F32) for c in cs]
            return tiles if bias is None else [s + bias(head) for s in tiles]

        def values(gg):
            return [_with_ones(v_ref[0, c, gg * dh:(gg + 1) * dh, :]) for c in cs]

        return [((gg * 2 + branch) * nrep + r, functools.partial(logits, gg, r), functools.partial(values, gg), first)
                for gg in range(gps) for r in range(nrep)]

    diag = lambda head: tab_ref[0, head, 1]
    prev = lambda head: tab_ref[0, head, 0]
    edge = lambda head: far2_ref[0]
    c_prev = jnp.maximum(qi - 1, 0)
    c_edge = jnp.maximum(qi - 2, 0)
    _pipelined_updates(m_ref, acc_ref,
                       jobs(0, [qi], diag, first=True) + jobs(1, [qi], diag, first=True)
                       + jobs(0, [c_prev], prev) + jobs(1, [c_prev], prev) + jobs(1, [c_edge], edge))

    _for_far_tiles(jnp.maximum(qi - 1, 0), lambda groups: _pipelined_updates(
        m_ref, acc_ref, [job for cs in groups for job in jobs(0, cs)]))

    outs = []
    for gg in range(gps):
        gates = jax.nn.sigmoid(gate_ref[0, gg])
        for r in range(nrep):
            outs.append(gates[3 * r:3 * r + 1] * oc_ref[0, gg, r]
                        + gates[3 * r + 1:3 * r + 2] * _normalised(acc_ref, (gg * 2) * nrep + r)
                        + gates[3 * r + 2:3 * r + 3] * _normalised(acc_ref, (gg * 2 + 1) * nrep + r))
    o_ref[0] = jnp.concatenate(outs, axis=0).T.astype(BF16)


def _sel_win(qkv_t, sel_t, onehot_t, tab, far2, gates_t, oc_t):
    b, nt, channels, tq = qkv_t.shape
    g, r, dh = NSA_KV_HEADS, NSA_GROUP, HEAD_DIM
    gps = NSA_GROUPS_PER_STEP
    hd = g * r * dh
    n_sel = sel_t.shape[2]
    kv_block = lambda which: pl.BlockSpec(
        (1, nt, gps * dh, tq), lambda i, j, k: (i, 0, (hd + which * g * dh) // (gps * dh) + j, 0))
    return pl.pallas_call(
        _sel_win_kernel,
        grid=(b, g // gps, nt),
        in_specs=[
            pl.BlockSpec((1, 1, gps * r * dh, tq), lambda i, j, k: (i, k, j, 0)),
            kv_block(0), kv_block(1), kv_block(2), kv_block(3),
            pl.BlockSpec((1, gps, n_sel, tq), lambda i, j, k: (i, j, 0, k)),
            pl.BlockSpec((nt, n_sel, tq), lambda i, j, k: (0, 0, 0)),
            pl.BlockSpec((1, gps * r, 2, tq, tq), lambda i, j, k: (jnp.minimum(k, tab.shape[0] - 1), j, 0, 0, 0)),
            pl.BlockSpec((1, tq, tq), lambda i, j, k: (jnp.minimum(k, far2.shape[0] - 1), 0, 0)),
            pl.BlockSpec((1, gps, FEAT_ROWS, tq), lambda i, j, k: (i, j, 0, k)),
            pl.BlockSpec((1, gps, r, dh, tq), lambda i, j, k: (i, j, 0, 0, k)),
        ],
        out_specs=pl.BlockSpec((1, tq, gps * r * dh), lambda i, j, k: (i, k, j)),
        out_shape=jax.ShapeDtypeStruct((b, nt * tq, hd), BF16),
        scratch_shapes=[pltpu.VMEM((gps, nt, tq, KEY_FEATS), BF16), pltpu.VMEM((gps, nt, tq, KEY_FEATS), BF16),
                        pltpu.VMEM((gps * 2 * r, 1, tq), F32), pltpu.VMEM((gps * 2 * r, ACC_ROWS, tq), F32)],
        name="sel_win",
        compiler_params=_params("parallel", "parallel", "arbitrary"),
    )(qkv_t, qkv_t, qkv_t, qkv_t, qkv_t, sel_t, onehot_t, tab, far2, gates_t, oc_t)


N_DECAY_PIECES = 3


def _forget_cumsum_kernel(fl_ref, bf_ref, piece_ref):
    x = fl_ref[0] + bf_ref[...]
    y = jnp.minimum(x, 0.0) - jnp.log1p(jnp.exp(-jnp.abs(x)))
    t = y.shape[1]
    lane = lax.broadcasted_iota(jnp.int32, y.shape, 1)
    k = 1
    while k < t:
        y = y + jnp.where(lane >= k, pltpu.roll(y, k, axis=1), 0.0)
        k *= 2
    for i, piece in enumerate(_split3(-LOG2E * y)):
        piece_ref[0, i] = piece


def _forget_cumsum(fl_t, b_f):
    b, h, t = fl_t.shape
    return pl.pallas_call(
        _forget_cumsum_kernel,
        grid=(b,),
        in_specs=[pl.BlockSpec((1, h, t), lambda i: (i, 0, 0)), pl.BlockSpec((h, 1), lambda i: (0, 0))],
        out_specs=pl.BlockSpec((1, N_DECAY_PIECES, h, t), lambda i: (i, 0, 0, 0)),
        out_shape=jax.ShapeDtypeStruct((b, N_DECAY_PIECES, h, t), BF16),
        name="forget_cumsum",
        compiler_params=_params("parallel"),
    )(fl_t, b_f.reshape(h, 1))


FOX_HEADS_PER_STEP = 8


def _fox_kernel(q_ref, k_ref, v_ref, kf_ref, tri_ref, o_ref, krm_ref, m_ref, acc_ref):
    qi = pl.program_id(2)
    tq = SEQ_TILE
    hp = FOX_HEADS_PER_STEP
    n_pieces = N_DECAY_PIECES
    pad = jnp.zeros((KEY_FEATS - HEAD_DIM - FEAT_ROWS, tq), BF16)

    @pl.when(qi == 0)
    def _():
        def to_rows(c, carry):
            for h in range(hp):
                ka = jnp.concatenate([k_ref[0, c, h * HEAD_DIM:(h + 1) * HEAD_DIM, :], kf_ref[0, c, h], pad], axis=0)
                krm_ref[h, c] = ka.astype(F32).T.astype(BF16)
            return carry

        lax.fori_loop(0, k_ref.shape[1], to_rows, 0)

    q_feat = jnp.where(lax.broadcasted_iota(jnp.int32, (FEAT_ROWS, tq), 0) < n_pieces, 1.0, 0.0).astype(BF16)
    qa = [jnp.concatenate([q_ref[0, 0, h * HEAD_DIM:(h + 1) * HEAD_DIM, :], q_feat, pad], axis=0)
          for h in range(hp)]

    def step(groups, first=False, bias=None):
        def logits(h, cs):
            tiles = [jnp.dot(krm_ref[h, c], qa[h], preferred_element_type=F32) for c in cs]
            return tiles if bias is None else [s + bias() for s in tiles]

        def values(h, cs):
            return [_with_ones(v_ref[0, c, h * HEAD_DIM:(h + 1) * HEAD_DIM, :]) for c in cs]

        jobs = [(h, functools.partial(logits, h, cs), functools.partial(values, h, cs), first)
                for cs in groups for h in range(hp)]
        _pipelined_updates(m_ref, acc_ref, jobs)

    step([[qi]], first=True, bias=lambda: tri_ref[...])
    _for_far_tiles(qi, step)
    o_ref[0] = jnp.concatenate([_normalised(acc_ref, h) for h in range(hp)], axis=0).T.astype(BF16)


def _fox_attention(q_t, kv_t, kfeat_t, tri):
    b, nt, hd, tq = q_t.shape
    hp, dh = FOX_HEADS_PER_STEP, HEAD_DIM
    n_groups = hd // (hp * dh)
    return pl.pallas_call(
        _fox_kernel,
        grid=(b, n_groups, nt),
        in_specs=[
            pl.BlockSpec((1, 1, hp * dh, tq), lambda i, j, k: (i, k, j, 0)),
            pl.BlockSpec((1, nt, hp * dh, tq), lambda i, j, k: (i, 0, j, 0)),
            pl.BlockSpec((1, nt, hp * dh, tq), lambda i, j, k: (i, 0, n_groups + j, 0)),
            pl.BlockSpec((1, nt, hp, FEAT_ROWS, tq), lambda i, j, k: (i, 0, j, 0, 0)),
            pl.BlockSpec((tq, tq), lambda i, j, k: (0, 0)),
        ],
        out_specs=pl.BlockSpec((1, tq, hp * dh), lambda i, j, k: (i, k, j)),
        out_shape=jax.ShapeDtypeStruct((b, nt * tq, hd), BF16),
        scratch_shapes=[pltpu.VMEM((hp, nt, tq, KEY_FEATS), BF16),
                        pltpu.VMEM((hp, 1, tq), F32), pltpu.VMEM((hp, ACC_ROWS, tq), F32)],
        name="fox_attention",
        compiler_params=_params("parallel", "parallel", "arbitrary"),
    )(q_t, kv_t, kv_t, kfeat_t, tri)


def _t5_bucket_np(dist):
    dist = np.maximum(dist, 0)
    max_exact = N_BUCKETS // 2
    ratio = np.maximum(dist, 1).astype(np.float32) / np.float32(max_exact)
    scaled = np.log(ratio) / np.float32(math.log(MAX_DISTANCE / max_exact))
    large = np.minimum(max_exact + (scaled * np.float32(N_BUCKETS - max_exact)).astype(np.int32), N_BUCKETS - 1)
    return np.where(dist < max_exact, dist, large).astype(np.int32)


_MASKED = MAX_DISTANCE + 1


def _dist_index(dist, valid):
    return np.where(valid, np.minimum(dist, MAX_DISTANCE), _MASKED).astype(np.int32)


def _toeplitz(f_ext, n):
    period = 2 * n
    flat = jnp.tile(f_ext, n)[..., :n * (period - 1)]
    return flat.reshape(f_ext.shape[:-1] + (n, period - 1))[..., :n]


def _key_query_delta(n):
    p = np.arange(2 * n)
    return np.where(p < n, p, p - 2 * n)


def _bias_tables(rel_bias, n_pad):
    bucket = _t5_bucket_np(np.arange(MAX_DISTANCE + 1))
    by_dist = jnp.concatenate([rel_bias[bucket], jnp.full((1, N_HEADS), NEG_INF, F32)], axis=0).T
    far_const = by_dist[:, MAX_DISTANCE:MAX_DISTANCE + 1]

    tq = SEQ_TILE
    u = np.arange(2 * n_pad) - n_pad
    dist = np.arange(tq)[None, :] - CMP_STRIDE * u[:, None] - (CMP_BLOCK - 1)
    varying = np.nonzero((dist.max(axis=1) >= 0) & (dist.min(axis=1) < MAX_DISTANCE))[0]
    lo, hi = int(varying[0]), int(varying[-1]) + 1
    strip = by_dist[:, _dist_index(dist[lo:hi], dist[lo:hi] >= 0)]
    tb = jnp.concatenate([
        jnp.broadcast_to(far_const[:, :, None], (N_HEADS, lo, tq)),
        strip,
        jnp.full((N_HEADS, 2 * n_pad - hi, tq), NEG_INF, F32)], axis=1)

    shifted = jnp.concatenate([by_dist[:, :-1] - far_const, by_dist[:, -1:]], axis=1)
    n = SEQ_TILE
    delta = _key_query_delta(n)
    prev = _toeplitz(shifted[:, _dist_index(delta + n, delta + n >= 0)], n)
    diag = _toeplitz(shifted[:, _dist_index(delta, delta >= 0)], n)
    tab = jnp.stack([prev, diag], axis=1)
    no_prev = jnp.stack([jnp.full_like(prev, NEG_INF), diag], axis=1)
    return LOG2E * tb, LOG2E * jnp.stack([no_prev, tab])


def _static_tiles(seq):
    n = SEQ_TILE
    key = np.arange(n)[:, None]
    query = np.arange(n)[None, :]
    tri = np.where(key <= query, 0.0, NEG_INF).astype(np.float32)
    far2 = np.where(query - key + 2 * n < WINDOW, 0.0, NEG_INF).astype(np.float32)
    assert WINDOW == 2 * n
    far2 = np.stack([np.full_like(far2, NEG_INF)] * 2 + [far2])
    n_sel = seq // SEL_BLOCK
    pos = np.arange(seq).reshape(seq // n, 1, n)
    onehot = (pos // SEL_BLOCK == np.arange(n_sel)[None, :, None]).astype(np.float32)
    return jnp.asarray(tri), jnp.asarray(far2), jnp.asarray(onehot, BF16)


def _overlap_t(n_pad, n_sel):
    n_cmp = n_pad - 1
    cmp_start = np.arange(n_pad) * CMP_STRIDE
    sel_start = np.arange(n_sel) * SEL_BLOCK
    ov = ((cmp_start[None, :] < sel_start[:, None] + SEL_BLOCK)
          & (cmp_start[None, :] + CMP_BLOCK > sel_start[:, None])
          & (np.arange(n_pad)[None, :] < n_cmp))
    return jnp.asarray(ov.astype(np.float32), BF16)


def _nsa_mixer(x, g, w_in, cmp_pos, cmp_w1, cmp_w2, rel_bias, batch, seq):
    n, d = x.shape
    G, R, dh = NSA_KV_HEADS, NSA_GROUP, HEAD_DIM
    hd, kvd = N_HEADS * dh, G * dh
    cols = [hd + i * kvd for i in range(7)]
    wq, wkc, wvc, wks, wvs, wkw, wvw, wgl = jnp.split(w_in, cols, axis=1)
    n_gate = wgl.shape[1]
    wt = jnp.concatenate([wq * Q_PRESCALE, wks, wvs, wkw, wvw], axis=1).T.astype(BF16)
    w2 = jnp.concatenate([wkc, wvc, wgl, jnp.zeros((d, -n_gate % 128), w_in.dtype)], axis=1).astype(BF16)
    qkv_t, f = _norm_proj(x, g, wt, w2, batch, seq)

    gates_t = f[:, 2 * kvd:2 * kvd + n_gate].reshape(batch, seq, G, R * 3).transpose(0, 2, 3, 1)
    gates_t = jnp.pad(gates_t, ((0, 0), (0, 0), (0, FEAT_ROWS - R * 3), (0, 0)))

    nh = seq // CMP_STRIDE
    cmp = _compress(f, cmp_pos.reshape(2, 2, CMP_STRIDE * dh), cmp_w1.astype(BF16), cmp_w2.astype(BF16),
                    batch, seq)
    kcmp = cmp[:, 0]
    vcmp_t = cmp[:, 1].transpose(0, 1, 3, 2)

    n_sel = seq // SEL_BLOCK
    tb, tab = _bias_tables(rel_bias, nh)
    _, far2, onehot_t = _static_tiles(seq)
    oc_t, sel_t = _cmp_sel(qkv_t, kcmp, vcmp_t, tb, _overlap_t(nh, n_sel))
    o = _sel_win(qkv_t, sel_t, onehot_t, tab, far2, gates_t, oc_t)
    return o.reshape(n, hd)


def _shared_kv(x, g, w_kvf, b_f, batch, seq):
    hd = N_HEADS * HEAD_DIM
    wt = w_kvf[:, :2 * hd].T.astype(BF16)
    wf = w_kvf[:, 2 * hd:]
    w2 = jnp.concatenate([wf, jnp.zeros((wf.shape[0], -wf.shape[1] % 128), wf.dtype)], axis=1).astype(BF16)
    kv_t, f = _norm_proj(x, g, wt, w2, batch, seq)
    fl_t = f[:, :N_HEADS].reshape(batch, seq, N_HEADS).transpose(0, 2, 1)
    pieces = _forget_cumsum(fl_t, b_f.astype(F32)).transpose(0, 2, 1, 3)
    pieces = jnp.pad(pieces, ((0, 0), (0, 0), (0, FEAT_ROWS - N_DECAY_PIECES), (0, 0)))
    kfeat_t = pieces.reshape(batch, N_HEADS, FEAT_ROWS, seq // SEQ_TILE, SEQ_TILE).transpose(0, 3, 1, 2, 4)
    return kv_t, kfeat_t


def _fox_mixer(x, g, w_q, kv, batch, seq):
    n, _ = x.shape
    hd = N_HEADS * HEAD_DIM
    (q_t,) = _norm_proj(x, g, (w_q * Q_PRESCALE).T.astype(BF16), None, batch, seq)
    tri, _, _ = _static_tiles(seq)
    o = _fox_attention(q_t, *kv, tri)
    return o.reshape(n, hd)


def kernel(x, norm_g, ffn_w_in, ffn_w_out, nsa_w_in, nsa_cmp_pos, nsa_cmp_w1, nsa_cmp_w2, nsa_w_out, rel_bias,
           kv_norm_g, fox_w_kvf, fox_b_f, fox_w_q, fox_w_out, final_g):
    batch, seq, d = x.shape
    depth = norm_g.shape[0]
    n_a = nsa_w_in.shape[0]
    xf = x.reshape(batch * seq, d)
    w_in = ffn_w_in.astype(BF16)
    w_out = ffn_w_out.astype(BF16)
    kv = None
    for l in range(depth):
        if l == n_a:
            kv = _shared_kv(xf, kv_norm_g, fox_w_kvf, fox_b_f, batch, seq)
        xf = _ffn(xf, norm_g[l, 0], w_in, w_out, (l, 0))
        if l < n_a:
            o = _nsa_mixer(xf, norm_g[l, 1], nsa_w_in[l], nsa_cmp_pos[l], nsa_cmp_w1[l], nsa_cmp_w2[l],
                           rel_bias, batch, seq)
            w_mix = nsa_w_out[l]
        else:
            o = _fox_mixer(xf, norm_g[l, 1], fox_w_q[l - n_a], kv, batch, seq)
            w_mix = fox_w_out[l - n_a]
        xf = _ffn(xf, norm_g[l, 2], w_in, w_out, (l, 1), mixer=(o, w_mix.astype(BF16)),
                  final_g=final_g if l == depth - 1 else None)
    return xf.reshape(batch, seq, d)
```

```python
import functools
import math

import numpy as np
import jax
import jax.numpy as jnp
from jax import lax
from jax.experimental import pallas as pl
from jax.experimental.pallas import tpu as pltpu

N_HEADS = 16
HEAD_DIM = 64
NSA_KV_HEADS = 4
NSA_GROUP = N_HEADS // NSA_KV_HEADS
NSA_GROUPS_PER_STEP = 2
CMP_BLOCK = 32
CMP_STRIDE = 16
SEL_BLOCK = 64
SEL_BLOCK_SHIFT = SEL_BLOCK.bit_length() - 1
assert 1 << SEL_BLOCK_SHIFT == SEL_BLOCK
SEL_TOPK = 16
WINDOW = 512
N_BUCKETS = 32
MAX_DISTANCE = 128
EPS = 1e-6
NEG_INF = -1e30
LOG2E = math.log2(math.e)
Q_PRESCALE = LOG2E * HEAD_DIM ** -0.5
SEL_FORCE = 1e4

F32 = jnp.float32
BF16 = jnp.bfloat16

VMEM_LIMIT_BYTES = 48 * 1024 * 1024

ROW_TILE = 512
FF_TILE = 256
SEQ_TILE = 256
FEAT_ROWS = 16
ACC_ROWS = HEAD_DIM + FEAT_ROWS
QK_LOOKAHEAD_TILES = 6
KEY_FEATS = 128

_NT = (((1,), (1,)), ((), ()))
_TN = (((0,), (0,)), ((), ()))


def _params(*sem):
    return pltpu.CompilerParams(dimension_semantics=sem, vmem_limit_bytes=VMEM_LIMIT_BYTES)


def _rms(x, g):
    return x * lax.rsqrt(jnp.mean(x * x, axis=-1, keepdims=True) + EPS) * g


def _silu(a):
    return a * jax.nn.sigmoid(a)


def _ffn_kernel(x_ref, g_ref, wi_ref, wo_ref, *rest, has_mixer, final_norm):
    d_ff = wo_ref.shape[2]
    rest = list(rest)
    mix_ref, wm_ref = (rest.pop(0), rest.pop(0)) if has_mixer else (None, None)
    fg_ref = rest.pop(0) if final_norm else None
    (o_ref,) = rest
    x = x_ref[...]
    if has_mixer:
        x = x + jnp.dot(mix_ref[...], wm_ref[...], preferred_element_type=F32)
    hn = _rms(x, g_ref[...]).astype(BF16)
    acc = None
    for c in range(0, d_ff, FF_TILE):
        a = jnp.dot(hn, wi_ref[0, 0, :, c:c + FF_TILE], preferred_element_type=F32)
        b = jnp.dot(hn, wi_ref[0, 0, :, d_ff + c:d_ff + c + FF_TILE], preferred_element_type=F32)
        part = jnp.dot((_silu(a) * b).astype(BF16), wo_ref[0, 0, c:c + FF_TILE, :], preferred_element_type=F32)
        acc = part if acc is None else acc + part
    y = x + 0.5 * acc
    if final_norm:
        y = _rms(y, fg_ref[...])
    o_ref[...] = y


def _resident(shape, index=None):
    index = (0,) * len(shape) if index is None else index
    return pl.BlockSpec(shape, lambda *_: index, pipeline_mode=pl.Buffered(1))


def _ffn(x, g, w_in, w_out, which, mixer=None, final_g=None):
    n, d = x.shape
    in_specs = [
        pl.BlockSpec((ROW_TILE, d), lambda i: (i, 0)),
        _resident((1, d)),
        _resident((1, 1) + w_in.shape[2:], which + (0, 0)),
        _resident((1, 1) + w_out.shape[2:], which + (0, 0)),
    ]
    args = [x, g.reshape(1, d), w_in, w_out]
    if mixer is not None:
        o, w_mix = mixer
        in_specs += [pl.BlockSpec((ROW_TILE, o.shape[1]), lambda i: (i, 0)), _resident(w_mix.shape)]
        args += [o, w_mix]
    if final_g is not None:
        in_specs.append(_resident((1, d)))
        args.append(final_g.reshape(1, d))
    return pl.pallas_call(
        functools.partial(_ffn_kernel, has_mixer=mixer is not None, final_norm=final_g is not None),
        grid=(n // ROW_TILE,),
        in_specs=in_specs,
        out_specs=pl.BlockSpec((ROW_TILE, d), lambda i: (i, 0)),
        out_shape=jax.ShapeDtypeStruct((n, d), F32),
        name="ffn",
        compiler_params=_params("parallel"),
    )(*args)


PROJ_CHUNK = 512


def _norm_proj_kernel(x_ref, g_ref, wt_ref, *rest, has_rows):
    if has_rows:
        w2_ref, ot_ref, o2_ref = rest
    else:
        (ot_ref,) = rest
    hn = _rms(x_ref[...], g_ref[...]).astype(BF16)
    channels = ot_ref.shape[2]
    for c in range(0, channels, PROJ_CHUNK):
        r = lax.dot_general(wt_ref[c:c + PROJ_CHUNK, :], hn, _NT, preferred_element_type=F32).astype(BF16)
        for j in range(ot_ref.shape[1]):
            ot_ref[0, j, c:c + PROJ_CHUNK, :] = r[:, j * SEQ_TILE:(j + 1) * SEQ_TILE]
    if has_rows:
        o2_ref[...] = jnp.dot(hn, w2_ref[...], preferred_element_type=F32)


def _norm_proj(x, g, wt, w2, batch, seq):
    n, d = x.shape
    channels = wt.shape[0]
    per_seq = seq // ROW_TILE
    sub = ROW_TILE // SEQ_TILE
    in_specs = [
        pl.BlockSpec((ROW_TILE, d), lambda i: (i, 0)),
        pl.BlockSpec((1, d), lambda i: (0, 0)),
        pl.BlockSpec((channels, d), lambda i: (0, 0)),
    ]
    out_specs = [pl.BlockSpec((1, sub, channels, SEQ_TILE), lambda i: (i // per_seq, i % per_seq, 0, 0))]
    out_shape = [jax.ShapeDtypeStruct((batch, seq // SEQ_TILE, channels, SEQ_TILE), BF16)]
    args = [x, g.reshape(1, d), wt]
    if w2 is not None:
        n2 = w2.shape[1]
        in_specs.append(pl.BlockSpec((d, n2), lambda i: (0, 0)))
        out_specs.append(pl.BlockSpec((ROW_TILE, n2), lambda i: (i, 0)))
        out_shape.append(jax.ShapeDtypeStruct((n, n2), F32))
        args.append(w2)
    return pl.pallas_call(
        functools.partial(_norm_proj_kernel, has_rows=w2 is not None),
        grid=(n // ROW_TILE,),
        in_specs=in_specs,
        out_specs=out_specs,
        out_shape=out_shape,
        name="norm_proj",
        compiler_params=_params("parallel"),
    )(*args)


def _compress_kernel(*refs):
    *f_refs, pos_ref, w1_ref, w2_ref, o_ref = refs
    nh = f_refs[0].shape[0] // CMP_STRIDE
    dh = HEAD_DIM
    per_chunk = f_refs[0].shape[1] // dh
    for chunk, f_ref in enumerate(f_refs):
        rows = [f_ref[pl.ds(j, nh, stride=CMP_STRIDE), :] for j in range(CMP_STRIDE)]
        for sub in range(per_chunk):
            kv, g = divmod(chunk * per_chunk + sub, NSA_KV_HEADS)
            x = jnp.concatenate([r[:, sub * dh:(sub + 1) * dh] for r in rows], axis=1)
            pos = pos_ref[kv]
            half = x.shape[1]
            top = jnp.dot((x + pos[0:1]).astype(BF16), w1_ref[kv, :half, :], preferred_element_type=F32)
            bot = jnp.dot((x + pos[1:2]).astype(BF16), w1_ref[kv, half:, :], preferred_element_type=F32)
            pre = top + pltpu.roll(bot, nh - 1, axis=0)
            hid = _silu(pre).astype(BF16)
            o_ref[0, kv, g] = jnp.dot(hid, w2_ref[kv], preferred_element_type=F32).astype(BF16)


def _compress(f, pos2, w1, w2, batch, seq):
    g, dh = NSA_KV_HEADS, HEAD_DIM
    nh = seq // CMP_STRIDE
    lanes = 128
    n_chunks = 2 * g * dh // lanes
    return pl.pallas_call(
        _compress_kernel,
        grid=(batch,),
        in_specs=[pl.BlockSpec((seq, lanes), functools.partial(lambda c, i: (i, c), c)) for c in range(n_chunks)] + [
            _resident(pos2.shape),
            _resident(w1.shape),
            _resident(w2.shape),
        ],
        out_specs=pl.BlockSpec((1, 2, g, nh, dh), lambda i: (i, 0, 0, 0, 0)),
        out_shape=jax.ShapeDtypeStruct((batch, 2, g, nh, dh), BF16),
        name="compress",
        compiler_params=_params("parallel"),
    )(*([f] * n_chunks), pos2, w1, w2)


def _split3(x):
    p1 = x.astype(BF16)
    r1 = x - p1.astype(F32)
    p2 = r1.astype(BF16)
    p3 = (r1 - p2.astype(F32)).astype(BF16)
    return p1, p2, p3


def _cmp_block_scores(gg, qi, q_ref, kc_ref, vct_ref, tb_ref, ovt_ref, oct_ref):
    tq = SEQ_TILE
    nrep = NSA_GROUP
    dh = HEAD_DIM
    n_cmp_pad = kc_ref.shape[2]
    qt = jnp.concatenate([q_ref[0, 0, (gg * nrep + r) * dh:(gg * nrep + r + 1) * dh, :] for r in range(nrep)],
                         axis=1)
    s = jnp.dot(kc_ref[0, gg], qt, preferred_element_type=F32)
    off = pl.multiple_of(n_cmp_pad - (tq // CMP_STRIDE) * qi, 8)
    bias = jnp.concatenate([tb_ref[gg * nrep + r, pl.ds(off, n_cmp_pad), :] for r in range(nrep)], axis=1)
    s = s + bias
    m = jnp.max(s, axis=0, keepdims=True)
    e = jnp.exp2(s - m)
    assert tq & (tq - 1) == 0
    t = qi * tq + (lax.broadcasted_iota(jnp.int32, (1, nrep * tq), 1) & (tq - 1))
    p = e * jnp.where(t >= CMP_BLOCK - 1, 1.0 / jnp.sum(e, axis=0, keepdims=True), 0.0)

    oct = jnp.dot(vct_ref[0, gg], p.astype(BF16), preferred_element_type=F32)
    for r in range(nrep):
        oct_ref[0, gg, r] = oct[:, r * tq:(r + 1) * tq]

    psum = p[:, 0:tq]
    for r in range(1, nrep):
        psum = psum + p[:, r * tq:(r + 1) * tq]
    ovt = ovt_ref[...]
    imp = sum(jnp.dot(ovt, part, preferred_element_type=F32) for part in _split3(psum))

    n_sel = imp.shape[0]
    blk = lax.broadcasted_iota(jnp.int32, (n_sel, tq), 0)
    cur = jnp.right_shift(qi * tq + lax.broadcasted_iota(jnp.int32, (n_sel, tq), 1), SEL_BLOCK_SHIFT)
    forced = (blk == 0) | (blk == cur) | (blk == cur - 1)
    return jnp.where(blk > cur, -SEL_FORCE, imp + jnp.where(forced, SEL_FORCE, 0.0))


def _cmp_sel_kernel(q_ref, kc_ref, vct_ref, tb_ref, ovt_ref, oct_ref, sel_ref, cnt_ref):
    qi = pl.program_id(2)
    tq = SEQ_TILE
    gps = kc_ref.shape[1]
    scores = [_cmp_block_scores(gg, qi, q_ref, kc_ref, vct_ref, tb_ref, ovt_ref, oct_ref) for gg in range(gps)]
    n_sel = scores[0].shape[0]

    rows = 8
    n_groups = n_sel // rows
    blk_in_group = lax.broadcasted_iota(jnp.int32, (rows, tq), 0)
    last_group = jnp.right_shift(qi * tq + tq - 1, SEL_BLOCK_SHIFT) // rows
    cnt_ref[...] = jnp.zeros_like(cnt_ref)

    def count(score, k, g):
        sg = score[g * rows:(g + 1) * rows]
        total = None
        for sp in range(k * rows, (k + 1) * rows):
            row = jnp.broadcast_to(score[sp:sp + 1, :], (rows, tq))
            if g > k:
                one = jnp.where(row >= sg, 1.0, 0.0)
            elif g < k:
                one = jnp.where(row > sg, 1.0, 0.0)
            else:
                tie = jnp.where(blk_in_group + g * rows > sp, 1.0, 0.0)
                one = jnp.where(row > sg, 1.0, jnp.where(row == sg, tie, 0.0))
            total = one if total is None else total + one
        return total

    for level in range(n_groups):
        @pl.when(level <= last_group)
        def _():
            for gg, score in enumerate(scores):
                for g in range(level + 1):
                    cnt_ref[gg, g * rows:(g + 1) * rows, :] += count(score, level, g)
                extra = None
                for k in range(level):
                    c = count(score, k, level)
                    extra = c if extra is None else extra + c
                if extra is not None:
                    cnt_ref[gg, level * rows:(level + 1) * rows, :] += extra

    sel_ref[0] = jnp.where(cnt_ref[...] < min(SEL_TOPK, n_sel), 1.0, 0.0).astype(BF16)


def _cmp_sel(qkv_t, kcmp, vcmp_t, tb, ovt):
    b, nt, _, _ = qkv_t.shape
    g, r, dh = NSA_KV_HEADS, NSA_GROUP, HEAD_DIM
    t = nt * SEQ_TILE
    n_pad = kcmp.shape[2]
    n_sel = ovt.shape[0]
    tq = SEQ_TILE
    gps = NSA_GROUPS_PER_STEP
    return pl.pallas_call(
        _cmp_sel_kernel,
        grid=(b, g // gps, t // tq),
        in_specs=[
            pl.BlockSpec((1, 1, gps * r * dh, tq), lambda i, j, k: (i, k, j, 0)),
            pl.BlockSpec((1, gps, n_pad, dh), lambda i, j, k: (i, j, 0, 0)),
            pl.BlockSpec((1, gps, dh, n_pad), lambda i, j, k: (i, j, 0, 0)),
            pl.BlockSpec((gps * r, 2 * n_pad, tq), lambda i, j, k: (j, 0, 0)),
            pl.BlockSpec((n_sel, n_pad), lambda i, j, k: (0, 0)),
        ],
        out_specs=[
            pl.BlockSpec((1, gps, r, dh, tq), lambda i, j, k: (i, j, 0, 0, k)),
            pl.BlockSpec((1, gps, n_sel, tq), lambda i, j, k: (i, j, 0, k)),
        ],
        out_shape=[
            jax.ShapeDtypeStruct((b, g, r, dh, t), F32),
            jax.ShapeDtypeStruct((b, g, n_sel, t), BF16),
        ],
        scratch_shapes=[pltpu.VMEM((gps, n_sel, tq), F32)],
        name="cmp_sel",
        compiler_params=_params("parallel", "parallel", "parallel"),
    )(qkv_t, kcmp, vcmp_t, tb, ovt)


def _ones_rows(width):
    return jnp.ones((FEAT_ROWS, width), BF16)


def _with_ones(v_t):
    return jnp.concatenate([v_t, _ones_rows(v_t.shape[1])], axis=0)


def _online_update(m_ref, acc_ref, slot, tiles, first):
    mt = None
    for s, _ in tiles:
        cm = jnp.max(s, axis=0, keepdims=True)
        mt = cm if mt is None else jnp.maximum(mt, cm)
    m_old = None if first else m_ref[slot]
    m_new = mt if first else jnp.maximum(m_old, mt)
    pv = None
    for s, va in tiles:
        d = jnp.dot(va, jnp.exp2(s - m_new).astype(BF16), preferred_element_type=F32)
        pv = d if pv is None else pv + d
    acc_ref[slot] = pv if first else jnp.exp2(m_old - m_new) * acc_ref[slot] + pv
    m_ref[slot] = m_new


def _pipelined_updates(m_ref, acc_ref, jobs, tiles_per_job=1):
    lookahead = max(1, QK_LOOKAHEAD_TILES // tiles_per_job)
    pending = [job[1]() for job in jobs[:lookahead]]
    for i, (slot, _, values_fn, first) in enumerate(jobs):
        current = pending.pop(0)
        if i + lookahead < len(jobs):
            pending.append(jobs[i + lookahead][1]())
        _online_update(m_ref, acc_ref, slot, list(zip(current, values_fn())), first)


def _for_far_tiles(n_far, run):
    n_quads = n_far // 4

    def quad(i, carry):
        run([[4 * i, 4 * i + 1], [4 * i + 2, 4 * i + 3]])
        return carry

    lax.fori_loop(0, n_quads, quad, 0)

    @pl.when((n_far & 2) != 0)
    def _():
        run([[4 * n_quads, 4 * n_quads + 1]])

    @pl.when((n_far & 1) != 0)
    def _():
        run([[n_far - 1]])


def _normalised(acc_ref, slot):
    acc = acc_ref[slot]
    return acc[:HEAD_DIM] / acc[HEAD_DIM:HEAD_DIM + 1]


def _sel_win_kernel(q_ref, ks_ref, vs_ref, kw_ref, vw_ref, sel_ref, oh_ref, tab_ref, far2_ref, gate_ref, oc_ref,
                    o_ref, ksrm_ref, kwrm_ref, m_ref, acc_ref):
    qi = pl.program_id(2)
    tq = SEQ_TILE
    nrep = NSA_GROUP
    gps = NSA_GROUPS_PER_STEP
    dh = HEAD_DIM
    pad = jnp.zeros((KEY_FEATS - dh, tq), BF16)

    @pl.when(qi == 0)
    def _():
        def to_rows(c, carry):
            for gg in range(gps):
                ksa = jnp.concatenate([ks_ref[0, c, gg * dh:(gg + 1) * dh, :], oh_ref[c]], axis=0)
                ksrm_ref[gg, c] = ksa.astype(F32).T.astype(BF16)
                kwa = jnp.concatenate([kw_ref[0, c, gg * dh:(gg + 1) * dh, :], pad], axis=0)
                kwrm_ref[gg, c] = kwa.astype(F32).T.astype(BF16)
            return carry

        lax.fori_loop(0, ks_ref.shape[1], to_rows, 0)

    q_sel, q_win = [], []
    for gg in range(gps):
        mq = ((sel_ref[0, gg].astype(F32) - 1.0) * (-NEG_INF)).astype(BF16)
        for r in range(nrep):
            q = q_ref[0, 0, (gg * nrep + r) * dh:(gg * nrep + r + 1) * dh, :]
            q_sel.append(jnp.concatenate([q, mq], axis=0))
            q_win.append(jnp.concatenate([q, pad], axis=0))

    def jobs(branch, cs, bias=None, first=False):
        k_ref, v_ref, qa = (ksrm_ref, vs_ref, q_sel) if branch == 0 else (kwrm_ref, vw_ref, q_win)

        def logits(gg, r):
            head = gg * nrep + r
            tiles = [jnp.dot(k_ref[gg, c], qa[head], preferred_element_type=F32) for c in cs]
            return tiles if bias is None else [s + bias(head) for s in tiles]

        def values(gg):
            return [_with_ones(v_ref[0, c, gg * dh:(gg + 1) * dh, :]) for c in cs]

        return [((gg * 2 + branch) * nrep + r, functools.partial(logits, gg, r), functools.partial(values, gg), first)
                for gg in range(gps) for r in range(nrep)]

    diag = lambda head: tab_ref[0, head, 1]
    prev = lambda head: tab_ref[0, head, 0]
    edge = lambda head: far2_ref[0]
    c_prev = jnp.maximum(qi - 1, 0)
    c_edge = jnp.maximum(qi - 2, 0)
    _pipelined_updates(m_ref, acc_ref,
                       jobs(0, [qi], diag, first=True) + jobs(1, [qi], diag, first=True)
                       + jobs(0, [c_prev], prev) + jobs(1, [c_prev], prev) + jobs(1, [c_edge], edge))

    _for_far_tiles(jnp.maximum(qi - 1, 0), lambda groups: _pipelined_updates(
        m_ref, acc_ref, [job for cs in groups for job in jobs(0, cs)], tiles_per_job=len(groups[0])))

    outs = []
    for gg in range(gps):
        gates = jax.nn.sigmoid(gate_ref[0, gg])
        for r in range(nrep):
            outs.append(gates[3 * r:3 * r + 1] * oc_ref[0, gg, r]
                        + gates[3 * r + 1:3 * r + 2] * _normalised(acc_ref, (gg * 2) * nrep + r)
                        + gates[3 * r + 2:3 * r + 3] * _normalised(acc_ref, (gg * 2 + 1) * nrep + r))
    o_ref[0] = jnp.concatenate(outs, axis=0).T.astype(BF16)


def _sel_win(qkv_t, sel_t, onehot_t, tab, far2, gates_t, oc_t):
    b, nt, channels, tq = qkv_t.shape
    g, r, dh = NSA_KV_HEADS, NSA_GROUP, HEAD_DIM
    gps = NSA_GROUPS_PER_STEP
    hd = g * r * dh
    n_sel = sel_t.shape[2]
    kv_block = lambda which: pl.BlockSpec(
        (1, nt, gps * dh, tq), lambda i, j, k: (i, 0, (hd + which * g * dh) // (gps * dh) + j, 0))
    return pl.pallas_call(
        _sel_win_kernel,
        grid=(b, g // gps, nt),
        in_specs=[
            pl.BlockSpec((1, 1, gps * r * dh, tq), lambda i, j, k: (i, k, j, 0)),
            kv_block(0), kv_block(1), kv_block(2), kv_block(3),
            pl.BlockSpec((1, gps, n_sel, tq), lambda i, j, k: (i, j, 0, k)),
            pl.BlockSpec((nt, n_sel, tq), lambda i, j, k: (0, 0, 0)),
            pl.BlockSpec((1, gps * r, 2, tq, tq), lambda i, j, k: (jnp.minimum(k, tab.shape[0] - 1), j, 0, 0, 0)),
            pl.BlockSpec((1, tq, tq), lambda i, j, k: (jnp.minimum(k, far2.shape[0] - 1), 0, 0)),
            pl.BlockSpec((1, gps, FEAT_ROWS, tq), lambda i, j, k: (i, j, 0, k)),
            pl.BlockSpec((1, gps, r, dh, tq), lambda i, j, k: (i, j, 0, 0, k)),
        ],
        out_specs=pl.BlockSpec((1, tq, gps * r * dh), lambda i, j, k: (i, k, j)),
        out_shape=jax.ShapeDtypeStruct((b, nt * tq, hd), BF16),
        scratch_shapes=[pltpu.VMEM((gps, nt, tq, KEY_FEATS), BF16), pltpu.VMEM((gps, nt, tq, KEY_FEATS), BF16),
                        pltpu.VMEM((gps * 2 * r, 1, tq), F32), pltpu.VMEM((gps * 2 * r, ACC_ROWS, tq), F32)],
        name="sel_win",
        compiler_params=_params("parallel", "parallel", "arbitrary"),
    )(qkv_t, qkv_t, qkv_t, qkv_t, qkv_t, sel_t, onehot_t, tab, far2, gates_t, oc_t)


N_DECAY_PIECES = 3


def _forget_cumsum_kernel(fl_ref, bf_ref, piece_ref):
    x = fl_ref[0] + bf_ref[...]
    y = jnp.minimum(x, 0.0) - jnp.log1p(jnp.exp(-jnp.abs(x)))
    t = y.shape[1]
    lane = lax.broadcasted_iota(jnp.int32, y.shape, 1)
    k = 1
    while k < t:
        y = y + jnp.where(lane >= k, pltpu.roll(y, k, axis=1), 0.0)
        k *= 2
    for i, piece in enumerate(_split3(-LOG2E * y)):
        piece_ref[0, i] = piece


def _forget_cumsum(fl_t, b_f):
    b, h, t = fl_t.shape
    return pl.pallas_call(
        _forget_cumsum_kernel,
        grid=(b,),
        in_specs=[pl.BlockSpec((1, h, t), lambda i: (i, 0, 0)), pl.BlockSpec((h, 1), lambda i: (0, 0))],
        out_specs=pl.BlockSpec((1, N_DECAY_PIECES, h, t), lambda i: (i, 0, 0, 0)),
        out_shape=jax.ShapeDtypeStruct((b, N_DECAY_PIECES, h, t), BF16),
        name="forget_cumsum",
        compiler_params=_params("parallel"),
    )(fl_t, b_f.reshape(h, 1))


FOX_HEADS_PER_STEP = 8
FOX_Q_TILES_PER_STEP = 2


def _fox_kernel(q_ref, k_ref, v_ref, kf_ref, tri_ref, o_ref, krm_ref, m_ref, acc_ref):
    step_id = pl.program_id(2)
    tq = SEQ_TILE
    hp = FOX_HEADS_PER_STEP
    nq = FOX_Q_TILES_PER_STEP
    n_pieces = N_DECAY_PIECES
    pad = jnp.zeros((KEY_FEATS - HEAD_DIM - FEAT_ROWS, tq), BF16)

    @pl.when(step_id == 0)
    def _():
        def to_rows(c, carry):
            for h in range(hp):
                ka = jnp.concatenate([k_ref[0, c, h * HEAD_DIM:(h + 1) * HEAD_DIM, :], kf_ref[0, c, h], pad], axis=0)
                krm_ref[h, c] = ka.astype(F32).T.astype(BF16)
            return carry

        lax.fori_loop(0, k_ref.shape[1], to_rows, 0)

    q_feat = jnp.where(lax.broadcasted_iota(jnp.int32, (FEAT_ROWS, tq), 0) < n_pieces, 1.0, 0.0).astype(BF16)
    qa = [[jnp.concatenate([q_ref[0, sub, h * HEAD_DIM:(h + 1) * HEAD_DIM, :], q_feat, pad], axis=0)
           for h in range(hp)] for sub in range(nq)]

    def jobs(sub, cs, first=False, causal=False):
        def logits(h):
            tiles = [jnp.dot(krm_ref[h, c], qa[sub][h], preferred_element_type=F32) for c in cs]
            return [s + tri_ref[...] for s in tiles] if causal else tiles

        def values(h):
            return [_with_ones(v_ref[0, c, h * HEAD_DIM:(h + 1) * HEAD_DIM, :]) for c in cs]

        return [(sub * hp + h, functools.partial(logits, h), functools.partial(values, h), first)
                for h in range(hp)]

    base = nq * step_id
    near = [job for sub in range(nq) for job in jobs(sub, [base + sub], first=True, causal=True)]
    near += [job for sub in range(nq) for c in range(sub) for job in jobs(sub, [base + c])]
    _pipelined_updates(m_ref, acc_ref, near)
    _for_far_tiles(base, lambda groups: _pipelined_updates(
        m_ref, acc_ref, [job for cs in groups for sub in range(nq) for job in jobs(sub, cs)],
        tiles_per_job=len(groups[0])))
    for sub in range(nq):
        o_t = jnp.concatenate([_normalised(acc_ref, sub * hp + h) for h in range(hp)], axis=0)
        o_ref[0, sub * tq:(sub + 1) * tq, :] = o_t.T.astype(BF16)


def _fox_attention(q_t, kv_t, kfeat_t, tri):
    b, nt, hd, tq = q_t.shape
    hp, dh, nq = FOX_HEADS_PER_STEP, HEAD_DIM, FOX_Q_TILES_PER_STEP
    n_groups = hd // (hp * dh)
    return pl.pallas_call(
        _fox_kernel,
        grid=(b, n_groups, nt // nq),
        in_specs=[
            pl.BlockSpec((1, nq, hp * dh, tq), lambda i, j, k: (i, k, j, 0)),
            pl.BlockSpec((1, nt, hp * dh, tq), lambda i, j, k: (i, 0, j, 0)),
            pl.BlockSpec((1, nt, hp * dh, tq), lambda i, j, k: (i, 0, n_groups + j, 0)),
            pl.BlockSpec((1, nt, hp, FEAT_ROWS, tq), lambda i, j, k: (i, 0, j, 0, 0)),
            pl.BlockSpec((tq, tq), lambda i, j, k: (0, 0)),
        ],
        out_specs=pl.BlockSpec((1, nq * tq, hp * dh), lambda i, j, k: (i, k, j)),
        out_shape=jax.ShapeDtypeStruct((b, nt * tq, hd), BF16),
        scratch_shapes=[pltpu.VMEM((hp, nt, tq, KEY_FEATS), BF16),
                        pltpu.VMEM((nq * hp, 1, tq), F32), pltpu.VMEM((nq * hp, ACC_ROWS, tq), F32)],
        name="fox_attention",
        compiler_params=_params("parallel", "parallel", "arbitrary"),
    )(q_t, kv_t, kv_t, kfeat_t, tri)


def _t5_bucket_np(dist):
    dist = np.maximum(dist, 0)
    max_exact = N_BUCKETS // 2
    ratio = np.maximum(dist, 1).astype(np.float32) / np.float32(max_exact)
    scaled = np.log(ratio) / np.float32(math.log(MAX_DISTANCE / max_exact))
    large = np.minimum(max_exact + (scaled * np.float32(N_BUCKETS - max_exact)).astype(np.int32), N_BUCKETS - 1)
    return np.where(dist < max_exact, dist, large).astype(np.int32)


_MASKED = MAX_DISTANCE + 1


def _dist_index(dist, valid):
    return np.where(valid, np.minimum(dist, MAX_DISTANCE), _MASKED).astype(np.int32)


def _toeplitz(f_ext, n):
    period = 2 * n
    flat = jnp.tile(f_ext, n)[..., :n * (period - 1)]
    return flat.reshape(f_ext.shape[:-1] + (n, period - 1))[..., :n]


def _key_query_delta(n):
    p = np.arange(2 * n)
    return np.where(p < n, p, p - 2 * n)


def _bias_tables(rel_bias, n_pad):
    bucket = _t5_bucket_np(np.arange(MAX_DISTANCE + 1))
    by_dist = jnp.concatenate([rel_bias[bucket], jnp.full((1, N_HEADS), NEG_INF, F32)], axis=0).T
    far_const = by_dist[:, MAX_DISTANCE:MAX_DISTANCE + 1]

    tq = SEQ_TILE
    u = np.arange(2 * n_pad) - n_pad
    dist = np.arange(tq)[None, :] - CMP_STRIDE * u[:, None] - (CMP_BLOCK - 1)
    varying = np.nonzero((dist.max(axis=1) >= 0) & (dist.min(axis=1) < MAX_DISTANCE))[0]
    lo, hi = int(varying[0]), int(varying[-1]) + 1
    strip = by_dist[:, _dist_index(dist[lo:hi], dist[lo:hi] >= 0)]
    tb = jnp.concatenate([
        jnp.broadcast_to(far_const[:, :, None], (N_HEADS, lo, tq)),
        strip,
        jnp.full((N_HEADS, 2 * n_pad - hi, tq), NEG_INF, F32)], axis=1)

    shifted = jnp.concatenate([by_dist[:, :-1] - far_const, by_dist[:, -1:]], axis=1)
    n = SEQ_TILE
    delta = _key_query_delta(n)
    prev = _toeplitz(shifted[:, _dist_index(delta + n, delta + n >= 0)], n)
    diag = _toeplitz(shifted[:, _dist_index(delta, delta >= 0)], n)
    tab = jnp.stack([prev, diag], axis=1)
    no_prev = jnp.stack([jnp.full_like(prev, NEG_INF), diag], axis=1)
    return LOG2E * tb, LOG2E * jnp.stack([no_prev, tab])


def _static_tiles(seq):
    n = SEQ_TILE
    key = np.arange(n)[:, None]
    query = np.arange(n)[None, :]
    tri = np.where(key <= query, 0.0, NEG_INF).astype(np.float32)
    far2 = np.where(query - key + 2 * n < WINDOW, 0.0, NEG_INF).astype(np.float32)
    assert WINDOW == 2 * n
    far2 = np.stack([np.full_like(far2, NEG_INF)] * 2 + [far2])
    n_sel = seq // SEL_BLOCK
    pos = np.arange(seq).reshape(seq // n, 1, n)
    onehot = (pos // SEL_BLOCK == np.arange(n_sel)[None, :, None]).astype(np.float32)
    return jnp.asarray(tri), jnp.asarray(far2), jnp.asarray(onehot, BF16)


def _overlap_t(n_pad, n_sel):
    n_cmp = n_pad - 1
    cmp_start = np.arange(n_pad) * CMP_STRIDE
    sel_start = np.arange(n_sel) * SEL_BLOCK
    ov = ((cmp_start[None, :] < sel_start[:, None] + SEL_BLOCK)
          & (cmp_start[None, :] + CMP_BLOCK > sel_start[:, None])
          & (np.arange(n_pad)[None, :] < n_cmp))
    return jnp.asarray(ov.astype(np.float32), BF16)


def _nsa_mixer(x, g, w_in, cmp_pos, cmp_w1, cmp_w2, rel_bias, batch, seq):
    n, d = x.shape
    G, R, dh = NSA_KV_HEADS, NSA_GROUP, HEAD_DIM
    hd, kvd = N_HEADS * dh, G * dh
    cols = [hd + i * kvd for i in range(7)]
    wq, wkc, wvc, wks, wvs, wkw, wvw, wgl = jnp.split(w_in, cols, axis=1)
    n_gate = wgl.shape[1]
    wt = jnp.concatenate([wq * Q_PRESCALE, wks, wvs, wkw, wvw], axis=1).T.astype(BF16)
    w2 = jnp.concatenate([wkc, wvc, wgl, jnp.zeros((d, -n_gate % 128), w_in.dtype)], axis=1).astype(BF16)
    qkv_t, f = _norm_proj(x, g, wt, w2, batch, seq)

    gates_t = f[:, 2 * kvd:2 * kvd + n_gate].reshape(batch, seq, G, R * 3).transpose(0, 2, 3, 1)
    gates_t = jnp.pad(gates_t, ((0, 0), (0, 0), (0, FEAT_ROWS - R * 3), (0, 0)))

    nh = seq // CMP_STRIDE
    cmp = _compress(f, cmp_pos.reshape(2, 2, CMP_STRIDE * dh), cmp_w1.astype(BF16), cmp_w2.astype(BF16),
                    batch, seq)
    kcmp = cmp[:, 0]
    vcmp_t = cmp[:, 1].transpose(0, 1, 3, 2)

    n_sel = seq // SEL_BLOCK
    tb, tab = _bias_tables(rel_bias, nh)
    _, far2, onehot_t = _static_tiles(seq)
    oc_t, sel_t = _cmp_sel(qkv_t, kcmp, vcmp_t, tb, _overlap_t(nh, n_sel))
    o = _sel_win(qkv_t, sel_t, onehot_t, tab, far2, gates_t, oc_t)
    return o.reshape(n, hd)


def _shared_kv(x, g, w_kvf, b_f, batch, seq):
    hd = N_HEADS * HEAD_DIM
    wt = w_kvf[:, :2 * hd].T.astype(BF16)
    wf = w_kvf[:, 2 * hd:]
    w2 = jnp.concatenate([wf, jnp.zeros((wf.shape[0], -wf.shape[1] % 128), wf.dtype)], axis=1).astype(BF16)
    kv_t, f = _norm_proj(x, g, wt, w2, batch, seq)
    fl_t = f[:, :N_HEADS].reshape(batch, seq, N_HEADS).transpose(0, 2, 1)
    pieces = _forget_cumsum(fl_t, b_f.astype(F32)).transpose(0, 2, 1, 3)
    pieces = jnp.pad(pieces, ((0, 0), (0, 0), (0, FEAT_ROWS - N_DECAY_PIECES), (0, 0)))
    kfeat_t = pieces.reshape(batch, N_HEADS, FEAT_ROWS, seq // SEQ_TILE, SEQ_TILE).transpose(0, 3, 1, 2, 4)
    return kv_t, kfeat_t


def _fox_mixer(x, g, w_q, kv, batch, seq):
    n, _ = x.shape
    hd = N_HEADS * HEAD_DIM
    (q_t,) = _norm_proj(x, g, (w_q * Q_PRESCALE).T.astype(BF16), None, batch, seq)
    tri, _, _ = _static_tiles(seq)
    o = _fox_attention(q_t, *kv, tri)
    return o.reshape(n, hd)


def kernel(x, norm_g, ffn_w_in, ffn_w_out, nsa_w_in, nsa_cmp_pos, nsa_cmp_w1, nsa_cmp_w2, nsa_w_out, rel_bias,
           kv_norm_g, fox_w_kvf, fox_b_f, fox_w_q, fox_w_out, final_g):
    batch, seq, d = x.shape
    depth = norm_g.shape[0]
    n_a = nsa_w_in.shape[0]
    xf = x.reshape(batch * seq, d)
    w_in = ffn_w_in.astype(BF16)
    w_out = ffn_w_out.astype(BF16)
    kv = None
    for l in range(depth):
        if l == n_a:
            kv = _shared_kv(xf, kv_norm_g, fox_w_kvf, fox_b_f, batch, seq)
        xf = _ffn(xf, norm_g[l, 0], w_in, w_out, (l, 0))
        if l < n_a:
            o = _nsa_mixer(xf, norm_g[l, 1], nsa_w_in[l], nsa_cmp_pos[l], nsa_cmp_w1[l], nsa_cmp_w2[l],
                           rel_bias, batch, seq)
            w_mix = nsa_w_out[l]
        else:
            o = _fox_mixer(xf, norm_g[l, 1], fox_w_q[l - n_a], kv, batch, seq)
            w_mix = fox_w_out[l - n_a]
        xf = _ffn(xf, norm_g[l, 2], w_in, w_out, (l, 1), mixer=(o, w_mix.astype(BF16)),
                  final_g=final_g if l == depth - 1 else None)
    return xf.reshape(batch, seq, d)
```

```python
import functools
import math

import numpy as np
import jax
import jax.numpy as jnp
from jax import lax
from jax.experimental import pallas as pl
from jax.experimental.pallas import tpu as pltpu

N_HEADS = 16
HEAD_DIM = 64
NSA_KV_HEADS = 4
NSA_GROUP = N_HEADS // NSA_KV_HEADS
NSA_GROUPS_PER_STEP = 2
NSA_Q_TILES_PER_STEP = 2
CMP_BLOCK = 32
CMP_STRIDE = 16
SEL_BLOCK = 64
SEL_BLOCK_SHIFT = SEL_BLOCK.bit_length() - 1
assert 1 << SEL_BLOCK_SHIFT == SEL_BLOCK
SEL_TOPK = 16
WINDOW = 512
N_BUCKETS = 32
MAX_DISTANCE = 128
EPS = 1e-6
NEG_INF = -1e30
LOG2E = math.log2(math.e)
Q_PRESCALE = LOG2E * HEAD_DIM ** -0.5
SEL_FORCE = 1e4

F32 = jnp.float32
BF16 = jnp.bfloat16

VMEM_LIMIT_BYTES = 48 * 1024 * 1024

ROW_TILE = 512
FF_TILE = 256
SEQ_TILE = 256
FEAT_ROWS = 16
ACC_ROWS = HEAD_DIM + FEAT_ROWS
QK_LOOKAHEAD_TILES = 6
KEY_FEATS = 128

_NT = (((1,), (1,)), ((), ()))
_TN = (((0,), (0,)), ((), ()))


def _params(*sem):
    return pltpu.CompilerParams(dimension_semantics=sem, vmem_limit_bytes=VMEM_LIMIT_BYTES)


def _rms(x, g):
    return x * lax.rsqrt(jnp.mean(x * x, axis=-1, keepdims=True) + EPS) * g


def _silu(a):
    return a * jax.nn.sigmoid(a)


def _ffn_kernel(x_ref, g_ref, wi_ref, wo_ref, *rest, has_mixer, final_norm):
    d_ff = wo_ref.shape[2]
    rest = list(rest)
    mix_ref, wm_ref = (rest.pop(0), rest.pop(0)) if has_mixer else (None, None)
    fg_ref = rest.pop(0) if final_norm else None
    (o_ref,) = rest
    x = x_ref[...]
    if has_mixer:
        x = x + jnp.dot(mix_ref[...], wm_ref[...], preferred_element_type=F32)
    hn = _rms(x, g_ref[...]).astype(BF16)
    acc = None
    for c in range(0, d_ff, FF_TILE):
        a = jnp.dot(hn, wi_ref[0, 0, :, c:c + FF_TILE], preferred_element_type=F32)
        b = jnp.dot(hn, wi_ref[0, 0, :, d_ff + c:d_ff + c + FF_TILE], preferred_element_type=F32)
        part = jnp.dot((_silu(a) * b).astype(BF16), wo_ref[0, 0, c:c + FF_TILE, :], preferred_element_type=F32)
        acc = part if acc is None else acc + part
    y = x + 0.5 * acc
    if final_norm:
        y = _rms(y, fg_ref[...])
    o_ref[...] = y


def _resident(shape, index=None):
    index = (0,) * len(shape) if index is None else index
    return pl.BlockSpec(shape, lambda *_: index, pipeline_mode=pl.Buffered(1))


def _ffn(x, g, w_in, w_out, which, mixer=None, final_g=None):
    n, d = x.shape
    in_specs = [
        pl.BlockSpec((ROW_TILE, d), lambda i: (i, 0)),
        _resident((1, d)),
        _resident((1, 1) + w_in.shape[2:], which + (0, 0)),
        _resident((1, 1) + w_out.shape[2:], which + (0, 0)),
    ]
    args = [x, g.reshape(1, d), w_in, w_out]
    if mixer is not None:
        o, w_mix = mixer
        in_specs += [pl.BlockSpec((ROW_TILE, o.shape[1]), lambda i: (i, 0)), _resident(w_mix.shape)]
        args += [o, w_mix]
    if final_g is not None:
        in_specs.append(_resident((1, d)))
        args.append(final_g.reshape(1, d))
    return pl.pallas_call(
        functools.partial(_ffn_kernel, has_mixer=mixer is not None, final_norm=final_g is not None),
        grid=(n // ROW_TILE,),
        in_specs=in_specs,
        out_specs=pl.BlockSpec((ROW_TILE, d), lambda i: (i, 0)),
        out_shape=jax.ShapeDtypeStruct((n, d), F32),
        name="ffn",
        compiler_params=_params("parallel"),
    )(*args)


PROJ_CHUNK = 512


def _norm_proj_kernel(x_ref, g_ref, wt_ref, *rest, has_rows):
    if has_rows:
        w2_ref, ot_ref, o2_ref = rest
    else:
        (ot_ref,) = rest
    channels = ot_ref.shape[2]
    for j in range(ot_ref.shape[1]):
        rows = slice(j * SEQ_TILE, (j + 1) * SEQ_TILE)
        hn = _rms(x_ref[rows, :], g_ref[...]).astype(BF16)
        for c in range(0, channels, PROJ_CHUNK):
            ot_ref[0, j, c:c + PROJ_CHUNK, :] = lax.dot_general(
                wt_ref[c:c + PROJ_CHUNK, :], hn, _NT, preferred_element_type=F32).astype(BF16)
        if has_rows:
            o2_ref[rows, :] = jnp.dot(hn, w2_ref[...], preferred_element_type=F32)


def _norm_proj(x, g, wt, w2, batch, seq):
    n, d = x.shape
    channels = wt.shape[0]
    per_seq = seq // ROW_TILE
    sub = ROW_TILE // SEQ_TILE
    in_specs = [
        pl.BlockSpec((ROW_TILE, d), lambda i: (i, 0)),
        pl.BlockSpec((1, d), lambda i: (0, 0)),
        pl.BlockSpec((channels, d), lambda i: (0, 0)),
    ]
    out_specs = [pl.BlockSpec((1, sub, channels, SEQ_TILE), lambda i: (i // per_seq, i % per_seq, 0, 0))]
    out_shape = [jax.ShapeDtypeStruct((batch, seq // SEQ_TILE, channels, SEQ_TILE), BF16)]
    args = [x, g.reshape(1, d), wt]
    if w2 is not None:
        n2 = w2.shape[1]
        in_specs.append(pl.BlockSpec((d, n2), lambda i: (0, 0)))
        out_specs.append(pl.BlockSpec((ROW_TILE, n2), lambda i: (i, 0)))
        out_shape.append(jax.ShapeDtypeStruct((n, n2), F32))
        args.append(w2)
    return pl.pallas_call(
        functools.partial(_norm_proj_kernel, has_rows=w2 is not None),
        grid=(n // ROW_TILE,),
        in_specs=in_specs,
        out_specs=out_specs,
        out_shape=out_shape,
        name="norm_proj",
        compiler_params=_params("parallel"),
    )(*args)


def _compress_kernel(*refs):
    *f_refs, pos_ref, w1_ref, w2_ref, o_ref = refs
    nh = f_refs[0].shape[0] // CMP_STRIDE
    dh = HEAD_DIM
    per_chunk = f_refs[0].shape[1] // dh
    for chunk, f_ref in enumerate(f_refs):
        rows = [f_ref[pl.ds(j, nh, stride=CMP_STRIDE), :] for j in range(CMP_STRIDE)]
        for sub in range(per_chunk):
            kv, g = divmod(chunk * per_chunk + sub, NSA_KV_HEADS)
            x = jnp.concatenate([r[:, sub * dh:(sub + 1) * dh] for r in rows], axis=1)
            pos = pos_ref[kv]
            half = x.shape[1]
            top = jnp.dot((x + pos[0:1]).astype(BF16), w1_ref[kv, :half, :], preferred_element_type=F32)
            bot = jnp.dot((x + pos[1:2]).astype(BF16), w1_ref[kv, half:, :], preferred_element_type=F32)
            pre = top + pltpu.roll(bot, nh - 1, axis=0)
            hid = _silu(pre).astype(BF16)
            o_ref[0, kv, g] = jnp.dot(hid, w2_ref[kv], preferred_element_type=F32).astype(BF16)


def _compress(f, pos2, w1, w2, batch, seq):
    g, dh = NSA_KV_HEADS, HEAD_DIM
    nh = seq // CMP_STRIDE
    lanes = 128
    n_chunks = 2 * g * dh // lanes
    return pl.pallas_call(
        _compress_kernel,
        grid=(batch,),
        in_specs=[pl.BlockSpec((seq, lanes), functools.partial(lambda c, i: (i, c), c)) for c in range(n_chunks)] + [
            _resident(pos2.shape),
            _resident(w1.shape),
            _resident(w2.shape),
        ],
        out_specs=pl.BlockSpec((1, 2, g, nh, dh), lambda i: (i, 0, 0, 0, 0)),
        out_shape=jax.ShapeDtypeStruct((batch, 2, g, nh, dh), BF16),
        name="compress",
        compiler_params=_params("parallel"),
    )(*([f] * n_chunks), pos2, w1, w2)


def _split3(x):
    p1 = x.astype(BF16)
    r1 = x - p1.astype(F32)
    p2 = r1.astype(BF16)
    p3 = (r1 - p2.astype(F32)).astype(BF16)
    return p1, p2, p3


def _cmp_block_scores(gg, qi, q_ref, kc_ref, vct_ref, tb_ref, ovt_ref, oct_ref):
    tq = SEQ_TILE
    nrep = NSA_GROUP
    dh = HEAD_DIM
    n_cmp_pad = kc_ref.shape[2]
    qt = jnp.concatenate([q_ref[0, 0, (gg * nrep + r) * dh:(gg * nrep + r + 1) * dh, :] for r in range(nrep)],
                         axis=1)
    s = jnp.dot(kc_ref[0, gg], qt, preferred_element_type=F32)
    off = pl.multiple_of(n_cmp_pad - (tq // CMP_STRIDE) * qi, 8)
    bias = jnp.concatenate([tb_ref[gg * nrep + r, pl.ds(off, n_cmp_pad), :] for r in range(nrep)], axis=1)
    s = s + bias
    m = jnp.max(s, axis=0, keepdims=True)
    e = jnp.exp2(s - m)
    assert tq & (tq - 1) == 0
    t = qi * tq + (lax.broadcasted_iota(jnp.int32, (1, nrep * tq), 1) & (tq - 1))
    p = e * jnp.where(t >= CMP_BLOCK - 1, 1.0 / jnp.sum(e, axis=0, keepdims=True), 0.0)

    oct = jnp.dot(vct_ref[0, gg], p.astype(BF16), preferred_element_type=F32)
    for r in range(nrep):
        oct_ref[0, gg, r] = oct[:, r * tq:(r + 1) * tq]

    psum = p[:, 0:tq]
    for r in range(1, nrep):
        psum = psum + p[:, r * tq:(r + 1) * tq]
    ovt = ovt_ref[...]
    imp = sum(jnp.dot(ovt, part, preferred_element_type=F32) for part in _split3(psum))

    n_sel = imp.shape[0]
    blk = lax.broadcasted_iota(jnp.int32, (n_sel, tq), 0)
    cur = jnp.right_shift(qi * tq + lax.broadcasted_iota(jnp.int32, (n_sel, tq), 1), SEL_BLOCK_SHIFT)
    forced = (blk == 0) | (blk == cur) | (blk == cur - 1)
    return jnp.where(blk > cur, -SEL_FORCE, imp + jnp.where(forced, SEL_FORCE, 0.0))


def _cmp_sel_kernel(q_ref, kc_ref, vct_ref, tb_ref, ovt_ref, oct_ref, sel_ref, cnt_ref):
    qi = pl.program_id(2)
    tq = SEQ_TILE
    gps = kc_ref.shape[1]
    scores = [_cmp_block_scores(gg, qi, q_ref, kc_ref, vct_ref, tb_ref, ovt_ref, oct_ref) for gg in range(gps)]
    n_sel = scores[0].shape[0]

    rows = 8
    n_groups = n_sel // rows
    blk_in_group = lax.broadcasted_iota(jnp.int32, (rows, tq), 0)
    last_group = jnp.right_shift(qi * tq + tq - 1, SEL_BLOCK_SHIFT) // rows
    cnt_ref[...] = jnp.zeros_like(cnt_ref)

    def count(score, k, g):
        sg = score[g * rows:(g + 1) * rows]
        total = None
        for sp in range(k * rows, (k + 1) * rows):
            row = jnp.broadcast_to(score[sp:sp + 1, :], (rows, tq))
            if g > k:
                one = jnp.where(row >= sg, 1.0, 0.0)
            elif g < k:
                one = jnp.where(row > sg, 1.0, 0.0)
            else:
                tie = jnp.where(blk_in_group + g * rows > sp, 1.0, 0.0)
                one = jnp.where(row > sg, 1.0, jnp.where(row == sg, tie, 0.0))
            total = one if total is None else total + one
        return total

    for level in range(n_groups):
        @pl.when(level <= last_group)
        def _():
            for gg, score in enumerate(scores):
                for g in range(level + 1):
                    cnt_ref[gg, g * rows:(g + 1) * rows, :] += count(score, level, g)
                extra = None
                for k in range(level):
                    c = count(score, k, level)
                    extra = c if extra is None else extra + c
                if extra is not None:
                    cnt_ref[gg, level * rows:(level + 1) * rows, :] += extra

    sel_ref[0] = jnp.where(cnt_ref[...] < min(SEL_TOPK, n_sel), 1.0, 0.0).astype(BF16)


def _cmp_sel(qkv_t, kcmp, vcmp_t, tb, ovt):
    b, nt, _, _ = qkv_t.shape
    g, r, dh = NSA_KV_HEADS, NSA_GROUP, HEAD_DIM
    t = nt * SEQ_TILE
    n_pad = kcmp.shape[2]
    n_sel = ovt.shape[0]
    tq = SEQ_TILE
    gps = NSA_GROUPS_PER_STEP
    return pl.pallas_call(
        _cmp_sel_kernel,
        grid=(b, g // gps, t // tq),
        in_specs=[
            pl.BlockSpec((1, 1, gps * r * dh, tq), lambda i, j, k: (i, k, j, 0)),
            pl.BlockSpec((1, gps, n_pad, dh), lambda i, j, k: (i, j, 0, 0)),
            pl.BlockSpec((1, gps, dh, n_pad), lambda i, j, k: (i, j, 0, 0)),
            pl.BlockSpec((gps * r, 2 * n_pad, tq), lambda i, j, k: (j, 0, 0)),
            pl.BlockSpec((n_sel, n_pad), lambda i, j, k: (0, 0)),
        ],
        out_specs=[
            pl.BlockSpec((1, gps, r, dh, tq), lambda i, j, k: (i, j, 0, 0, k)),
            pl.BlockSpec((1, gps, n_sel, tq), lambda i, j, k: (i, j, 0, k)),
        ],
        out_shape=[
            jax.ShapeDtypeStruct((b, g, r, dh, t), F32),
            jax.ShapeDtypeStruct((b, g, n_sel, t), BF16),
        ],
        scratch_shapes=[pltpu.VMEM((gps, n_sel, tq), F32)],
        name="cmp_sel",
        compiler_params=_params("parallel", "parallel", "parallel"),
    )(qkv_t, kcmp, vcmp_t, tb, ovt)


def _ones_rows(width):
    return jnp.ones((FEAT_ROWS, width), BF16)


def _with_ones(v_t):
    return jnp.concatenate([v_t, _ones_rows(v_t.shape[1])], axis=0)


def _online_update(m_ref, acc_ref, slot, tiles, first):
    mt = None
    for s, _ in tiles:
        cm = jnp.max(s, axis=0, keepdims=True)
        mt = cm if mt is None else jnp.maximum(mt, cm)
    m_old = None if first else m_ref[slot]
    m_new = mt if first else jnp.maximum(m_old, mt)
    pv = None
    for s, va in tiles:
        d = jnp.dot(va, jnp.exp2(s - m_new).astype(BF16), preferred_element_type=F32)
        pv = d if pv is None else pv + d
    acc_ref[slot] = pv if first else jnp.exp2(m_old - m_new) * acc_ref[slot] + pv
    m_ref[slot] = m_new


def _pipelined_updates(m_ref, acc_ref, jobs, tiles_per_job=1):
    lookahead = max(1, QK_LOOKAHEAD_TILES // tiles_per_job)
    pending = [job[1]() for job in jobs[:lookahead]]
    for i, (slot, _, values_fn, first) in enumerate(jobs):
        current = pending.pop(0)
        if i + lookahead < len(jobs):
            pending.append(jobs[i + lookahead][1]())
        _online_update(m_ref, acc_ref, slot, list(zip(current, values_fn())), first)


def _for_far_tiles(n_far, run):
    n_quads = n_far // 4

    def quad(i, carry):
        run([[4 * i, 4 * i + 1], [4 * i + 2, 4 * i + 3]])
        return carry

    lax.fori_loop(0, n_quads, quad, 0)

    @pl.when((n_far & 2) != 0)
    def _():
        run([[4 * n_quads, 4 * n_quads + 1]])

    @pl.when((n_far & 1) != 0)
    def _():
        run([[n_far - 1]])


def _normalised(acc_ref, slot):
    acc = acc_ref[slot]
    return acc[:HEAD_DIM] / acc[HEAD_DIM:HEAD_DIM + 1]


def _sel_win_kernel(q_ref, ks_ref, vs_ref, kw_ref, vw_ref, sel_ref, oh_ref, *rest):
    nq = NSA_Q_TILES_PER_STEP
    tab_refs, far2_refs = rest[:nq], rest[nq:2 * nq]
    gate_ref, oc_ref, o_ref, ksrm_ref, kwrm_ref, m_ref, acc_ref = rest[2 * nq:]
    step_id = pl.program_id(2)
    tq = SEQ_TILE
    nrep = NSA_GROUP
    gps = NSA_GROUPS_PER_STEP
    dh = HEAD_DIM
    pad = jnp.zeros((KEY_FEATS - dh, tq), BF16)
    base = nq * step_id

    @pl.when(step_id == 0)
    def _():
        def to_rows(c, carry):
            for gg in range(gps):
                ksa = jnp.concatenate([ks_ref[0, c, gg * dh:(gg + 1) * dh, :], oh_ref[c]], axis=0)
                ksrm_ref[gg, c] = ksa.astype(F32).T.astype(BF16)
                kwa = jnp.concatenate([kw_ref[0, c, gg * dh:(gg + 1) * dh, :], pad], axis=0)
                kwrm_ref[gg, c] = kwa.astype(F32).T.astype(BF16)
            return carry

        lax.fori_loop(0, ks_ref.shape[1], to_rows, 0)

    q_sel, q_win, q_none = [], [], []
    for sub in range(nq):
        lanes = slice(sub * tq, (sub + 1) * tq)
        q_sel.append([]), q_win.append([]), q_none.append([])
        for gg in range(gps):
            mq = ((sel_ref[0, gg, :, lanes].astype(F32) - 1.0) * (-NEG_INF)).astype(BF16)
            for r in range(nrep):
                q = q_ref[0, sub, (gg * nrep + r) * dh:(gg * nrep + r + 1) * dh, :]
                q_sel[sub].append(jnp.concatenate([q, mq], axis=0))
                q_win[sub].append(jnp.concatenate([q, pad], axis=0))
                q_none[sub].append(jnp.concatenate([q, jnp.full_like(mq, NEG_INF)], axis=0))

    def jobs(sub, branch, cs, bias=None, first=False, queries=None):
        k_ref, v_ref = (ksrm_ref, vs_ref) if branch == 0 else (kwrm_ref, vw_ref)
        qa = queries if queries is not None else (q_sel if branch == 0 else q_win)[sub]

        def logits(gg, r):
            head = gg * nrep + r
            tiles = [jnp.dot(k_ref[gg, c], qa[head], preferred_element_type=F32) for c in cs]
            return tiles if bias is None else [s + bias(head) for s in tiles]

        def values(gg):
            return [_with_ones(v_ref[0, c, gg * dh:(gg + 1) * dh, :]) for c in cs]

        return [(((sub * gps + gg) * 2 + branch) * nrep + r, functools.partial(logits, gg, r),
                 functools.partial(values, gg), first) for gg in range(gps) for r in range(nrep)]

    near = []
    for sub in range(nq):
        diag = lambda head, sub=sub: tab_refs[sub][0, head, 1]
        near += jobs(sub, 0, [base + sub], diag, first=True) + jobs(sub, 1, [base + sub], diag, first=True)
    for sub in range(nq):
        prev = lambda head, sub=sub: tab_refs[sub][0, head, 0]
        c_prev = jnp.maximum(base + sub - 1, 0)
        near += jobs(sub, 0, [c_prev], prev) + jobs(sub, 1, [c_prev], prev)
    for sub in range(nq):
        edge = lambda head, sub=sub: far2_refs[sub][0]
        near += jobs(sub, 1, [jnp.maximum(base + sub - 2, 0)], edge)
    for sub in range(nq):
        for e in range(sub):
            if e == 0:
                masked = [jnp.where(step_id > 0, a, b) for a, b in zip(q_sel[sub], q_none[sub])]
                near += jobs(sub, 0, [jnp.maximum(base - 1, 0)], queries=masked)
            else:
                near += jobs(sub, 0, [base - 1 + e])
    _pipelined_updates(m_ref, acc_ref, near)

    _for_far_tiles(jnp.maximum(base - 1, 0), lambda groups: _pipelined_updates(
        m_ref, acc_ref, [job for cs in groups for sub in range(nq) for job in jobs(sub, 0, cs)],
        tiles_per_job=len(groups[0])))

    for sub in range(nq):
        lanes = slice(sub * tq, (sub + 1) * tq)
        outs = []
        for gg in range(gps):
            gates = jax.nn.sigmoid(gate_ref[0, gg, :, lanes])
            slot = (sub * gps + gg) * 2 * nrep
            for r in range(nrep):
                outs.append(gates[3 * r:3 * r + 1] * oc_ref[0, gg, r, :, lanes]
                            + gates[3 * r + 1:3 * r + 2] * _normalised(acc_ref, slot + r)
                            + gates[3 * r + 2:3 * r + 3] * _normalised(acc_ref, slot + nrep + r))
        o_ref[0, lanes, :] = jnp.concatenate(outs, axis=0).T.astype(BF16)


def _sel_win(qkv_t, sel_t, onehot_t, tab, far2, gates_t, oc_t):
    b, nt, channels, tq = qkv_t.shape
    g, r, dh = NSA_KV_HEADS, NSA_GROUP, HEAD_DIM
    gps, nq = NSA_GROUPS_PER_STEP, NSA_Q_TILES_PER_STEP
    hd = g * r * dh
    n_sel = sel_t.shape[2]
    kv_block = lambda which: pl.BlockSpec(
        (1, nt, gps * dh, tq), lambda i, j, k: (i, 0, (hd + which * g * dh) // (gps * dh) + j, 0))
    tab_block = lambda sub: pl.BlockSpec(
        (1, gps * r, 2, tq, tq), lambda i, j, k: (jnp.minimum(nq * k + sub, tab.shape[0] - 1), j, 0, 0, 0))
    far2_block = lambda sub: pl.BlockSpec(
        (1, tq, tq), lambda i, j, k: (jnp.minimum(nq * k + sub, far2.shape[0] - 1), 0, 0))
    slots = nq * gps * 2 * r
    return pl.pallas_call(
        _sel_win_kernel,
        grid=(b, g // gps, nt // nq),
        in_specs=[
            pl.BlockSpec((1, nq, gps * r * dh, tq), lambda i, j, k: (i, k, j, 0)),
            kv_block(0), kv_block(1), kv_block(2), kv_block(3),
            pl.BlockSpec((1, gps, n_sel, nq * tq), lambda i, j, k: (i, j, 0, k)),
            pl.BlockSpec((nt, n_sel, tq), lambda i, j, k: (0, 0, 0)),
            *[tab_block(sub) for sub in range(nq)],
            *[far2_block(sub) for sub in range(nq)],
            pl.BlockSpec((1, gps, FEAT_ROWS, nq * tq), lambda i, j, k: (i, j, 0, k)),
            pl.BlockSpec((1, gps, r, dh, nq * tq), lambda i, j, k: (i, j, 0, 0, k)),
        ],
        out_specs=pl.BlockSpec((1, nq * tq, gps * r * dh), lambda i, j, k: (i, k, j)),
        out_shape=jax.ShapeDtypeStruct((b, nt * tq, hd), BF16),
        scratch_shapes=[pltpu.VMEM((gps, nt, tq, KEY_FEATS), BF16), pltpu.VMEM((gps, nt, tq, KEY_FEATS), BF16),
                        pltpu.VMEM((slots, 1, tq), F32), pltpu.VMEM((slots, ACC_ROWS, tq), F32)],
        name="sel_win",
        compiler_params=_params("parallel", "parallel", "arbitrary"),
    )(qkv_t, qkv_t, qkv_t, qkv_t, qkv_t, sel_t, onehot_t, *([tab] * nq), *([far2] * nq), gates_t, oc_t)


N_DECAY_PIECES = 3


def _forget_cumsum_kernel(fl_ref, bf_ref, piece_ref):
    x = fl_ref[0] + bf_ref[...]
    y = jnp.minimum(x, 0.0) - jnp.log1p(jnp.exp(-jnp.abs(x)))
    t = y.shape[1]
    lane = lax.broadcasted_iota(jnp.int32, y.shape, 1)
    k = 1
    while k < t:
        y = y + jnp.where(lane >= k, pltpu.roll(y, k, axis=1), 0.0)
        k *= 2
    for i, piece in enumerate(_split3(-LOG2E * y)):
        piece_ref[0, i] = piece


def _forget_cumsum(fl_t, b_f):
    b, h, t = fl_t.shape
    return pl.pallas_call(
        _forget_cumsum_kernel,
        grid=(b,),
        in_specs=[pl.BlockSpec((1, h, t), lambda i: (i, 0, 0)), pl.BlockSpec((h, 1), lambda i: (0, 0))],
        out_specs=pl.BlockSpec((1, N_DECAY_PIECES, h, t), lambda i: (i, 0, 0, 0)),
        out_shape=jax.ShapeDtypeStruct((b, N_DECAY_PIECES, h, t), BF16),
        name="forget_cumsum",
        compiler_params=_params("parallel"),
    )(fl_t, b_f.reshape(h, 1))


FOX_HEADS_PER_STEP = 8
FOX_Q_TILES_PER_STEP = 2


def _fox_kernel(q_ref, k_ref, v_ref, kf_ref, tri_ref, o_ref, krm_ref, m_ref, acc_ref):
    step_id = pl.program_id(2)
    tq = SEQ_TILE
    hp = FOX_HEADS_PER_STEP
    nq = FOX_Q_TILES_PER_STEP
    n_pieces = N_DECAY_PIECES
    pad = jnp.zeros((KEY_FEATS - HEAD_DIM - FEAT_ROWS, tq), BF16)

    @pl.when(step_id == 0)
    def _():
        def to_rows(c, carry):
            for h in range(hp):
                ka = jnp.concatenate([k_ref[0, c, h * HEAD_DIM:(h + 1) * HEAD_DIM, :], kf_ref[0, c, h], pad], axis=0)
                krm_ref[h, c] = ka.astype(F32).T.astype(BF16)
            return carry

        lax.fori_loop(0, k_ref.shape[1], to_rows, 0)

    q_feat = jnp.where(lax.broadcasted_iota(jnp.int32, (FEAT_ROWS, tq), 0) < n_pieces, 1.0, 0.0).astype(BF16)
    qa = [[jnp.concatenate([q_ref[0, sub, h * HEAD_DIM:(h + 1) * HEAD_DIM, :], q_feat, pad], axis=0)
           for h in range(hp)] for sub in range(nq)]

    def jobs(sub, cs, first=False, causal=False):
        def logits(h):
            tiles = [jnp.dot(krm_ref[h, c], qa[sub][h], preferred_element_type=F32) for c in cs]
            return [s + tri_ref[...] for s in tiles] if causal else tiles

        def values(h):
            return [_with_ones(v_ref[0, c, h * HEAD_DIM:(h + 1) * HEAD_DIM, :]) for c in cs]

        return [(sub * hp + h, functools.partial(logits, h), functools.partial(values, h), first)
                for h in range(hp)]

    base = nq * step_id
    near = [job for sub in range(nq) for job in jobs(sub, [base + sub], first=True, causal=True)]
    near += [job for sub in range(nq) for c in range(sub) for job in jobs(sub, [base + c])]
    _pipelined_updates(m_ref, acc_ref, near)
    _for_far_tiles(base, lambda groups: _pipelined_updates(
        m_ref, acc_ref, [job for cs in groups for sub in range(nq) for job in jobs(sub, cs)],
        tiles_per_job=len(groups[0])))
    for sub in range(nq):
        o_t = jnp.concatenate([_normalised(acc_ref, sub * hp + h) for h in range(hp)], axis=0)
        o_ref[0, sub * tq:(sub + 1) * tq, :] = o_t.T.astype(BF16)


def _fox_attention(q_t, kv_t, kfeat_t, tri):
    b, nt, hd, tq = q_t.shape
    hp, dh, nq = FOX_HEADS_PER_STEP, HEAD_DIM, FOX_Q_TILES_PER_STEP
    n_groups = hd // (hp * dh)
    return pl.pallas_call(
        _fox_kernel,
        grid=(b, n_groups, nt // nq),
        in_specs=[
            pl.BlockSpec((1, nq, hp * dh, tq), lambda i, j, k: (i, k, j, 0)),
            pl.BlockSpec((1, nt, hp * dh, tq), lambda i, j, k: (i, 0, j, 0)),
            pl.BlockSpec((1, nt, hp * dh, tq), lambda i, j, k: (i, 0, n_groups + j, 0)),
            pl.BlockSpec((1, nt, hp, FEAT_ROWS, tq), lambda i, j, k: (i, 0, j, 0, 0)),
            pl.BlockSpec((tq, tq), lambda i, j, k: (0, 0)),
        ],
        out_specs=pl.BlockSpec((1, nq * tq, hp * dh), lambda i, j, k: (i, k, j)),
        out_shape=jax.ShapeDtypeStruct((b, nt * tq, hd), BF16),
        scratch_shapes=[pltpu.VMEM((hp, nt, tq, KEY_FEATS), BF16),
                        pltpu.VMEM((nq * hp, 1, tq), F32), pltpu.VMEM((nq * hp, ACC_ROWS, tq), F32)],
        name="fox_attention",
        compiler_params=_params("parallel", "parallel", "arbitrary"),
    )(q_t, kv_t, kv_t, kfeat_t, tri)


def _t5_bucket_np(dist):
    dist = np.maximum(dist, 0)
    max_exact = N_BUCKETS // 2
    ratio = np.maximum(dist, 1).astype(np.float32) / np.float32(max_exact)
    scaled = np.log(ratio) / np.float32(math.log(MAX_DISTANCE / max_exact))
    large = np.minimum(max_exact + (scaled * np.float32(N_BUCKETS - max_exact)).astype(np.int32), N_BUCKETS - 1)
    return np.where(dist < max_exact, dist, large).astype(np.int32)


_MASKED = MAX_DISTANCE + 1


def _dist_index(dist, valid):
    return np.where(valid, np.minimum(dist, MAX_DISTANCE), _MASKED).astype(np.int32)


def _toeplitz(f_ext, n):
    period = 2 * n
    flat = jnp.tile(f_ext, n)[..., :n * (period - 1)]
    return flat.reshape(f_ext.shape[:-1] + (n, period - 1))[..., :n]


def _key_query_delta(n):
    p = np.arange(2 * n)
    return np.where(p < n, p, p - 2 * n)


def _bias_tables(rel_bias, n_pad):
    bucket = _t5_bucket_np(np.arange(MAX_DISTANCE + 1))
    by_dist = jnp.concatenate([rel_bias[bucket], jnp.full((1, N_HEADS), NEG_INF, F32)], axis=0).T
    far_const = by_dist[:, MAX_DISTANCE:MAX_DISTANCE + 1]

    tq = SEQ_TILE
    u = np.arange(2 * n_pad) - n_pad
    dist = np.arange(tq)[None, :] - CMP_STRIDE * u[:, None] - (CMP_BLOCK - 1)
    varying = np.nonzero((dist.max(axis=1) >= 0) & (dist.min(axis=1) < MAX_DISTANCE))[0]
    lo, hi = int(varying[0]), int(varying[-1]) + 1
    strip = by_dist[:, _dist_index(dist[lo:hi], dist[lo:hi] >= 0)]
    tb = jnp.concatenate([
        jnp.broadcast_to(far_const[:, :, None], (N_HEADS, lo, tq)),
        strip,
        jnp.full((N_HEADS, 2 * n_pad - hi, tq), NEG_INF, F32)], axis=1)

    shifted = jnp.concatenate([by_dist[:, :-1] - far_const, by_dist[:, -1:]], axis=1)
    n = SEQ_TILE
    delta = _key_query_delta(n)
    prev = _toeplitz(shifted[:, _dist_index(delta + n, delta + n >= 0)], n)
    diag = _toeplitz(shifted[:, _dist_index(delta, delta >= 0)], n)
    tab = jnp.stack([prev, diag], axis=1)
    no_prev = jnp.stack([jnp.full_like(prev, NEG_INF), diag], axis=1)
    return LOG2E * tb, LOG2E * jnp.stack([no_prev, tab])


def _static_tiles(seq):
    n = SEQ_TILE
    key = np.arange(n)[:, None]
    query = np.arange(n)[None, :]
    tri = np.where(key <= query, 0.0, NEG_INF).astype(np.float32)
    far2 = np.where(query - key + 2 * n < WINDOW, 0.0, NEG_INF).astype(np.float32)
    assert WINDOW == 2 * n
    far2 = np.stack([np.full_like(far2, NEG_INF)] * 2 + [far2])
    n_sel = seq // SEL_BLOCK
    pos = np.arange(seq).reshape(seq // n, 1, n)
    onehot = (pos // SEL_BLOCK == np.arange(n_sel)[None, :, None]).astype(np.float32)
    return jnp.asarray(tri), jnp.asarray(far2), jnp.asarray(onehot, BF16)


def _overlap_t(n_pad, n_sel):
    n_cmp = n_pad - 1
    cmp_start = np.arange(n_pad) * CMP_STRIDE
    sel_start = np.arange(n_sel) * SEL_BLOCK
    ov = ((cmp_start[None, :] < sel_start[:, None] + SEL_BLOCK)
          & (cmp_start[None, :] + CMP_BLOCK > sel_start[:, None])
          & (np.arange(n_pad)[None, :] < n_cmp))
    return jnp.asarray(ov.astype(np.float32), BF16)


def _nsa_mixer(x, g, w_in, cmp_pos, cmp_w1, cmp_w2, rel_bias, batch, seq):
    n, d = x.shape
    G, R, dh = NSA_KV_HEADS, NSA_GROUP, HEAD_DIM
    hd, kvd = N_HEADS * dh, G * dh
    cols = [hd + i * kvd for i in range(7)]
    wq, wkc, wvc, wks, wvs, wkw, wvw, wgl = jnp.split(w_in, cols, axis=1)
    n_gate = wgl.shape[1]
    wt = jnp.concatenate([wq * Q_PRESCALE, wks, wvs, wkw, wvw], axis=1).T.astype(BF16)
    w2 = jnp.concatenate([wkc, wvc, wgl, jnp.zeros((d, -n_gate % 128), w_in.dtype)], axis=1).astype(BF16)
    qkv_t, f = _norm_proj(x, g, wt, w2, batch, seq)

    gates_t = f[:, 2 * kvd:2 * kvd + n_gate].reshape(batch, seq, G, R * 3).transpose(0, 2, 3, 1)
    gates_t = jnp.pad(gates_t, ((0, 0), (0, 0), (0, FEAT_ROWS - R * 3), (0, 0)))

    nh = seq // CMP_STRIDE
    cmp = _compress(f, cmp_pos.reshape(2, 2, CMP_STRIDE * dh), cmp_w1.astype(BF16), cmp_w2.astype(BF16),
                    batch, seq)
    kcmp = cmp[:, 0]
    vcmp_t = cmp[:, 1].transpose(0, 1, 3, 2)

    n_sel = seq // SEL_BLOCK
    tb, tab = _bias_tables(rel_bias, nh)
    _, far2, onehot_t = _static_tiles(seq)
    oc_t, sel_t = _cmp_sel(qkv_t, kcmp, vcmp_t, tb, _overlap_t(nh, n_sel))
    o = _sel_win(qkv_t, sel_t, onehot_t, tab, far2, gates_t, oc_t)
    return o.reshape(n, hd)


def _shared_kv(x, g, w_kvf, b_f, batch, seq):
    hd = N_HEADS * HEAD_DIM
    wt = w_kvf[:, :2 * hd].T.astype(BF16)
    wf = w_kvf[:, 2 * hd:]
    w2 = jnp.concatenate([wf, jnp.zeros((wf.shape[0], -wf.shape[1] % 128), wf.dtype)], axis=1).astype(BF16)
    kv_t, f = _norm_proj(x, g, wt, w2, batch, seq)
    fl_t = f[:, :N_HEADS].reshape(batch, seq, N_HEADS).transpose(0, 2, 1)
    pieces = _forget_cumsum(fl_t, b_f.astype(F32)).transpose(0, 2, 1, 3)
    pieces = jnp.pad(pieces, ((0, 0), (0, 0), (0, FEAT_ROWS - N_DECAY_PIECES), (0, 0)))
    kfeat_t = pieces.reshape(batch, N_HEADS, FEAT_ROWS, seq // SEQ_TILE, SEQ_TILE).transpose(0, 3, 1, 2, 4)
    return kv_t, kfeat_t


def _fox_mixer(x, g, w_q, kv, batch, seq):
    n, _ = x.shape
    hd = N_HEADS * HEAD_DIM
    (q_t,) = _norm_proj(x, g, (w_q * Q_PRESCALE).T.astype(BF16), None, batch, seq)
    tri, _, _ = _static_tiles(seq)
    o = _fox_attention(q_t, *kv, tri)
    return o.reshape(n, hd)


def kernel(x, norm_g, ffn_w_in, ffn_w_out, nsa_w_in, nsa_cmp_pos, nsa_cmp_w1, nsa_cmp_w2, nsa_w_out, rel_bias,
           kv_norm_g, fox_w_kvf, fox_b_f, fox_w_q, fox_w_out, final_g):
    batch, seq, d = x.shape
    depth = norm_g.shape[0]
    n_a = nsa_w_in.shape[0]
    xf = x.reshape(batch * seq, d)
    w_in = ffn_w_in.astype(BF16)
    w_out = ffn_w_out.astype(BF16)
    kv = None
    for l in range(depth):
        if l == n_a:
            kv = _shared_kv(xf, kv_norm_g, fox_w_kvf, fox_b_f, batch, seq)
        xf = _ffn(xf, norm_g[l, 0], w_in, w_out, (l, 0))
        if l < n_a:
            o = _nsa_mixer(xf, norm_g[l, 1], nsa_w_in[l], nsa_cmp_pos[l], nsa_cmp_w1[l], nsa_cmp_w2[l],
                           rel_bias, batch, seq)
            w_mix = nsa_w_out[l]
        else:
            o = _fox_mixer(xf, norm_g[l, 1], fox_w_q[l - n_a], kv, batch, seq)
            w_mix = fox_w_out[l - n_a]
        xf = _ffn(xf, norm_g[l, 2], w_in, w_out, (l, 1), mixer=(o, w_mix.astype(BF16)),
                  final_g=final_g if l == depth - 1 else None)
    return xf.reshape(batch, seq, d)
```

```python
import functools
import math

import numpy as np
import jax
import jax.numpy as jnp
from jax import lax
from jax.experimental import pallas as pl
from jax.experimental.pallas import tpu as pltpu

N_HEADS = 16
HEAD_DIM = 64
NSA_KV_HEADS = 4
NSA_GROUP = N_HEADS // NSA_KV_HEADS
NSA_GROUPS_PER_STEP = 2
NSA_Q_TILES_PER_STEP = 2
CMP_BLOCK = 32
CMP_STRIDE = 16
SEL_BLOCK = 64
SEL_BLOCK_SHIFT = SEL_BLOCK.bit_length() - 1
assert 1 << SEL_BLOCK_SHIFT == SEL_BLOCK
SEL_TOPK = 16
WINDOW = 512
N_BUCKETS = 32
MAX_DISTANCE = 128
EPS = 1e-6
NEG_INF = -1e30
LOG2E = math.log2(math.e)
Q_PRESCALE = LOG2E * HEAD_DIM ** -0.5
SEL_FORCE = 1e4

F32 = jnp.float32
BF16 = jnp.bfloat16

VMEM_LIMIT_BYTES = 48 * 1024 * 1024

ROW_TILE = 512
FF_TILE = 256
SEQ_TILE = 256
FEAT_ROWS = 16
ACC_ROWS = HEAD_DIM + FEAT_ROWS
QK_LOOKAHEAD_TILES = 6
KEY_FEATS = 128

_NT = (((1,), (1,)), ((), ()))
_TN = (((0,), (0,)), ((), ()))


def _params(*sem):
    return pltpu.CompilerParams(dimension_semantics=sem, vmem_limit_bytes=VMEM_LIMIT_BYTES)


def _rms(x, g):
    return x * lax.rsqrt(jnp.mean(x * x, axis=-1, keepdims=True) + EPS) * g


def _silu(a):
    return a * jax.nn.sigmoid(a)


def _ffn_kernel(x_ref, g_ref, wi_ref, wo_ref, *rest, has_mixer, final_norm):
    d_ff = wo_ref.shape[2]
    rest = list(rest)
    mix_ref, wm_ref = (rest.pop(0), rest.pop(0)) if has_mixer else (None, None)
    fg_ref = rest.pop(0) if final_norm else None
    (o_ref,) = rest
    x = x_ref[...]
    if has_mixer:
        x = x + jnp.dot(mix_ref[...], wm_ref[...], preferred_element_type=F32)
    hn = _rms(x, g_ref[...]).astype(BF16)
    acc = None
    for c in range(0, d_ff, FF_TILE):
        a = jnp.dot(hn, wi_ref[0, 0, :, c:c + FF_TILE], preferred_element_type=F32)
        b = jnp.dot(hn, wi_ref[0, 0, :, d_ff + c:d_ff + c + FF_TILE], preferred_element_type=F32)
        part = jnp.dot((_silu(a) * b).astype(BF16), wo_ref[0, 0, c:c + FF_TILE, :], preferred_element_type=F32)
        acc = part if acc is None else acc + part
    y = x + 0.5 * acc
    if final_norm:
        y = _rms(y, fg_ref[...])
    o_ref[...] = y


def _resident(shape, index=None):
    index = (0,) * len(shape) if index is None else index
    return pl.BlockSpec(shape, lambda *_: index, pipeline_mode=pl.Buffered(1))


def _ffn(x, g, w_in, w_out, which, mixer=None, final_g=None):
    n, d = x.shape
    in_specs = [
        pl.BlockSpec((ROW_TILE, d), lambda i: (i, 0)),
        _resident((1, d)),
        _resident((1, 1) + w_in.shape[2:], which + (0, 0)),
        _resident((1, 1) + w_out.shape[2:], which + (0, 0)),
    ]
    args = [x, g.reshape(1, d), w_in, w_out]
    if mixer is not None:
        o, w_mix = mixer
        in_specs += [pl.BlockSpec((ROW_TILE, o.shape[1]), lambda i: (i, 0)), _resident(w_mix.shape)]
        args += [o, w_mix]
    if final_g is not None:
        in_specs.append(_resident((1, d)))
        args.append(final_g.reshape(1, d))
    return pl.pallas_call(
        functools.partial(_ffn_kernel, has_mixer=mixer is not None, final_norm=final_g is not None),
        grid=(n // ROW_TILE,),
        in_specs=in_specs,
        out_specs=pl.BlockSpec((ROW_TILE, d), lambda i: (i, 0)),
        out_shape=jax.ShapeDtypeStruct((n, d), F32),
        name="ffn",
        compiler_params=_params("parallel"),
    )(*args)


PROJ_CHUNK = 512


def _norm_proj_kernel(x_ref, g_ref, wt_ref, *rest, has_rows):
    if has_rows:
        w2_ref, ot_ref, o2_ref = rest
    else:
        (ot_ref,) = rest
    channels = ot_ref.shape[2]
    for j in range(ot_ref.shape[1]):
        rows = slice(j * SEQ_TILE, (j + 1) * SEQ_TILE)
        hn = _rms(x_ref[rows, :], g_ref[...]).astype(BF16)
        for c in range(0, channels, PROJ_CHUNK):
            ot_ref[0, j, c:c + PROJ_CHUNK, :] = lax.dot_general(
                wt_ref[c:c + PROJ_CHUNK, :], hn, _NT, preferred_element_type=F32).astype(BF16)
        if has_rows:
            o2_ref[rows, :] = jnp.dot(hn, w2_ref[...], preferred_element_type=F32)


def _norm_proj(x, g, wt, w2, batch, seq):
    n, d = x.shape
    channels = wt.shape[0]
    per_seq = seq // ROW_TILE
    sub = ROW_TILE // SEQ_TILE
    in_specs = [
        pl.BlockSpec((ROW_TILE, d), lambda i: (i, 0)),
        pl.BlockSpec((1, d), lambda i: (0, 0)),
        pl.BlockSpec((channels, d), lambda i: (0, 0)),
    ]
    out_specs = [pl.BlockSpec((1, sub, channels, SEQ_TILE), lambda i: (i // per_seq, i % per_seq, 0, 0))]
    out_shape = [jax.ShapeDtypeStruct((batch, seq // SEQ_TILE, channels, SEQ_TILE), BF16)]
    args = [x, g.reshape(1, d), wt]
    if w2 is not None:
        n2 = w2.shape[1]
        in_specs.append(pl.BlockSpec((d, n2), lambda i: (0, 0)))
        out_specs.append(pl.BlockSpec((ROW_TILE, n2), lambda i: (i, 0)))
        out_shape.append(jax.ShapeDtypeStruct((n, n2), F32))
        args.append(w2)
    return pl.pallas_call(
        functools.partial(_norm_proj_kernel, has_rows=w2 is not None),
        grid=(n // ROW_TILE,),
        in_specs=in_specs,
        out_specs=out_specs,
        out_shape=out_shape,
        name="norm_proj",
        compiler_params=_params("parallel"),
    )(*args)


def _compress_kernel(*refs):
    *f_refs, pos_ref, w1_ref, w2_ref, o_ref = refs
    nh = f_refs[0].shape[0] // CMP_STRIDE
    dh = HEAD_DIM
    per_chunk = f_refs[0].shape[1] // dh
    for chunk, f_ref in enumerate(f_refs):
        rows = [f_ref[pl.ds(j, nh, stride=CMP_STRIDE), :] for j in range(CMP_STRIDE)]
        for sub in range(per_chunk):
            kv, g = divmod(chunk * per_chunk + sub, NSA_KV_HEADS)
            x = jnp.concatenate([r[:, sub * dh:(sub + 1) * dh] for r in rows], axis=1)
            pos = pos_ref[kv]
            half = x.shape[1]
            top = jnp.dot((x + pos[0:1]).astype(BF16), w1_ref[kv, :half, :], preferred_element_type=F32)
            bot = jnp.dot((x + pos[1:2]).astype(BF16), w1_ref[kv, half:, :], preferred_element_type=F32)
            pre = top + pltpu.roll(bot, nh - 1, axis=0)
            hid = _silu(pre).astype(BF16)
            o_ref[0, kv, g] = jnp.dot(hid, w2_ref[kv], preferred_element_type=F32).astype(BF16)


def _compress(f, pos2, w1, w2, batch, seq):
    g, dh = NSA_KV_HEADS, HEAD_DIM
    nh = seq // CMP_STRIDE
    lanes = 128
    n_chunks = 2 * g * dh // lanes
    return pl.pallas_call(
        _compress_kernel,
        grid=(batch,),
        in_specs=[pl.BlockSpec((seq, lanes), functools.partial(lambda c, i: (i, c), c)) for c in range(n_chunks)] + [
            _resident(pos2.shape),
            _resident(w1.shape),
            _resident(w2.shape),
        ],
        out_specs=pl.BlockSpec((1, 2, g, nh, dh), lambda i: (i, 0, 0, 0, 0)),
        out_shape=jax.ShapeDtypeStruct((batch, 2, g, nh, dh), BF16),
        name="compress",
        compiler_params=_params("parallel"),
    )(*([f] * n_chunks), pos2, w1, w2)


def _split3(x):
    p1 = x.astype(BF16)
    r1 = x - p1.astype(F32)
    p2 = r1.astype(BF16)
    p3 = (r1 - p2.astype(F32)).astype(BF16)
    return p1, p2, p3


def _cmp_block_scores(gg, qi, n_rows, q_ref, kc_ref, vct_ref, tb_ref, ovt_ref, oct_ref):
    tq = SEQ_TILE
    nrep = NSA_GROUP
    dh = HEAD_DIM
    n_cmp_pad = kc_ref.shape[2]
    qt = jnp.concatenate([q_ref[0, 0, (gg * nrep + r) * dh:(gg * nrep + r + 1) * dh, :] for r in range(nrep)],
                         axis=1)
    s = jnp.dot(kc_ref[0, gg, :n_rows, :], qt, preferred_element_type=F32)
    off = pl.multiple_of(n_cmp_pad - (tq // CMP_STRIDE) * qi, 8)
    bias = jnp.concatenate([tb_ref[gg * nrep + r, pl.ds(off, n_rows), :] for r in range(nrep)], axis=1)
    s = s + bias
    m = jnp.max(s, axis=0, keepdims=True)
    e = jnp.exp2(s - m)
    assert tq & (tq - 1) == 0
    t = qi * tq + (lax.broadcasted_iota(jnp.int32, (1, nrep * tq), 1) & (tq - 1))
    p = e * jnp.where(t >= CMP_BLOCK - 1, 1.0 / jnp.sum(e, axis=0, keepdims=True), 0.0)

    oct = jnp.dot(vct_ref[0, gg, :, :n_rows], p.astype(BF16), preferred_element_type=F32)
    for r in range(nrep):
        oct_ref[0, gg, r] = oct[:, r * tq:(r + 1) * tq]

    psum = p[:, 0:tq]
    for r in range(1, nrep):
        psum = psum + p[:, r * tq:(r + 1) * tq]
    ovt = ovt_ref[:, :n_rows]
    imp = sum(jnp.dot(ovt, part, preferred_element_type=F32) for part in _split3(psum))

    n_sel = imp.shape[0]
    blk = lax.broadcasted_iota(jnp.int32, (n_sel, tq), 0)
    cur = jnp.right_shift(qi * tq + lax.broadcasted_iota(jnp.int32, (n_sel, tq), 1), SEL_BLOCK_SHIFT)
    forced = (blk == 0) | (blk == cur) | (blk == cur - 1)
    return jnp.where(blk > cur, -SEL_FORCE, imp + jnp.where(forced, SEL_FORCE, 0.0))


def _cmp_sel_kernel(q_ref, kc_ref, vct_ref, tb_ref, ovt_ref, oct_ref, sel_ref, score_ref, cnt_ref):
    qi = pl.program_id(2)
    tq = SEQ_TILE
    gps, n_cmp_pad = kc_ref.shape[1], kc_ref.shape[2]
    n_sel = score_ref.shape[1]

    half = n_cmp_pad // 2
    last_visible = (qi * tq + tq - CMP_BLOCK) // CMP_STRIDE
    for n_rows, cond in ((half, last_visible < half), (n_cmp_pad, last_visible >= half)):
        @pl.when(cond)
        def _():
            for gg in range(gps):
                score_ref[gg] = _cmp_block_scores(gg, qi, n_rows, q_ref, kc_ref, vct_ref, tb_ref, ovt_ref, oct_ref)

    scores = [score_ref[gg] for gg in range(gps)]

    rows = 8
    n_groups = n_sel // rows
    blk_in_group = lax.broadcasted_iota(jnp.int32, (rows, tq), 0)
    last_group = jnp.right_shift(qi * tq + tq - 1, SEL_BLOCK_SHIFT) // rows
    cnt_ref[...] = jnp.zeros_like(cnt_ref)

    def count(score, k, g):
        sg = score[g * rows:(g + 1) * rows]
        total = None
        for sp in range(k * rows, (k + 1) * rows):
            row = jnp.broadcast_to(score[sp:sp + 1, :], (rows, tq))
            if g > k:
                one = jnp.where(row >= sg, 1.0, 0.0)
            elif g < k:
                one = jnp.where(row > sg, 1.0, 0.0)
            else:
                tie = jnp.where(blk_in_group + g * rows > sp, 1.0, 0.0)
                one = jnp.where(row > sg, 1.0, jnp.where(row == sg, tie, 0.0))
            total = one if total is None else total + one
        return total

    for level in range(n_groups):
        @pl.when(level <= last_group)
        def _():
            for gg, score in enumerate(scores):
                for g in range(level + 1):
                    cnt_ref[gg, g * rows:(g + 1) * rows, :] += count(score, level, g)
                extra = None
                for k in range(level):
                    c = count(score, k, level)
                    extra = c if extra is None else extra + c
                if extra is not None:
                    cnt_ref[gg, level * rows:(level + 1) * rows, :] += extra

    sel_ref[0] = jnp.where(cnt_ref[...] < min(SEL_TOPK, n_sel), 1.0, 0.0).astype(BF16)


def _cmp_sel(qkv_t, kcmp, vcmp_t, tb, ovt):
    b, nt, _, _ = qkv_t.shape
    g, r, dh = NSA_KV_HEADS, NSA_GROUP, HEAD_DIM
    t = nt * SEQ_TILE
    n_pad = kcmp.shape[2]
    n_sel = ovt.shape[0]
    tq = SEQ_TILE
    gps = NSA_GROUPS_PER_STEP
    return pl.pallas_call(
        _cmp_sel_kernel,
        grid=(b, g // gps, t // tq),
        in_specs=[
            pl.BlockSpec((1, 1, gps * r * dh, tq), lambda i, j, k: (i, k, j, 0)),
            pl.BlockSpec((1, gps, n_pad, dh), lambda i, j, k: (i, j, 0, 0)),
            pl.BlockSpec((1, gps, dh, n_pad), lambda i, j, k: (i, j, 0, 0)),
            pl.BlockSpec((gps * r, 2 * n_pad, tq), lambda i, j, k: (j, 0, 0)),
            pl.BlockSpec((n_sel, n_pad), lambda i, j, k: (0, 0)),
        ],
        out_specs=[
            pl.BlockSpec((1, gps, r, dh, tq), lambda i, j, k: (i, j, 0, 0, k)),
            pl.BlockSpec((1, gps, n_sel, tq), lambda i, j, k: (i, j, 0, k)),
        ],
        out_shape=[
            jax.ShapeDtypeStruct((b, g, r, dh, t), F32),
            jax.ShapeDtypeStruct((b, g, n_sel, t), BF16),
        ],
        scratch_shapes=[pltpu.VMEM((gps, n_sel, tq), F32), pltpu.VMEM((gps, n_sel, tq), F32)],
        name="cmp_sel",
        compiler_params=_params("parallel", "parallel", "parallel"),
    )(qkv_t, kcmp, vcmp_t, tb, ovt)


def _ones_rows(width):
    return jnp.ones((FEAT_ROWS, width), BF16)


def _with_ones(v_t):
    return jnp.concatenate([v_t, _ones_rows(v_t.shape[1])], axis=0)


def _online_update(m_ref, acc_ref, slot, tiles, first):
    mt = None
    for s, _ in tiles:
        cm = jnp.max(s, axis=0, keepdims=True)
        mt = cm if mt is None else jnp.maximum(mt, cm)
    m_old = None if first else m_ref[slot]
    m_new = mt if first else jnp.maximum(m_old, mt)
    pv = None
    for s, va in tiles:
        d = jnp.dot(va, jnp.exp2(s - m_new).astype(BF16), preferred_element_type=F32)
        pv = d if pv is None else pv + d
    acc_ref[slot] = pv if first else jnp.exp2(m_old - m_new) * acc_ref[slot] + pv
    m_ref[slot] = m_new


def _pipelined_updates(m_ref, acc_ref, jobs, tiles_per_job=1):
    lookahead = max(1, QK_LOOKAHEAD_TILES // tiles_per_job)
    pending = [job[1]() for job in jobs[:lookahead]]
    for i, (slot, _, values_fn, first) in enumerate(jobs):
        current = pending.pop(0)
        if i + lookahead < len(jobs):
            pending.append(jobs[i + lookahead][1]())
        _online_update(m_ref, acc_ref, slot, list(zip(current, values_fn())), first)


def _for_far_tiles(n_far, run):
    n_quads = n_far // 4

    def quad(i, carry):
        run([[4 * i, 4 * i + 1], [4 * i + 2, 4 * i + 3]])
        return carry

    lax.fori_loop(0, n_quads, quad, 0)

    @pl.when((n_far & 2) != 0)
    def _():
        run([[4 * n_quads, 4 * n_quads + 1]])

    @pl.when((n_far & 1) != 0)
    def _():
        run([[n_far - 1]])


def _normalised(acc_ref, slot):
    acc = acc_ref[slot]
    return acc[:HEAD_DIM] / acc[HEAD_DIM:HEAD_DIM + 1]


def _sel_win_kernel(q_ref, ks_ref, vs_ref, kw_ref, vw_ref, sel_ref, oh_ref, *rest):
    nq = NSA_Q_TILES_PER_STEP
    tab_refs, far2_refs = rest[:nq], rest[nq:2 * nq]
    gate_ref, oc_ref, o_ref, ksrm_ref, kwrm_ref, m_ref, acc_ref = rest[2 * nq:]
    step_id = pl.program_id(2)
    tq = SEQ_TILE
    nrep = NSA_GROUP
    gps = NSA_GROUPS_PER_STEP
    dh = HEAD_DIM
    pad = jnp.zeros((KEY_FEATS - dh, tq), BF16)
    base = nq * step_id

    @pl.when(step_id == 0)
    def _():
        def to_rows(c, carry):
            for gg in range(gps):
                ksa = jnp.concatenate([ks_ref[0, c, gg * dh:(gg + 1) * dh, :], oh_ref[c]], axis=0)
                ksrm_ref[gg, c] = ksa.astype(F32).T.astype(BF16)
                kwa = jnp.concatenate([kw_ref[0, c, gg * dh:(gg + 1) * dh, :], pad], axis=0)
                kwrm_ref[gg, c] = kwa.astype(F32).T.astype(BF16)
            return carry

        lax.fori_loop(0, ks_ref.shape[1], to_rows, 0)

    q_sel, q_win, q_none = [], [], []
    for sub in range(nq):
        lanes = slice(sub * tq, (sub + 1) * tq)
        q_sel.append([]), q_win.append([]), q_none.append([])
        for gg in range(gps):
            mq = ((sel_ref[0, gg, :, lanes].astype(F32) - 1.0) * (-NEG_INF)).astype(BF16)
            for r in range(nrep):
                q = q_ref[0, sub, (gg * nrep + r) * dh:(gg * nrep + r + 1) * dh, :]
                q_sel[sub].append(jnp.concatenate([q, mq], axis=0))
                q_win[sub].append(jnp.concatenate([q, pad], axis=0))
                q_none[sub].append(jnp.concatenate([q, jnp.full_like(mq, NEG_INF)], axis=0))

    def jobs(sub, branch, cs, bias=None, first=False, queries=None):
        k_ref, v_ref = (ksrm_ref, vs_ref) if branch == 0 else (kwrm_ref, vw_ref)
        qa = queries if queries is not None else (q_sel if branch == 0 else q_win)[sub]

        def logits(gg, r):
            head = gg * nrep + r
            tiles = [jnp.dot(k_ref[gg, c], qa[head], preferred_element_type=F32) for c in cs]
            return tiles if bias is None else [s + bias(head) for s in tiles]

        def values(gg):
            return [_with_ones(v_ref[0, c, gg * dh:(gg + 1) * dh, :]) for c in cs]

        return [(((sub * gps + gg) * 2 + branch) * nrep + r, functools.partial(logits, gg, r),
                 functools.partial(values, gg), first) for gg in range(gps) for r in range(nrep)]

    near = []
    for sub in range(nq):
        diag = lambda head, sub=sub: tab_refs[sub][0, head, 1]
        near += jobs(sub, 0, [base + sub], diag, first=True) + jobs(sub, 1, [base + sub], diag, first=True)
    for sub in range(nq):
        prev = lambda head, sub=sub: tab_refs[sub][0, head, 0]
        c_prev = jnp.maximum(base + sub - 1, 0)
        near += jobs(sub, 0, [c_prev], prev) + jobs(sub, 1, [c_prev], prev)
    for sub in range(nq):
        edge = lambda head, sub=sub: far2_refs[sub][0]
        near += jobs(sub, 1, [jnp.maximum(base + sub - 2, 0)], edge)
    for sub in range(nq):
        for e in range(sub):
            if e == 0:
                masked = [jnp.where(step_id > 0, a, b) for a, b in zip(q_sel[sub], q_none[sub])]
                near += jobs(sub, 0, [jnp.maximum(base - 1, 0)], queries=masked)
            else:
                near += jobs(sub, 0, [base - 1 + e])
    _pipelined_updates(m_ref, acc_ref, near)

    _for_far_tiles(jnp.maximum(base - 1, 0), lambda groups: _pipelined_updates(
        m_ref, acc_ref, [job for cs in groups for sub in range(nq) for job in jobs(sub, 0, cs)],
        tiles_per_job=len(groups[0])))

    for sub in range(nq):
        lanes = slice(sub * tq, (sub + 1) * tq)
        outs = []
        for gg in range(gps):
            gates = jax.nn.sigmoid(gate_ref[0, gg, :, lanes])
            slot = (sub * gps + gg) * 2 * nrep
            for r in range(nrep):
                outs.append(gates[3 * r:3 * r + 1] * oc_ref[0, gg, r, :, lanes]
                            + gates[3 * r + 1:3 * r + 2] * _normalised(acc_ref, slot + r)
                            + gates[3 * r + 2:3 * r + 3] * _normalised(acc_ref, slot + nrep + r))
        o_ref[0, lanes, :] = jnp.concatenate(outs, axis=0).T.astype(BF16)


def _sel_win(qkv_t, sel_t, onehot_t, tab, far2, gates_t, oc_t):
    b, nt, channels, tq = qkv_t.shape
    g, r, dh = NSA_KV_HEADS, NSA_GROUP, HEAD_DIM
    gps, nq = NSA_GROUPS_PER_STEP, NSA_Q_TILES_PER_STEP
    hd = g * r * dh
    n_sel = sel_t.shape[2]
    kv_block = lambda which: pl.BlockSpec(
        (1, nt, gps * dh, tq), lambda i, j, k: (i, 0, (hd + which * g * dh) // (gps * dh) + j, 0))
    tab_block = lambda sub: pl.BlockSpec(
        (1, gps * r, 2, tq, tq), lambda i, j, k: (jnp.minimum(nq * k + sub, tab.shape[0] - 1), j, 0, 0, 0))
    far2_block = lambda sub: pl.BlockSpec(
        (1, tq, tq), lambda i, j, k: (jnp.minimum(nq * k + sub, far2.shape[0] - 1), 0, 0))
    slots = nq * gps * 2 * r
    return pl.pallas_call(
        _sel_win_kernel,
        grid=(b, g // gps, nt // nq),
        in_specs=[
            pl.BlockSpec((1, nq, gps * r * dh, tq), lambda i, j, k: (i, k, j, 0)),
            kv_block(0), kv_block(1), kv_block(2), kv_block(3),
            pl.BlockSpec((1, gps, n_sel, nq * tq), lambda i, j, k: (i, j, 0, k)),
            pl.BlockSpec((nt, n_sel, tq), lambda i, j, k: (0, 0, 0)),
            *[tab_block(sub) for sub in range(nq)],
            *[far2_block(sub) for sub in range(nq)],
            pl.BlockSpec((1, gps, FEAT_ROWS, nq * tq), lambda i, j, k: (i, j, 0, k)),
            pl.BlockSpec((1, gps, r, dh, nq * tq), lambda i, j, k: (i, j, 0, 0, k)),
        ],
        out_specs=pl.BlockSpec((1, nq * tq, gps * r * dh), lambda i, j, k: (i, k, j)),
        out_shape=jax.ShapeDtypeStruct((b, nt * tq, hd), BF16),
        scratch_shapes=[pltpu.VMEM((gps, nt, tq, KEY_FEATS), BF16), pltpu.VMEM((gps, nt, tq, KEY_FEATS), BF16),
                        pltpu.VMEM((slots, 1, tq), F32), pltpu.VMEM((slots, ACC_ROWS, tq), F32)],
        name="sel_win",
        compiler_params=_params("parallel", "parallel", "arbitrary"),
    )(qkv_t, qkv_t, qkv_t, qkv_t, qkv_t, sel_t, onehot_t, *([tab] * nq), *([far2] * nq), gates_t, oc_t)


N_DECAY_PIECES = 3


def _forget_cumsum_kernel(fl_ref, bf_ref, piece_ref):
    x = fl_ref[0] + bf_ref[...]
    y = jnp.minimum(x, 0.0) - jnp.log1p(jnp.exp(-jnp.abs(x)))
    t = y.shape[1]
    lane = lax.broadcasted_iota(jnp.int32, y.shape, 1)
    k = 1
    while k < t:
        y = y + jnp.where(lane >= k, pltpu.roll(y, k, axis=1), 0.0)
        k *= 2
    for i, piece in enumerate(_split3(-LOG2E * y)):
        piece_ref[0, i] = piece


def _forget_cumsum(fl_t, b_f):
    b, h, t = fl_t.shape
    return pl.pallas_call(
        _forget_cumsum_kernel,
        grid=(b,),
        in_specs=[pl.BlockSpec((1, h, t), lambda i: (i, 0, 0)), pl.BlockSpec((h, 1), lambda i: (0, 0))],
        out_specs=pl.BlockSpec((1, N_DECAY_PIECES, h, t), lambda i: (i, 0, 0, 0)),
        out_shape=jax.ShapeDtypeStruct((b, N_DECAY_PIECES, h, t), BF16),
        name="forget_cumsum",
        compiler_params=_params("parallel"),
    )(fl_t, b_f.reshape(h, 1))


FOX_HEADS_PER_STEP = 8
FOX_Q_TILES_PER_STEP = 4


def _fox_kernel(q_ref, k_ref, v_ref, kf_ref, tri_ref, o_ref, krm_ref, m_ref, acc_ref):
    step_id = pl.program_id(2)
    tq = SEQ_TILE
    hp = FOX_HEADS_PER_STEP
    nq = FOX_Q_TILES_PER_STEP
    n_pieces = N_DECAY_PIECES
    pad = jnp.zeros((KEY_FEATS - HEAD_DIM - FEAT_ROWS, tq), BF16)

    @pl.when(step_id == 0)
    def _():
        def to_rows(c, carry):
            for h in range(hp):
                ka = jnp.concatenate([k_ref[0, c, h * HEAD_DIM:(h + 1) * HEAD_DIM, :], kf_ref[0, c, h], pad], axis=0)
                krm_ref[h, c] = ka.astype(F32).T.astype(BF16)
            return carry

        lax.fori_loop(0, k_ref.shape[1], to_rows, 0)

    q_feat = jnp.where(lax.broadcasted_iota(jnp.int32, (FEAT_ROWS, tq), 0) < n_pieces, 1.0, 0.0).astype(BF16)
    qa = [[jnp.concatenate([q_ref[0, sub, h * HEAD_DIM:(h + 1) * HEAD_DIM, :], q_feat, pad], axis=0)
           for h in range(hp)] for sub in range(nq)]

    def jobs(sub, cs, first=False, causal=False):
        def logits(h):
            tiles = [jnp.dot(krm_ref[h, c], qa[sub][h], preferred_element_type=F32) for c in cs]
            return [s + tri_ref[...] for s in tiles] if causal else tiles

        def values(h):
            return [_with_ones(v_ref[0, c, h * HEAD_DIM:(h + 1) * HEAD_DIM, :]) for c in cs]

        return [(sub * hp + h, functools.partial(logits, h), functools.partial(values, h), first)
                for h in range(hp)]

    base = nq * step_id
    near = [job for sub in range(nq) for job in jobs(sub, [base + sub], first=True, causal=True)]
    near += [job for sub in range(nq) for c in range(sub) for job in jobs(sub, [base + c])]
    _pipelined_updates(m_ref, acc_ref, near)
    _for_far_tiles(base, lambda groups: _pipelined_updates(
        m_ref, acc_ref, [job for cs in groups for sub in range(nq) for job in jobs(sub, cs)],
        tiles_per_job=len(groups[0])))
    for sub in range(nq):
        o_t = jnp.concatenate([_normalised(acc_ref, sub * hp + h) for h in range(hp)], axis=0)
        o_ref[0, sub * tq:(sub + 1) * tq, :] = o_t.T.astype(BF16)


def _fox_attention(q_t, kv_t, kfeat_t, tri):
    b, nt, hd, tq = q_t.shape
    hp, dh, nq = FOX_HEADS_PER_STEP, HEAD_DIM, FOX_Q_TILES_PER_STEP
    n_groups = hd // (hp * dh)
    return pl.pallas_call(
        _fox_kernel,
        grid=(b, n_groups, nt // nq),
        in_specs=[
            pl.BlockSpec((1, nq, hp * dh, tq), lambda i, j, k: (i, k, j, 0)),
            pl.BlockSpec((1, nt, hp * dh, tq), lambda i, j, k: (i, 0, j, 0)),
            pl.BlockSpec((1, nt, hp * dh, tq), lambda i, j, k: (i, 0, n_groups + j, 0)),
            pl.BlockSpec((1, nt, hp, FEAT_ROWS, tq), lambda i, j, k: (i, 0, j, 0, 0)),
            pl.BlockSpec((tq, tq), lambda i, j, k: (0, 0)),
        ],
        out_specs=pl.BlockSpec((1, nq * tq, hp * dh), lambda i, j, k: (i, k, j)),
        out_shape=jax.ShapeDtypeStruct((b, nt * tq, hd), BF16),
        scratch_shapes=[pltpu.VMEM((hp, nt, tq, KEY_FEATS), BF16),
                        pltpu.VMEM((nq * hp, 1, tq), F32), pltpu.VMEM((nq * hp, ACC_ROWS, tq), F32)],
        name="fox_attention",
        compiler_params=_params("parallel", "parallel", "arbitrary"),
    )(q_t, kv_t, kv_t, kfeat_t, tri)


def _t5_bucket_np(dist):
    dist = np.maximum(dist, 0)
    max_exact = N_BUCKETS // 2
    ratio = np.maximum(dist, 1).astype(np.float32) / np.float32(max_exact)
    scaled = np.log(ratio) / np.float32(math.log(MAX_DISTANCE / max_exact))
    large = np.minimum(max_exact + (scaled * np.float32(N_BUCKETS - max_exact)).astype(np.int32), N_BUCKETS - 1)
    return np.where(dist < max_exact, dist, large).astype(np.int32)


_MASKED = MAX_DISTANCE + 1


def _dist_index(dist, valid):
    return np.where(valid, np.minimum(dist, MAX_DISTANCE), _MASKED).astype(np.int32)


def _toeplitz(f_ext, n):
    period = 2 * n
    flat = jnp.tile(f_ext, n)[..., :n * (period - 1)]
    return flat.reshape(f_ext.shape[:-1] + (n, period - 1))[..., :n]


def _key_query_delta(n):
    p = np.arange(2 * n)
    return np.where(p < n, p, p - 2 * n)


def _bias_tables(rel_bias, n_pad):
    bucket = _t5_bucket_np(np.arange(MAX_DISTANCE + 1))
    by_dist = jnp.concatenate([rel_bias[bucket], jnp.full((1, N_HEADS), NEG_INF, F32)], axis=0).T
    far_const = by_dist[:, MAX_DISTANCE:MAX_DISTANCE + 1]

    tq = SEQ_TILE
    u = np.arange(2 * n_pad) - n_pad
    dist = np.arange(tq)[None, :] - CMP_STRIDE * u[:, None] - (CMP_BLOCK - 1)
    varying = np.nonzero((dist.max(axis=1) >= 0) & (dist.min(axis=1) < MAX_DISTANCE))[0]
    lo, hi = int(varying[0]), int(varying[-1]) + 1
    strip = by_dist[:, _dist_index(dist[lo:hi], dist[lo:hi] >= 0)]
    tb = jnp.concatenate([
        jnp.broadcast_to(far_const[:, :, None], (N_HEADS, lo, tq)),
        strip,
        jnp.full((N_HEADS, 2 * n_pad - hi, tq), NEG_INF, F32)], axis=1)

    shifted = jnp.concatenate([by_dist[:, :-1] - far_const, by_dist[:, -1:]], axis=1)
    n = SEQ_TILE
    delta = _key_query_delta(n)
    prev = _toeplitz(shifted[:, _dist_index(delta + n, delta + n >= 0)], n)
    diag = _toeplitz(shifted[:, _dist_index(delta, delta >= 0)], n)
    tab = jnp.stack([prev, diag], axis=1)
    no_prev = jnp.stack([jnp.full_like(prev, NEG_INF), diag], axis=1)
    return LOG2E * tb, LOG2E * jnp.stack([no_prev, tab])


def _static_tiles(seq):
    n = SEQ_TILE
    key = np.arange(n)[:, None]
    query = np.arange(n)[None, :]
    tri = np.where(key <= query, 0.0, NEG_INF).astype(np.float32)
    far2 = np.where(query - key + 2 * n < WINDOW, 0.0, NEG_INF).astype(np.float32)
    assert WINDOW == 2 * n
    far2 = np.stack([np.full_like(far2, NEG_INF)] * 2 + [far2])
    n_sel = seq // SEL_BLOCK
    pos = np.arange(seq).reshape(seq // n, 1, n)
    onehot = (pos // SEL_BLOCK == np.arange(n_sel)[None, :, None]).astype(np.float32)
    return jnp.asarray(tri), jnp.asarray(far2), jnp.asarray(onehot, BF16)


def _overlap_t(n_pad, n_sel):
    n_cmp = n_pad - 1
    cmp_start = np.arange(n_pad) * CMP_STRIDE
    sel_start = np.arange(n_sel) * SEL_BLOCK
    ov = ((cmp_start[None, :] < sel_start[:, None] + SEL_BLOCK)
          & (cmp_start[None, :] + CMP_BLOCK > sel_start[:, None])
          & (np.arange(n_pad)[None, :] < n_cmp))
    return jnp.asarray(ov.astype(np.float32), BF16)


def _nsa_mixer(x, g, w_in, cmp_pos, cmp_w1, cmp_w2, rel_bias, batch, seq):
    n, d = x.shape
    G, R, dh = NSA_KV_HEADS, NSA_GROUP, HEAD_DIM
    hd, kvd = N_HEADS * dh, G * dh
    cols = [hd + i * kvd for i in range(7)]
    wq, wkc, wvc, wks, wvs, wkw, wvw, wgl = jnp.split(w_in, cols, axis=1)
    n_gate = wgl.shape[1]
    wt = jnp.concatenate([wq * Q_PRESCALE, wks, wvs, wkw, wvw], axis=1).T.astype(BF16)
    w2 = jnp.concatenate([wkc, wvc, wgl, jnp.zeros((d, -n_gate % 128), w_in.dtype)], axis=1).astype(BF16)
    qkv_t, f = _norm_proj(x, g, wt, w2, batch, seq)

    gates_t = f[:, 2 * kvd:2 * kvd + n_gate].reshape(batch, seq, G, R * 3).transpose(0, 2, 3, 1)
    gates_t = jnp.pad(gates_t, ((0, 0), (0, 0), (0, FEAT_ROWS - R * 3), (0, 0)))

    nh = seq // CMP_STRIDE
    cmp = _compress(f, cmp_pos.reshape(2, 2, CMP_STRIDE * dh), cmp_w1.astype(BF16), cmp_w2.astype(BF16),
                    batch, seq)
    kcmp = cmp[:, 0]
    vcmp_t = cmp[:, 1].transpose(0, 1, 3, 2)

    n_sel = seq // SEL_BLOCK
    tb, tab = _bias_tables(rel_bias, nh)
    _, far2, onehot_t = _static_tiles(seq)
    oc_t, sel_t = _cmp_sel(qkv_t, kcmp, vcmp_t, tb, _overlap_t(nh, n_sel))
    o = _sel_win(qkv_t, sel_t, onehot_t, tab, far2, gates_t, oc_t)
    return o.reshape(n, hd)


def _shared_kv(x, g, w_kvf, b_f, batch, seq):
    hd = N_HEADS * HEAD_DIM
    wt = w_kvf[:, :2 * hd].T.astype(BF16)
    wf = w_kvf[:, 2 * hd:]
    w2 = jnp.concatenate([wf, jnp.zeros((wf.shape[0], -wf.shape[1] % 128), wf.dtype)], axis=1).astype(BF16)
    kv_t, f = _norm_proj(x, g, wt, w2, batch, seq)
    fl_t = f[:, :N_HEADS].reshape(batch, seq, N_HEADS).transpose(0, 2, 1)
    pieces = _forget_cumsum(fl_t, b_f.astype(F32)).transpose(0, 2, 1, 3)
    pieces = jnp.pad(pieces, ((0, 0), (0, 0), (0, FEAT_ROWS - N_DECAY_PIECES), (0, 0)))
    kfeat_t = pieces.reshape(batch, N_HEADS, FEAT_ROWS, seq // SEQ_TILE, SEQ_TILE).transpose(0, 3, 1, 2, 4)
    return kv_t, kfeat_t


def _fox_mixer(x, g, w_q, kv, batch, seq):
    n, _ = x.shape
    hd = N_HEADS * HEAD_DIM
    (q_t,) = _norm_proj(x, g, (w_q * Q_PRESCALE).T.astype(BF16), None, batch, seq)
    tri, _, _ = _static_tiles(seq)
    o = _fox_attention(q_t, *kv, tri)
    return o.reshape(n, hd)


def kernel(x, norm_g, ffn_w_in, ffn_w_out, nsa_w_in, nsa_cmp_pos, nsa_cmp_w1, nsa_cmp_w2, nsa_w_out, rel_bias,
           kv_norm_g, fox_w_kvf, fox_b_f, fox_w_q, fox_w_out, final_g):
    batch, seq, d = x.shape
    depth = norm_g.shape[0]
    n_a = nsa_w_in.shape[0]
    xf = x.reshape(batch * seq, d)
    w_in = ffn_w_in.astype(BF16)
    w_out = ffn_w_out.astype(BF16)
    kv = None
    for l in range(depth):
        if l == n_a:
            kv = _shared_kv(xf, kv_norm_g, fox_w_kvf, fox_b_f, batch, seq)
        xf = _ffn(xf, norm_g[l, 0], w_in, w_out, (l, 0))
        if l < n_a:
            o = _nsa_mixer(xf, norm_g[l, 1], nsa_w_in[l], nsa_cmp_pos[l], nsa_cmp_w1[l], nsa_cmp_w2[l],
                           rel_bias, batch, seq)
            w_mix = nsa_w_out[l]
        else:
            o = _fox_mixer(xf, norm_g[l, 1], fox_w_q[l - n_a], kv, batch, seq)
            w_mix = fox_w_out[l - n_a]
        xf = _ffn(xf, norm_g[l, 2], w_in, w_out, (l, 1), mixer=(o, w_mix.astype(BF16)),
                  final_g=final_g if l == depth - 1 else None)
    return xf.reshape(batch, seq, d)
```

```python
import functools
import math

import numpy as np
import jax
import jax.numpy as jnp
from jax import lax
from jax.experimental import pallas as pl
from jax.experimental.pallas import tpu as pltpu

N_HEADS = 16
HEAD_DIM = 64
NSA_KV_HEADS = 4
NSA_GROUP = N_HEADS // NSA_KV_HEADS
NSA_GROUPS_PER_STEP = 2
NSA_Q_TILES_PER_STEP = 2
CMP_BLOCK = 32
CMP_STRIDE = 16
SEL_BLOCK = 64
SEL_BLOCK_SHIFT = SEL_BLOCK.bit_length() - 1
assert 1 << SEL_BLOCK_SHIFT == SEL_BLOCK
SEL_TOPK = 16
WINDOW = 512
N_BUCKETS = 32
MAX_DISTANCE = 128
EPS = 1e-6
NEG_INF = -1e30
LOG2E = math.log2(math.e)
Q_PRESCALE = LOG2E * HEAD_DIM ** -0.5
SEL_FORCE = 1e4

F32 = jnp.float32
BF16 = jnp.bfloat16

VMEM_LIMIT_BYTES = 48 * 1024 * 1024

ROW_TILE = 512
FF_TILE = 256
SEQ_TILE = 256
FEAT_ROWS = 16
ACC_ROWS = HEAD_DIM + FEAT_ROWS
QK_LOOKAHEAD_TILES = 6
KEY_FEATS = 128

_NT = (((1,), (1,)), ((), ()))
_TN = (((0,), (0,)), ((), ()))


def _params(*sem):
    return pltpu.CompilerParams(dimension_semantics=sem, vmem_limit_bytes=VMEM_LIMIT_BYTES)


def _rms(x, g):
    return x * lax.rsqrt(jnp.mean(x * x, axis=-1, keepdims=True) + EPS) * g


def _silu(a):
    return a * jax.nn.sigmoid(a)


PROJ_CHUNK = 512


def _project_tiles(y, g_ref, wt_ref, w2_ref, ot_ref, o2_ref):
    channels = ot_ref.shape[2]
    for j in range(ot_ref.shape[1]):
        rows = slice(j * SEQ_TILE, (j + 1) * SEQ_TILE)
        hn = _rms(y[rows, :], g_ref[...]).astype(BF16)
        for c in range(0, channels, PROJ_CHUNK):
            ot_ref[0, j, c:c + PROJ_CHUNK, :] = lax.dot_general(
                wt_ref[c:c + PROJ_CHUNK, :], hn, _NT, preferred_element_type=F32).astype(BF16)
        if w2_ref is not None:
            o2_ref[rows, :] = jnp.dot(hn, w2_ref[...], preferred_element_type=F32)


def _ffn_kernel(x_ref, g_ref, wi_ref, wo_ref, *rest, has_mixer, final_norm, n_proj):
    d_ff = wo_ref.shape[2]
    rest = list(rest)
    mix_ref, wm_ref = (rest.pop(0), rest.pop(0)) if has_mixer else (None, None)
    fg_ref = rest.pop(0) if final_norm else None
    pg_ref, pwt_ref = (rest.pop(0), rest.pop(0)) if n_proj else (None, None)
    pw2_ref = rest.pop(0) if n_proj == 2 else None
    o_ref = rest.pop(0)
    pot_ref = rest.pop(0) if n_proj else None
    po2_ref = rest.pop(0) if n_proj == 2 else None
    x = x_ref[...]
    if has_mixer:
        x = x + jnp.dot(mix_ref[...], wm_ref[...], preferred_element_type=F32)
    hn = _rms(x, g_ref[...]).astype(BF16)
    acc = None
    for c in range(0, d_ff, FF_TILE):
        a = jnp.dot(hn, wi_ref[0, 0, :, c:c + FF_TILE], preferred_element_type=F32)
        b = jnp.dot(hn, wi_ref[0, 0, :, d_ff + c:d_ff + c + FF_TILE], preferred_element_type=F32)
        part = jnp.dot((_silu(a) * b).astype(BF16), wo_ref[0, 0, c:c + FF_TILE, :], preferred_element_type=F32)
        acc = part if acc is None else acc + part
    y = x + 0.5 * acc
    if final_norm:
        y = _rms(y, fg_ref[...])
    o_ref[...] = y
    if n_proj:
        _project_tiles(y, pg_ref, pwt_ref, pw2_ref, pot_ref, po2_ref)


def _resident(shape, index=None):
    index = (0,) * len(shape) if index is None else index
    return pl.BlockSpec(shape, lambda *_: index, pipeline_mode=pl.Buffered(1))


def _ffn(x, g, w_in, w_out, which, batch, seq, mixer=None, final_g=None, proj=None):
    n, d = x.shape
    in_specs = [
        pl.BlockSpec((ROW_TILE, d), lambda i: (i, 0)),
        _resident((1, d)),
        _resident((1, 1) + w_in.shape[2:], which + (0, 0)),
        _resident((1, 1) + w_out.shape[2:], which + (0, 0)),
    ]
    args = [x, g.reshape(1, d), w_in, w_out]
    out_specs = [pl.BlockSpec((ROW_TILE, d), lambda i: (i, 0))]
    out_shape = [jax.ShapeDtypeStruct((n, d), F32)]
    if mixer is not None:
        o, w_mix = mixer
        in_specs += [pl.BlockSpec((ROW_TILE, o.shape[1]), lambda i: (i, 0)), _resident(w_mix.shape)]
        args += [o, w_mix]
    if final_g is not None:
        in_specs.append(_resident((1, d)))
        args.append(final_g.reshape(1, d))
    n_proj = 0
    if proj is not None:
        g2, wt, w2 = proj
        n_proj = 1 if w2 is None else 2
        channels = wt.shape[0]
        per_seq = seq // ROW_TILE
        in_specs += [_resident((1, d)), _resident(wt.shape)]
        args += [g2.reshape(1, d), wt]
        out_specs.append(pl.BlockSpec((1, ROW_TILE // SEQ_TILE, channels, SEQ_TILE),
                                      lambda i: (i // per_seq, i % per_seq, 0, 0)))
        out_shape.append(jax.ShapeDtypeStruct((batch, seq // SEQ_TILE, channels, SEQ_TILE), BF16))
        if w2 is not None:
            in_specs.append(_resident(w2.shape))
            args.append(w2)
            out_specs.append(pl.BlockSpec((ROW_TILE, w2.shape[1]), lambda i: (i, 0)))
            out_shape.append(jax.ShapeDtypeStruct((n, w2.shape[1]), F32))
    return pl.pallas_call(
        functools.partial(_ffn_kernel, has_mixer=mixer is not None, final_norm=final_g is not None, n_proj=n_proj),
        grid=(n // ROW_TILE,),
        in_specs=in_specs,
        out_specs=out_specs,
        out_shape=out_shape,
        name="ffn",
        compiler_params=_params("parallel"),
    )(*args)


def _compress_kernel(*refs):
    *f_refs, pos_ref, w1_ref, w2_ref, o_ref = refs
    nh = f_refs[0].shape[0] // CMP_STRIDE
    dh = HEAD_DIM
    per_chunk = f_refs[0].shape[1] // dh
    for chunk, f_ref in enumerate(f_refs):
        rows = [f_ref[pl.ds(j, nh, stride=CMP_STRIDE), :] for j in range(CMP_STRIDE)]
        for sub in range(per_chunk):
            kv, g = divmod(chunk * per_chunk + sub, NSA_KV_HEADS)
            x = jnp.concatenate([r[:, sub * dh:(sub + 1) * dh] for r in rows], axis=1)
            pos = pos_ref[kv]
            half = x.shape[1]
            top = jnp.dot((x + pos[0:1]).astype(BF16), w1_ref[kv, :half, :], preferred_element_type=F32)
            bot = jnp.dot((x + pos[1:2]).astype(BF16), w1_ref[kv, half:, :], preferred_element_type=F32)
            pre = top + pltpu.roll(bot, nh - 1, axis=0)
            hid = _silu(pre).astype(BF16)
            o_ref[0, kv, g] = jnp.dot(hid, w2_ref[kv], preferred_element_type=F32).astype(BF16)


def _compress(f, pos2, w1, w2, batch, seq):
    g, dh = NSA_KV_HEADS, HEAD_DIM
    nh = seq // CMP_STRIDE
    lanes = 128
    n_chunks = 2 * g * dh // lanes
    return pl.pallas_call(
        _compress_kernel,
        grid=(batch,),
        in_specs=[pl.BlockSpec((seq, lanes), functools.partial(lambda c, i: (i, c), c)) for c in range(n_chunks)] + [
            _resident(pos2.shape),
            _resident(w1.shape),
            _resident(w2.shape),
        ],
        out_specs=pl.BlockSpec((1, 2, g, nh, dh), lambda i: (i, 0, 0, 0, 0)),
        out_shape=jax.ShapeDtypeStruct((batch, 2, g, nh, dh), BF16),
        name="compress",
        compiler_params=_params("parallel"),
    )(*([f] * n_chunks), pos2, w1, w2)


def _split3(x):
    p1 = x.astype(BF16)
    r1 = x - p1.astype(F32)
    p2 = r1.astype(BF16)
    p3 = (r1 - p2.astype(F32)).astype(BF16)
    return p1, p2, p3


def _cmp_block_scores(gg, qi, n_rows, q_ref, kc_ref, vct_ref, tb_ref, ovt_ref, oct_ref):
    tq = SEQ_TILE
    nrep = NSA_GROUP
    dh = HEAD_DIM
    n_cmp_pad = kc_ref.shape[2]
    qt = jnp.concatenate([q_ref[0, 0, (gg * nrep + r) * dh:(gg * nrep + r + 1) * dh, :] for r in range(nrep)],
                         axis=1)
    s = jnp.dot(kc_ref[0, gg, :n_rows, :], qt, preferred_element_type=F32)
    off = pl.multiple_of(n_cmp_pad - (tq // CMP_STRIDE) * qi, 8)
    bias = jnp.concatenate([tb_ref[gg * nrep + r, pl.ds(off, n_rows), :] for r in range(nrep)], axis=1)
    s = s + bias
    m = jnp.max(s, axis=0, keepdims=True)
    e = jnp.exp2(s - m)
    assert tq & (tq - 1) == 0
    t = qi * tq + (lax.broadcasted_iota(jnp.int32, (1, nrep * tq), 1) & (tq - 1))
    p = e * jnp.where(t >= CMP_BLOCK - 1, 1.0 / jnp.sum(e, axis=0, keepdims=True), 0.0)

    oct = jnp.dot(vct_ref[0, gg, :, :n_rows], p.astype(BF16), preferred_element_type=F32)
    for r in range(nrep):
        oct_ref[0, gg, r] = oct[:, r * tq:(r + 1) * tq]

    psum = p[:, 0:tq]
    for r in range(1, nrep):
        psum = psum + p[:, r * tq:(r + 1) * tq]
    ovt = ovt_ref[:, :n_rows]
    imp = sum(jnp.dot(ovt, part, preferred_element_type=F32) for part in _split3(psum))

    n_sel = imp.shape[0]
    blk = lax.broadcasted_iota(jnp.int32, (n_sel, tq), 0)
    cur = jnp.right_shift(qi * tq + lax.broadcasted_iota(jnp.int32, (n_sel, tq), 1), SEL_BLOCK_SHIFT)
    forced = (blk == 0) | (blk == cur) | (blk == cur - 1)
    return jnp.where(blk > cur, -SEL_FORCE, imp + jnp.where(forced, SEL_FORCE, 0.0))


def _cmp_sel_kernel(q_ref, kc_ref, vct_ref, tb_ref, ovt_ref, oct_ref, sel_ref, score_ref, cnt_ref):
    qi = pl.program_id(2)
    tq = SEQ_TILE
    gps, n_cmp_pad = kc_ref.shape[1], kc_ref.shape[2]
    n_sel = score_ref.shape[1]

    half = n_cmp_pad // 2
    last_visible = (qi * tq + tq - CMP_BLOCK) // CMP_STRIDE
    for n_rows, cond in ((half, last_visible < half), (n_cmp_pad, last_visible >= half)):
        @pl.when(cond)
        def _():
            for gg in range(gps):
                score_ref[gg] = _cmp_block_scores(gg, qi, n_rows, q_ref, kc_ref, vct_ref, tb_ref, ovt_ref, oct_ref)

    scores = [score_ref[gg] for gg in range(gps)]

    rows = 8
    n_groups = n_sel // rows
    blk_in_group = lax.broadcasted_iota(jnp.int32, (rows, tq), 0)
    last_group = jnp.right_shift(qi * tq + tq - 1, SEL_BLOCK_SHIFT) // rows
    cnt_ref[...] = jnp.zeros_like(cnt_ref)

    def count(score, k, g):
        sg = score[g * rows:(g + 1) * rows]
        total = None
        for sp in range(k * rows, (k + 1) * rows):
            row = jnp.broadcast_to(score[sp:sp + 1, :], (rows, tq))
            if g > k:
                one = jnp.where(row >= sg, 1.0, 0.0)
            elif g < k:
                one = jnp.where(row > sg, 1.0, 0.0)
            else:
                tie = jnp.where(blk_in_group + g * rows > sp, 1.0, 0.0)
                one = jnp.where(row > sg, 1.0, jnp.where(row == sg, tie, 0.0))
            total = one if total is None else total + one
        return total

    for level in range(n_groups):
        @pl.when(level <= last_group)
        def _():
            for gg, score in enumerate(scores):
                for g in range(level + 1):
                    cnt_ref[gg, g * rows:(g + 1) * rows, :] += count(score, level, g)
                extra = None
                for k in range(level):
                    c = count(score, k, level)
                    extra = c if extra is None else extra + c
                if extra is not None:
                    cnt_ref[gg, level * rows:(level + 1) * rows, :] += extra

    sel_ref[0] = jnp.where(cnt_ref[...] < min(SEL_TOPK, n_sel), 1.0, 0.0).astype(BF16)


def _cmp_sel(qkv_t, kcmp, vcmp_t, tb, ovt):
    b, nt, _, _ = qkv_t.shape
    g, r, dh = NSA_KV_HEADS, NSA_GROUP, HEAD_DIM
    t = nt * SEQ_TILE
    n_pad = kcmp.shape[2]
    n_sel = ovt.shape[0]
    tq = SEQ_TILE
    gps = NSA_GROUPS_PER_STEP
    return pl.pallas_call(
        _cmp_sel_kernel,
        grid=(b, g // gps, t // tq),
        in_specs=[
            pl.BlockSpec((1, 1, gps * r * dh, tq), lambda i, j, k: (i, k, j, 0)),
            pl.BlockSpec((1, gps, n_pad, dh), lambda i, j, k: (i, j, 0, 0)),
            pl.BlockSpec((1, gps, dh, n_pad), lambda i, j, k: (i, j, 0, 0)),
            pl.BlockSpec((gps * r, 2 * n_pad, tq), lambda i, j, k: (j, 0, 0)),
            pl.BlockSpec((n_sel, n_pad), lambda i, j, k: (0, 0)),
        ],
        out_specs=[
            pl.BlockSpec((1, gps, r, dh, tq), lambda i, j, k: (i, j, 0, 0, k)),
            pl.BlockSpec((1, gps, n_sel, tq), lambda i, j, k: (i, j, 0, k)),
        ],
        out_shape=[
            jax.ShapeDtypeStruct((b, g, r, dh, t), F32),
            jax.ShapeDtypeStruct((b, g, n_sel, t), BF16),
        ],
        scratch_shapes=[pltpu.VMEM((gps, n_sel, tq), F32), pltpu.VMEM((gps, n_sel, tq), F32)],
        name="cmp_sel",
        compiler_params=_params("parallel", "parallel", "parallel"),
    )(qkv_t, kcmp, vcmp_t, tb, ovt)


def _ones_rows(width):
    return jnp.ones((FEAT_ROWS, width), BF16)


def _with_ones(v_t):
    return jnp.concatenate([v_t, _ones_rows(v_t.shape[1])], axis=0)


def _online_update(m_ref, acc_ref, slot, tiles, first):
    mt = None
    for s, _ in tiles:
        cm = jnp.max(s, axis=0, keepdims=True)
        mt = cm if mt is None else jnp.maximum(mt, cm)
    m_old = None if first else m_ref[slot]
    m_new = mt if first else jnp.maximum(m_old, mt)
    pv = None
    for s, va in tiles:
        d = jnp.dot(va, jnp.exp2(s - m_new).astype(BF16), preferred_element_type=F32)
        pv = d if pv is None else pv + d
    acc_ref[slot] = pv if first else jnp.exp2(m_old - m_new) * acc_ref[slot] + pv
    m_ref[slot] = m_new


def _pipelined_updates(m_ref, acc_ref, jobs, tiles_per_job=1):
    lookahead = max(1, QK_LOOKAHEAD_TILES // tiles_per_job)
    pending = [job[1]() for job in jobs[:lookahead]]
    for i, (slot, _, values_fn, first) in enumerate(jobs):
        current = pending.pop(0)
        if i + lookahead < len(jobs):
            pending.append(jobs[i + lookahead][1]())
        _online_update(m_ref, acc_ref, slot, list(zip(current, values_fn())), first)


def _for_far_tiles(n_far, run):
    n_quads = n_far // 4

    def quad(i, carry):
        run([[4 * i, 4 * i + 1], [4 * i + 2, 4 * i + 3]])
        return carry

    lax.fori_loop(0, n_quads, quad, 0)

    @pl.when((n_far & 2) != 0)
    def _():
        run([[4 * n_quads, 4 * n_quads + 1]])

    @pl.when((n_far & 1) != 0)
    def _():
        run([[n_far - 1]])


def _normalised(acc_ref, slot):
    acc = acc_ref[slot]
    return acc[:HEAD_DIM] / acc[HEAD_DIM:HEAD_DIM + 1]


def _sel_win_kernel(q_ref, ks_ref, vs_ref, kw_ref, vw_ref, sel_ref, oh_ref, *rest):
    nq = NSA_Q_TILES_PER_STEP
    tab_refs, far2_refs = rest[:nq], rest[nq:2 * nq]
    gate_ref, oc_ref, o_ref, ksrm_ref, kwrm_ref, m_ref, acc_ref = rest[2 * nq:]
    step_id = pl.program_id(2)
    tq = SEQ_TILE
    nrep = NSA_GROUP
    gps = NSA_GROUPS_PER_STEP
    dh = HEAD_DIM
    pad = jnp.zeros((KEY_FEATS - dh, tq), BF16)
    base = nq * step_id

    @pl.when(step_id == 0)
    def _():
        def to_rows(c, carry):
            for gg in range(gps):
                ksa = jnp.concatenate([ks_ref[0, c, gg * dh:(gg + 1) * dh, :], oh_ref[c]], axis=0)
                ksrm_ref[gg, c] = ksa.astype(F32).T.astype(BF16)
                kwa = jnp.concatenate([kw_ref[0, c, gg * dh:(gg + 1) * dh, :], pad], axis=0)
                kwrm_ref[gg, c] = kwa.astype(F32).T.astype(BF16)
            return carry

        lax.fori_loop(0, ks_ref.shape[1], to_rows, 0)

    q_sel, q_win, q_none = [], [], []
    for sub in range(nq):
        lanes = slice(sub * tq, (sub + 1) * tq)
        q_sel.append([]), q_win.append([]), q_none.append([])
        for gg in range(gps):
            mq = ((sel_ref[0, gg, :, lanes].astype(F32) - 1.0) * (-NEG_INF)).astype(BF16)
            for r in range(nrep):
                q = q_ref[0, sub, (gg * nrep + r) * dh:(gg * nrep + r + 1) * dh, :]
                q_sel[sub].append(jnp.concatenate([q, mq], axis=0))
                q_win[sub].append(jnp.concatenate([q, pad], axis=0))
                q_none[sub].append(jnp.concatenate([q, jnp.full_like(mq, NEG_INF)], axis=0))

    def jobs(sub, branch, cs, bias=None, first=False, queries=None):
        k_ref, v_ref = (ksrm_ref, vs_ref) if branch == 0 else (kwrm_ref, vw_ref)
        qa = queries if queries is not None else (q_sel if branch == 0 else q_win)[sub]

        def logits(gg, r):
            head = gg * nrep + r
            tiles = [jnp.dot(k_ref[gg, c], qa[head], preferred_element_type=F32) for c in cs]
            return tiles if bias is None else [s + bias(head) for s in tiles]

        def values(gg):
            return [_with_ones(v_ref[0, c, gg * dh:(gg + 1) * dh, :]) for c in cs]

        return [(((sub * gps + gg) * 2 + branch) * nrep + r, functools.partial(logits, gg, r),
                 functools.partial(values, gg), first) for gg in range(gps) for r in range(nrep)]

    near = []
    for sub in range(nq):
        diag = lambda head, sub=sub: tab_refs[sub][0, head, 1]
        near += jobs(sub, 0, [base + sub], diag, first=True) + jobs(sub, 1, [base + sub], diag, first=True)
    for sub in range(nq):
        prev = lambda head, sub=sub: tab_refs[sub][0, head, 0]
        c_prev = jnp.maximum(base + sub - 1, 0)
        near += jobs(sub, 0, [c_prev], prev) + jobs(sub, 1, [c_prev], prev)
    for sub in range(nq):
        edge = lambda head, sub=sub: far2_refs[sub][0]
        near += jobs(sub, 1, [jnp.maximum(base + sub - 2, 0)], edge)
    for sub in range(nq):
        for e in range(sub):
            if e == 0:
                masked = [jnp.where(step_id > 0, a, b) for a, b in zip(q_sel[sub], q_none[sub])]
                near += jobs(sub, 0, [jnp.maximum(base - 1, 0)], queries=masked)
            else:
                near += jobs(sub, 0, [base - 1 + e])
    _pipelined_updates(m_ref, acc_ref, near)

    _for_far_tiles(jnp.maximum(base - 1, 0), lambda groups: _pipelined_updates(
        m_ref, acc_ref, [job for cs in groups for sub in range(nq) for job in jobs(sub, 0, cs)],
        tiles_per_job=len(groups[0])))

    for sub in range(nq):
        lanes = slice(sub * tq, (sub + 1) * tq)
        outs = []
        for gg in range(gps):
            gates = jax.nn.sigmoid(gate_ref[0, gg, :, lanes])
            slot = (sub * gps + gg) * 2 * nrep
            for r in range(nrep):
                outs.append(gates[3 * r:3 * r + 1] * oc_ref[0, gg, r, :, lanes]
                            + gates[3 * r + 1:3 * r + 2] * _normalised(acc_ref, slot + r)
                            + gates[3 * r + 2:3 * r + 3] * _normalised(acc_ref, slot + nrep + r))
        o_ref[0, lanes, :] = jnp.concatenate(outs, axis=0).T.astype(BF16)


def _sel_win(qkv_t, sel_t, onehot_t, tab, far2, gates_t, oc_t):
    b, nt, channels, tq = qkv_t.shape
    g, r, dh = NSA_KV_HEADS, NSA_GROUP, HEAD_DIM
    gps, nq = NSA_GROUPS_PER_STEP, NSA_Q_TILES_PER_STEP
    hd = g * r * dh
    n_sel = sel_t.shape[2]
    kv_block = lambda which: pl.BlockSpec(
        (1, nt, gps * dh, tq), lambda i, j, k: (i, 0, (hd + which * g * dh) // (gps * dh) + j, 0))
    tab_block = lambda sub: pl.BlockSpec(
        (1, gps * r, 2, tq, tq), lambda i, j, k: (jnp.minimum(nq * k + sub, tab.shape[0] - 1), j, 0, 0, 0))
    far2_block = lambda sub: pl.BlockSpec(
        (1, tq, tq), lambda i, j, k: (jnp.minimum(nq * k + sub, far2.shape[0] - 1), 0, 0))
    slots = nq * gps * 2 * r
    return pl.pallas_call(
        _sel_win_kernel,
        grid=(b, g // gps, nt // nq),
        in_specs=[
            pl.BlockSpec((1, nq, gps * r * dh, tq), lambda i, j, k: (i, k, j, 0)),
            kv_block(0), kv_block(1), kv_block(2), kv_block(3),
            pl.BlockSpec((1, gps, n_sel, nq * tq), lambda i, j, k: (i, j, 0, k)),
            pl.BlockSpec((nt, n_sel, tq), lambda i, j, k: (0, 0, 0)),
            *[tab_block(sub) for sub in range(nq)],
            *[far2_block(sub) for sub in range(nq)],
            pl.BlockSpec((1, gps, FEAT_ROWS, nq * tq), lambda i, j, k: (i, j, 0, k)),
            pl.BlockSpec((1, gps, r, dh, nq * tq), lambda i, j, k: (i, j, 0, 0, k)),
        ],
        out_specs=pl.BlockSpec((1, nq * tq, gps * r * dh), lambda i, j, k: (i, k, j)),
        out_shape=jax.ShapeDtypeStruct((b, nt * tq, hd), BF16),
        scratch_shapes=[pltpu.VMEM((gps, nt, tq, KEY_FEATS), BF16), pltpu.VMEM((gps, nt, tq, KEY_FEATS), BF16),
                        pltpu.VMEM((slots, 1, tq), F32), pltpu.VMEM((slots, ACC_ROWS, tq), F32)],
        name="sel_win",
        compiler_params=_params("parallel", "parallel", "arbitrary"),
    )(qkv_t, qkv_t, qkv_t, qkv_t, qkv_t, sel_t, onehot_t, *([tab] * nq), *([far2] * nq), gates_t, oc_t)


N_DECAY_PIECES = 3


def _forget_cumsum_kernel(fl_ref, bf_ref, piece_ref):
    x = fl_ref[0] + bf_ref[...]
    y = jnp.minimum(x, 0.0) - jnp.log1p(jnp.exp(-jnp.abs(x)))
    t = y.shape[1]
    lane = lax.broadcasted_iota(jnp.int32, y.shape, 1)
    k = 1
    while k < t:
        y = y + jnp.where(lane >= k, pltpu.roll(y, k, axis=1), 0.0)
        k *= 2
    for i, piece in enumerate(_split3(-LOG2E * y)):
        piece_ref[0, i] = piece


def _forget_cumsum(fl_t, b_f):
    b, h, t = fl_t.shape
    return pl.pallas_call(
        _forget_cumsum_kernel,
        grid=(b,),
        in_specs=[pl.BlockSpec((1, h, t), lambda i: (i, 0, 0)), pl.BlockSpec((h, 1), lambda i: (0, 0))],
        out_specs=pl.BlockSpec((1, N_DECAY_PIECES, h, t), lambda i: (i, 0, 0, 0)),
        out_shape=jax.ShapeDtypeStruct((b, N_DECAY_PIECES, h, t), BF16),
        name="forget_cumsum",
        compiler_params=_params("parallel"),
    )(fl_t, b_f.reshape(h, 1))


FOX_HEADS_PER_STEP = 8
FOX_Q_TILES_PER_STEP = 4


def _fox_kernel(q_ref, k_ref, v_ref, kf_ref, tri_ref, o_ref, krm_ref, m_ref, acc_ref):
    step_id = pl.program_id(2)
    tq = SEQ_TILE
    hp = FOX_HEADS_PER_STEP
    nq = FOX_Q_TILES_PER_STEP
    n_pieces = N_DECAY_PIECES
    pad = jnp.zeros((KEY_FEATS - HEAD_DIM - FEAT_ROWS, tq), BF16)

    @pl.when(step_id == 0)
    def _():
        def to_rows(c, carry):
            for h in range(hp):
                ka = jnp.concatenate([k_ref[0, c, h * HEAD_DIM:(h + 1) * HEAD_DIM, :], kf_ref[0, c, h], pad], axis=0)
                krm_ref[h, c] = ka.astype(F32).T.astype(BF16)
            return carry

        lax.fori_loop(0, k_ref.shape[1], to_rows, 0)

    q_feat = jnp.where(lax.broadcasted_iota(jnp.int32, (FEAT_ROWS, tq), 0) < n_pieces, 1.0, 0.0).astype(BF16)
    qa = [[jnp.concatenate([q_ref[0, sub, h * HEAD_DIM:(h + 1) * HEAD_DIM, :], q_feat, pad], axis=0)
           for h in range(hp)] for sub in range(nq)]

    def jobs(sub, cs, first=False, causal=False):
        def logits(h):
            tiles = [jnp.dot(krm_ref[h, c], qa[sub][h], preferred_element_type=F32) for c in cs]
            return [s + tri_ref[...] for s in tiles] if causal else tiles

        def values(h):
            return [_with_ones(v_ref[0, c, h * HEAD_DIM:(h + 1) * HEAD_DIM, :]) for c in cs]

        return [(sub * hp + h, functools.partial(logits, h), functools.partial(values, h), first)
                for h in range(hp)]

    base = nq * step_id
    near = [job for sub in range(nq) for job in jobs(sub, [base + sub], first=True, causal=True)]
    near += [job for sub in range(nq) for c in range(sub) for job in jobs(sub, [base + c])]
    _pipelined_updates(m_ref, acc_ref, near)
    _for_far_tiles(base, lambda groups: _pipelined_updates(
        m_ref, acc_ref, [job for cs in groups for sub in range(nq) for job in jobs(sub, cs)],
        tiles_per_job=len(groups[0])))
    for sub in range(nq):
        o_t = jnp.concatenate([_normalised(acc_ref, sub * hp + h) for h in range(hp)], axis=0)
        o_ref[0, sub * tq:(sub + 1) * tq, :] = o_t.T.astype(BF16)


def _fox_attention(q_t, kv_t, kfeat_t, tri):
    b, nt, hd, tq = q_t.shape
    hp, dh, nq = FOX_HEADS_PER_STEP, HEAD_DIM, FOX_Q_TILES_PER_STEP
    n_groups = hd // (hp * dh)
    return pl.pallas_call(
        _fox_kernel,
        grid=(b, n_groups, nt // nq),
        in_specs=[
            pl.BlockSpec((1, nq, hp * dh, tq), lambda i, j, k: (i, k, j, 0)),
            pl.BlockSpec((1, nt, hp * dh, tq), lambda i, j, k: (i, 0, j, 0)),
            pl.BlockSpec((1, nt, hp * dh, tq), lambda i, j, k: (i, 0, n_groups + j, 0)),
            pl.BlockSpec((1, nt, hp, FEAT_ROWS, tq), lambda i, j, k: (i, 0, j, 0, 0)),
            pl.BlockSpec((tq, tq), lambda i, j, k: (0, 0)),
        ],
        out_specs=pl.BlockSpec((1, nq * tq, hp * dh), lambda i, j, k: (i, k, j)),
        out_shape=jax.ShapeDtypeStruct((b, nt * tq, hd), BF16),
        scratch_shapes=[pltpu.VMEM((hp, nt, tq, KEY_FEATS), BF16),
                        pltpu.VMEM((nq * hp, 1, tq), F32), pltpu.VMEM((nq * hp, ACC_ROWS, tq), F32)],
        name="fox_attention",
        compiler_params=_params("parallel", "parallel", "arbitrary"),
    )(q_t, kv_t, kv_t, kfeat_t, tri)


def _t5_bucket_np(dist):
    dist = np.maximum(dist, 0)
    max_exact = N_BUCKETS // 2
    ratio = np.maximum(dist, 1).astype(np.float32) / np.float32(max_exact)
    scaled = np.log(ratio) / np.float32(math.log(MAX_DISTANCE / max_exact))
    large = np.minimum(max_exact + (scaled * np.float32(N_BUCKETS - max_exact)).astype(np.int32), N_BUCKETS - 1)
    return np.where(dist < max_exact, dist, large).astype(np.int32)


_MASKED = MAX_DISTANCE + 1


def _dist_index(dist, valid):
    return np.where(valid, np.minimum(dist, MAX_DISTANCE), _MASKED).astype(np.int32)


def _toeplitz(f_ext, n):
    period = 2 * n
    flat = jnp.tile(f_ext, n)[..., :n * (period - 1)]
    return flat.reshape(f_ext.shape[:-1] + (n, period - 1))[..., :n]


def _key_query_delta(n):
    p = np.arange(2 * n)
    return np.where(p < n, p, p - 2 * n)


def _bias_tables(rel_bias, n_pad):
    bucket = _t5_bucket_np(np.arange(MAX_DISTANCE + 1))
    by_dist = jnp.concatenate([rel_bias[bucket], jnp.full((1, N_HEADS), NEG_INF, F32)], axis=0).T
    far_const = by_dist[:, MAX_DISTANCE:MAX_DISTANCE + 1]

    tq = SEQ_TILE
    u = np.arange(2 * n_pad) - n_pad
    dist = np.arange(tq)[None, :] - CMP_STRIDE * u[:, None] - (CMP_BLOCK - 1)
    varying = np.nonzero((dist.max(axis=1) >= 0) & (dist.min(axis=1) < MAX_DISTANCE))[0]
    lo, hi = int(varying[0]), int(varying[-1]) + 1
    strip = by_dist[:, _dist_index(dist[lo:hi], dist[lo:hi] >= 0)]
    tb = jnp.concatenate([
        jnp.broadcast_to(far_const[:, :, None], (N_HEADS, lo, tq)),
        strip,
        jnp.full((N_HEADS, 2 * n_pad - hi, tq), NEG_INF, F32)], axis=1)

    shifted = jnp.concatenate([by_dist[:, :-1] - far_const, by_dist[:, -1:]], axis=1)
    n = SEQ_TILE
    delta = _key_query_delta(n)
    prev = _toeplitz(shifted[:, _dist_index(delta + n, delta + n >= 0)], n)
    diag = _toeplitz(shifted[:, _dist_index(delta, delta >= 0)], n)
    tab = jnp.stack([prev, diag], axis=1)
    no_prev = jnp.stack([jnp.full_like(prev, NEG_INF), diag], axis=1)
    return LOG2E * tb, LOG2E * jnp.stack([no_prev, tab])


def _static_tiles(seq):
    n = SEQ_TILE
    key = np.arange(n)[:, None]
    query = np.arange(n)[None, :]
    tri = np.where(key <= query, 0.0, NEG_INF).astype(np.float32)
    far2 = np.where(query - key + 2 * n < WINDOW, 0.0, NEG_INF).astype(np.float32)
    assert WINDOW == 2 * n
    far2 = np.stack([np.full_like(far2, NEG_INF)] * 2 + [far2])
    n_sel = seq // SEL_BLOCK
    pos = np.arange(seq).reshape(seq // n, 1, n)
    onehot = (pos // SEL_BLOCK == np.arange(n_sel)[None, :, None]).astype(np.float32)
    return jnp.asarray(tri), jnp.asarray(far2), jnp.asarray(onehot, BF16)


def _overlap_t(n_pad, n_sel):
    n_cmp = n_pad - 1
    cmp_start = np.arange(n_pad) * CMP_STRIDE
    sel_start = np.arange(n_sel) * SEL_BLOCK
    ov = ((cmp_start[None, :] < sel_start[:, None] + SEL_BLOCK)
          & (cmp_start[None, :] + CMP_BLOCK > sel_start[:, None])
          & (np.arange(n_pad)[None, :] < n_cmp))
    return jnp.asarray(ov.astype(np.float32), BF16)


def _nsa_proj_weights(w_in):
    d = w_in.shape[0]
    hd, kvd = N_HEADS * HEAD_DIM, NSA_KV_HEADS * HEAD_DIM
    wq, wkc, wvc, wks, wvs, wkw, wvw, wgl = jnp.split(w_in, [hd + i * kvd for i in range(7)], axis=1)
    wt = jnp.concatenate([wq * Q_PRESCALE, wks, wvs, wkw, wvw], axis=1).T.astype(BF16)
    w2 = jnp.concatenate([wkc, wvc, wgl, jnp.zeros((d, -wgl.shape[1] % 128), w_in.dtype)], axis=1).astype(BF16)
    return wt, w2


def _nsa_mixer(qkv_t, f, cmp_pos, cmp_w1, cmp_w2, rel_bias, batch, seq):
    G, R, dh = NSA_KV_HEADS, NSA_GROUP, HEAD_DIM
    hd, kvd = N_HEADS * dh, G * dh
    n_gate = N_HEADS * 3
    gates_t = f[:, 2 * kvd:2 * kvd + n_gate].reshape(batch, seq, G, R * 3).transpose(0, 2, 3, 1)
    gates_t = jnp.pad(gates_t, ((0, 0), (0, 0), (0, FEAT_ROWS - R * 3), (0, 0)))

    nh = seq // CMP_STRIDE
    cmp = _compress(f, cmp_pos.reshape(2, 2, CMP_STRIDE * dh), cmp_w1.astype(BF16), cmp_w2.astype(BF16),
                    batch, seq)
    kcmp = cmp[:, 0]
    vcmp_t = cmp[:, 1].transpose(0, 1, 3, 2)

    n_sel = seq // SEL_BLOCK
    tb, tab = _bias_tables(rel_bias, nh)
    _, far2, onehot_t = _static_tiles(seq)
    oc_t, sel_t = _cmp_sel(qkv_t, kcmp, vcmp_t, tb, _overlap_t(nh, n_sel))
    o = _sel_win(qkv_t, sel_t, onehot_t, tab, far2, gates_t, oc_t)
    return o.reshape(batch * seq, hd)


def _kv_proj_weights(w_kvf):
    hd = N_HEADS * HEAD_DIM
    wf = w_kvf[:, 2 * hd:]
    w2 = jnp.concatenate([wf, jnp.zeros((wf.shape[0], -wf.shape[1] % 128), wf.dtype)], axis=1).astype(BF16)
    return w_kvf[:, :2 * hd].T.astype(BF16), w2


def _shared_kv(kv_t, f, b_f, batch, seq):
    fl_t = f[:, :N_HEADS].reshape(batch, seq, N_HEADS).transpose(0, 2, 1)
    pieces = _forget_cumsum(fl_t, b_f.astype(F32)).transpose(0, 2, 1, 3)
    pieces = jnp.pad(pieces, ((0, 0), (0, 0), (0, FEAT_ROWS - N_DECAY_PIECES), (0, 0)))
    kfeat_t = pieces.reshape(batch, N_HEADS, FEAT_ROWS, seq // SEQ_TILE, SEQ_TILE).transpose(0, 3, 1, 2, 4)
    return kv_t, kfeat_t


def _fox_mixer(q_t, kv, batch, seq):
    tri, _, _ = _static_tiles(seq)
    return _fox_attention(q_t, *kv, tri).reshape(batch * seq, N_HEADS * HEAD_DIM)


def kernel(x, norm_g, ffn_w_in, ffn_w_out, nsa_w_in, nsa_cmp_pos, nsa_cmp_w1, nsa_cmp_w2, nsa_w_out, rel_bias,
           kv_norm_g, fox_w_kvf, fox_b_f, fox_w_q, fox_w_out, final_g):
    batch, seq, d = x.shape
    depth = norm_g.shape[0]
    n_a = nsa_w_in.shape[0]
    assert 1 <= n_a <= depth
    xf = x.reshape(batch * seq, d)
    w_in = ffn_w_in.astype(BF16)
    w_out = ffn_w_out.astype(BF16)
    kv = None
    for l in range(depth):
        if l < n_a:
            mixer_proj = (norm_g[l, 1],) + _nsa_proj_weights(nsa_w_in[l])
            xf, qkv_t, f = _ffn(xf, norm_g[l, 0], w_in, w_out, (l, 0), batch, seq, proj=mixer_proj)
            o = _nsa_mixer(qkv_t, f, nsa_cmp_pos[l], nsa_cmp_w1[l], nsa_cmp_w2[l], rel_bias, batch, seq)
            w_mix = nsa_w_out[l]
        else:
            mixer_proj = (norm_g[l, 1], (fox_w_q[l - n_a] * Q_PRESCALE).T.astype(BF16), None)
            xf, q_t = _ffn(xf, norm_g[l, 0], w_in, w_out, (l, 0), batch, seq, proj=mixer_proj)
            o = _fox_mixer(q_t, kv, batch, seq)
            w_mix = fox_w_out[l - n_a]
        kv_proj = (kv_norm_g,) + _kv_proj_weights(fox_w_kvf) if l == n_a - 1 and n_a < depth else None
        outs = _ffn(xf, norm_g[l, 2], w_in, w_out, (l, 1), batch, seq, mixer=(o, w_mix.astype(BF16)),
                    final_g=final_g if l == depth - 1 else None, proj=kv_proj)
        xf = outs[0]
        if kv_proj is not None:
            kv = _shared_kv(outs[1], outs[2], fox_b_f, batch, seq)
    return xf.reshape(batch, seq, d)
```

```python
import functools
import math

import numpy as np
import jax
import jax.numpy as jnp
from jax import lax
from jax.experimental import pallas as pl
from jax.experimental.pallas import tpu as pltpu

N_HEADS = 16
HEAD_DIM = 64
NSA_KV_HEADS = 4
NSA_GROUP = N_HEADS // NSA_KV_HEADS
NSA_GROUPS_PER_STEP = 2
NSA_Q_TILES_PER_STEP = 2
CMP_BLOCK = 32
CMP_STRIDE = 16
SEL_BLOCK = 64
SEL_BLOCK_SHIFT = SEL_BLOCK.bit_length() - 1
assert 1 << SEL_BLOCK_SHIFT == SEL_BLOCK
SEL_TOPK = 16
WINDOW = 512
N_BUCKETS = 32
MAX_DISTANCE = 128
EPS = 1e-6
NEG_INF = -1e30
LOG2E = math.log2(math.e)
Q_PRESCALE = LOG2E * HEAD_DIM ** -0.5
SEL_FORCE = 1e4

F32 = jnp.float32
BF16 = jnp.bfloat16

VMEM_LIMIT_BYTES = 48 * 1024 * 1024

ROW_TILE = 512
FF_TILE = 256
SEQ_TILE = 256
FEAT_ROWS = 16
ACC_ROWS = HEAD_DIM + FEAT_ROWS
QK_LOOKAHEAD_TILES = 6
KEY_FEATS = 128

_NT = (((1,), (1,)), ((), ()))


def _params(*sem):
    return pltpu.CompilerParams(dimension_semantics=sem, vmem_limit_bytes=VMEM_LIMIT_BYTES)


def _rms(x, g):
    return x * lax.rsqrt(jnp.mean(x * x, axis=-1, keepdims=True) + EPS) * g


def _silu(a):
    return a * jax.nn.sigmoid(a)


PROJ_CHUNK = 512


def _project_tiles(y, g_ref, wt_ref, w2_ref, ot_ref, o2_ref):
    channels = ot_ref.shape[2]
    for j in range(ot_ref.shape[1]):
        rows = slice(j * SEQ_TILE, (j + 1) * SEQ_TILE)
        hn = _rms(y[rows, :], g_ref[...]).astype(BF16)
        for c in range(0, channels, PROJ_CHUNK):
            ot_ref[0, j, c:c + PROJ_CHUNK, :] = lax.dot_general(
                wt_ref[c:c + PROJ_CHUNK, :], hn, _NT, preferred_element_type=F32).astype(BF16)
        if w2_ref is not None:
            o2_ref[rows, :] = jnp.dot(hn, w2_ref[...], preferred_element_type=F32)


def _ffn_kernel(x_ref, g_ref, wi_ref, wo_ref, *rest, has_mixer, final_norm, n_proj):
    d_ff = wo_ref.shape[2]
    rest = list(rest)
    mix_ref, wm_ref = (rest.pop(0), rest.pop(0)) if has_mixer else (None, None)
    fg_ref = rest.pop(0) if final_norm else None
    pg_ref, pwt_ref = (rest.pop(0), rest.pop(0)) if n_proj else (None, None)
    pw2_ref = rest.pop(0) if n_proj == 2 else None
    o_ref = rest.pop(0)
    pot_ref = rest.pop(0) if n_proj else None
    po2_ref = rest.pop(0) if n_proj == 2 else None
    x = x_ref[...]
    if has_mixer:
        x = x + jnp.dot(mix_ref[...], wm_ref[...], preferred_element_type=F32)
    hn = _rms(x, g_ref[...]).astype(BF16)
    acc = None
    for c in range(0, d_ff, FF_TILE):
        a = jnp.dot(hn, wi_ref[0, 0, :, c:c + FF_TILE], preferred_element_type=F32)
        b = jnp.dot(hn, wi_ref[0, 0, :, d_ff + c:d_ff + c + FF_TILE], preferred_element_type=F32)
        part = jnp.dot((_silu(a) * b).astype(BF16), wo_ref[0, 0, c:c + FF_TILE, :], preferred_element_type=F32)
        acc = part if acc is None else acc + part
    y = x + 0.5 * acc
    if final_norm:
        y = _rms(y, fg_ref[...])
    o_ref[...] = y
    if n_proj:
        _project_tiles(y, pg_ref, pwt_ref, pw2_ref, pot_ref, po2_ref)


def _resident(shape, index=None):
    index = (0,) * len(shape) if index is None else index
    return pl.BlockSpec(shape, lambda *_: index, pipeline_mode=pl.Buffered(1))


def _ffn(x, g, w_in, w_out, which, batch, seq, mixer=None, final_g=None, proj=None):
    n, d = x.shape
    in_specs = [
        pl.BlockSpec((ROW_TILE, d), lambda i: (i, 0)),
        _resident((1, d)),
        _resident((1, 1) + w_in.shape[2:], which + (0, 0)),
        _resident((1, 1) + w_out.shape[2:], which + (0, 0)),
    ]
    args = [x, g.reshape(1, d), w_in, w_out]
    out_specs = [pl.BlockSpec((ROW_TILE, d), lambda i: (i, 0))]
    out_shape = [jax.ShapeDtypeStruct((n, d), F32)]
    if mixer is not None:
        o, w_mix = mixer
        in_specs += [pl.BlockSpec((ROW_TILE, o.shape[1]), lambda i: (i, 0)), _resident(w_mix.shape)]
        args += [o, w_mix]
    if final_g is not None:
        in_specs.append(_resident((1, d)))
        args.append(final_g.reshape(1, d))
    n_proj = 0
    if proj is not None:
        g2, wt, w2 = proj
        n_proj = 1 if w2 is None else 2
        channels = wt.shape[0]
        per_seq = seq // ROW_TILE
        in_specs += [_resident((1, d)), _resident(wt.shape)]
        args += [g2.reshape(1, d), wt]
        out_specs.append(pl.BlockSpec((1, ROW_TILE // SEQ_TILE, channels, SEQ_TILE),
                                      lambda i: (i // per_seq, i % per_seq, 0, 0)))
        out_shape.append(jax.ShapeDtypeStruct((batch, seq // SEQ_TILE, channels, SEQ_TILE), BF16))
        if w2 is not None:
            in_specs.append(_resident(w2.shape))
            args.append(w2)
            out_specs.append(pl.BlockSpec((ROW_TILE, w2.shape[1]), lambda i: (i, 0)))
            out_shape.append(jax.ShapeDtypeStruct((n, w2.shape[1]), F32))
    return pl.pallas_call(
        functools.partial(_ffn_kernel, has_mixer=mixer is not None, final_norm=final_g is not None, n_proj=n_proj),
        grid=(n // ROW_TILE,),
        in_specs=in_specs,
        out_specs=out_specs,
        out_shape=out_shape,
        name="ffn",
        compiler_params=_params("parallel"),
    )(*args)


def _compress_kernel(*refs):
    *f_refs, pos_ref, w1_ref, w2_ref, o_ref = refs
    nh = f_refs[0].shape[0] // CMP_STRIDE
    dh = HEAD_DIM
    per_chunk = f_refs[0].shape[1] // dh
    for chunk, f_ref in enumerate(f_refs):
        rows = [f_ref[pl.ds(j, nh, stride=CMP_STRIDE), :] for j in range(CMP_STRIDE)]
        for sub in range(per_chunk):
            kv, g = divmod(chunk * per_chunk + sub, NSA_KV_HEADS)
            x = jnp.concatenate([r[:, sub * dh:(sub + 1) * dh] for r in rows], axis=1)
            pos = pos_ref[kv]
            half = x.shape[1]
            top = jnp.dot((x + pos[0:1]).astype(BF16), w1_ref[kv, :half, :], preferred_element_type=F32)
            bot = jnp.dot((x + pos[1:2]).astype(BF16), w1_ref[kv, half:, :], preferred_element_type=F32)
            pre = top + pltpu.roll(bot, nh - 1, axis=0)
            hid = _silu(pre).astype(BF16)
            o_ref[0, kv, g] = jnp.dot(hid, w2_ref[kv], preferred_element_type=F32).astype(BF16)


def _compress(f, pos2, w1, w2, batch, seq):
    g, dh = NSA_KV_HEADS, HEAD_DIM
    nh = seq // CMP_STRIDE
    lanes = 128
    n_chunks = 2 * g * dh // lanes
    return pl.pallas_call(
        _compress_kernel,
        grid=(batch,),
        in_specs=[pl.BlockSpec((seq, lanes), functools.partial(lambda c, i: (i, c), c)) for c in range(n_chunks)] + [
            _resident(pos2.shape),
            _resident(w1.shape),
            _resident(w2.shape),
        ],
        out_specs=pl.BlockSpec((1, 2, g, nh, dh), lambda i: (i, 0, 0, 0, 0)),
        out_shape=jax.ShapeDtypeStruct((batch, 2, g, nh, dh), BF16),
        name="compress",
        compiler_params=_params("parallel"),
    )(*([f] * n_chunks), pos2, w1, w2)


def _split3(x):
    p1 = x.astype(BF16)
    r1 = x - p1.astype(F32)
    p2 = r1.astype(BF16)
    p3 = (r1 - p2.astype(F32)).astype(BF16)
    return p1, p2, p3


def _cmp_block_scores(gg, qi, n_rows, q_ref, kc_ref, vct_ref, tb_ref, ovt_ref, oct_ref):
    tq = SEQ_TILE
    nrep = NSA_GROUP
    dh = HEAD_DIM
    n_cmp_pad = kc_ref.shape[2]
    qt = jnp.concatenate([q_ref[0, 0, (gg * nrep + r) * dh:(gg * nrep + r + 1) * dh, :] for r in range(nrep)],
                         axis=1)
    s = jnp.dot(kc_ref[0, gg, :n_rows, :], qt, preferred_element_type=F32)
    off = pl.multiple_of(n_cmp_pad - (tq // CMP_STRIDE) * qi, 8)
    bias = jnp.concatenate([tb_ref[gg * nrep + r, pl.ds(off, n_rows), :] for r in range(nrep)], axis=1)
    s = s + bias
    m = jnp.max(s, axis=0, keepdims=True)
    e = jnp.exp2(s - m)
    assert tq & (tq - 1) == 0
    t = qi * tq + (lax.broadcasted_iota(jnp.int32, (1, nrep * tq), 1) & (tq - 1))
    p = e * jnp.where(t >= CMP_BLOCK - 1, 1.0 / jnp.sum(e, axis=0, keepdims=True), 0.0)

    oct = jnp.dot(vct_ref[0, gg, :, :n_rows], p.astype(BF16), preferred_element_type=F32)
    for r in range(nrep):
        oct_ref[0, gg, r] = oct[:, r * tq:(r + 1) * tq]

    psum = p[:, 0:tq]
    for r in range(1, nrep):
        psum = psum + p[:, r * tq:(r + 1) * tq]
    ovt = ovt_ref[:, :n_rows]
    imp = sum(jnp.dot(ovt, part, preferred_element_type=F32) for part in _split3(psum))

    n_sel = imp.shape[0]
    blk = lax.broadcasted_iota(jnp.int32, (n_sel, tq), 0)
    cur = jnp.right_shift(qi * tq + lax.broadcasted_iota(jnp.int32, (n_sel, tq), 1), SEL_BLOCK_SHIFT)
    forced = (blk == 0) | (blk == cur) | (blk == cur - 1)
    return jnp.where(blk > cur, -SEL_FORCE, imp + jnp.where(forced, SEL_FORCE, 0.0))


def _cmp_sel_kernel(q_ref, kc_ref, vct_ref, tb_ref, ovt_ref, oct_ref, sel_ref, score_ref, cnt_ref):
    qi = pl.program_id(2)
    tq = SEQ_TILE
    gps, n_cmp_pad = kc_ref.shape[1], kc_ref.shape[2]
    n_sel = score_ref.shape[1]

    half = n_cmp_pad // 2
    last_visible = (qi * tq + tq - CMP_BLOCK) // CMP_STRIDE
    for n_rows, cond in ((half, last_visible < half), (n_cmp_pad, last_visible >= half)):
        @pl.when(cond)
        def _():
            for gg in range(gps):
                score_ref[gg] = _cmp_block_scores(gg, qi, n_rows, q_ref, kc_ref, vct_ref, tb_ref, ovt_ref, oct_ref)

    scores = [score_ref[gg] for gg in range(gps)]

    rows = 8
    n_groups = n_sel // rows
    blk_in_group = lax.broadcasted_iota(jnp.int32, (rows, tq), 0)
    last_group = jnp.right_shift(qi * tq + tq - 1, SEL_BLOCK_SHIFT) // rows
    cnt_ref[...] = jnp.zeros_like(cnt_ref)

    def count(score, k, g):
        sg = score[g * rows:(g + 1) * rows]
        total = None
        for sp in range(k * rows, (k + 1) * rows):
            row = jnp.broadcast_to(score[sp:sp + 1, :], (rows, tq))
            if g > k:
                one = jnp.where(row >= sg, 1.0, 0.0)
            elif g < k:
                one = jnp.where(row > sg, 1.0, 0.0)
            else:
                tie = jnp.where(blk_in_group + g * rows > sp, 1.0, 0.0)
                one = jnp.where(row > sg, 1.0, jnp.where(row == sg, tie, 0.0))
            total = one if total is None else total + one
        return total

    for level in range(n_groups):
        @pl.when(level <= last_group)
        def _():
            for gg, score in enumerate(scores):
                for g in range(level + 1):
                    cnt_ref[gg, g * rows:(g + 1) * rows, :] += count(score, level, g)
                extra = None
                for k in range(level):
                    c = count(score, k, level)
                    extra = c if extra is None else extra + c
                if extra is not None:
                    cnt_ref[gg, level * rows:(level + 1) * rows, :] += extra

    sel_ref[0] = jnp.where(cnt_ref[...] < min(SEL_TOPK, n_sel), 1.0, 0.0).astype(BF16)


def _cmp_sel(qkv_t, kcmp, vcmp_t, tb, ovt):
    b, nt, _, _ = qkv_t.shape
    g, r, dh = NSA_KV_HEADS, NSA_GROUP, HEAD_DIM
    t = nt * SEQ_TILE
    n_pad = kcmp.shape[2]
    n_sel = ovt.shape[0]
    tq = SEQ_TILE
    gps = NSA_GROUPS_PER_STEP
    return pl.pallas_call(
        _cmp_sel_kernel,
        grid=(b, g // gps, t // tq),
        in_specs=[
            pl.BlockSpec((1, 1, gps * r * dh, tq), lambda i, j, k: (i, k, j, 0)),
            pl.BlockSpec((1, gps, n_pad, dh), lambda i, j, k: (i, j, 0, 0)),
            pl.BlockSpec((1, gps, dh, n_pad), lambda i, j, k: (i, j, 0, 0)),
            pl.BlockSpec((gps * r, 2 * n_pad, tq), lambda i, j, k: (j, 0, 0)),
            pl.BlockSpec((n_sel, n_pad), lambda i, j, k: (0, 0)),
        ],
        out_specs=[
            pl.BlockSpec((1, gps, r, dh, tq), lambda i, j, k: (i, j, 0, 0, k)),
            pl.BlockSpec((1, gps, n_sel, tq), lambda i, j, k: (i, j, 0, k)),
        ],
        out_shape=[
            jax.ShapeDtypeStruct((b, g, r, dh, t), F32),
            jax.ShapeDtypeStruct((b, g, n_sel, t), BF16),
        ],
        scratch_shapes=[pltpu.VMEM((gps, n_sel, tq), F32), pltpu.VMEM((gps, n_sel, tq), F32)],
        name="cmp_sel",
        compiler_params=_params("parallel", "parallel", "parallel"),
    )(qkv_t, kcmp, vcmp_t, tb, ovt)


def _ones_rows(width):
    return jnp.ones((FEAT_ROWS, width), BF16)


def _with_ones(v_t):
    return jnp.concatenate([v_t, _ones_rows(v_t.shape[1])], axis=0)


def _online_update(m_ref, acc_ref, slot, tiles, first):
    mt = None
    for s, _ in tiles:
        cm = jnp.max(s, axis=0, keepdims=True)
        mt = cm if mt is None else jnp.maximum(mt, cm)
    m_old = None if first else m_ref[slot]
    m_new = mt if first else jnp.maximum(m_old, mt)
    pv = None
    for s, va in tiles:
        d = jnp.dot(va, jnp.exp2(s - m_new).astype(BF16), preferred_element_type=F32)
        pv = d if pv is None else pv + d
    acc_ref[slot] = pv if first else jnp.exp2(m_old - m_new) * acc_ref[slot] + pv
    m_ref[slot] = m_new


def _pipelined_updates(m_ref, acc_ref, jobs, tiles_per_job=1):
    lookahead = max(1, QK_LOOKAHEAD_TILES // tiles_per_job)
    pending = [job[1]() for job in jobs[:lookahead]]
    for i, (slot, _, values_fn, first) in enumerate(jobs):
        current = pending.pop(0)
        if i + lookahead < len(jobs):
            pending.append(jobs[i + lookahead][1]())
        _online_update(m_ref, acc_ref, slot, list(zip(current, values_fn())), first)


def _for_far_tiles(n_far, run):
    n_quads = n_far // 4

    def quad(i, carry):
        run([[4 * i, 4 * i + 1], [4 * i + 2, 4 * i + 3]])
        return carry

    lax.fori_loop(0, n_quads, quad, 0)

    @pl.when((n_far & 2) != 0)
    def _():
        run([[4 * n_quads, 4 * n_quads + 1]])

    @pl.when((n_far & 1) != 0)
    def _():
        run([[n_far - 1]])


def _normalised(acc_ref, slot):
    acc = acc_ref[slot]
    return acc[:HEAD_DIM] / acc[HEAD_DIM:HEAD_DIM + 1]


def _sel_win_kernel(q_ref, ks_ref, vs_ref, kw_ref, vw_ref, sel_ref, oh_ref, *rest):
    nq = NSA_Q_TILES_PER_STEP
    tab_refs, far2_refs = rest[:nq], rest[nq:2 * nq]
    gate_ref, oc_ref, o_ref, ksrm_ref, kwrm_ref, m_ref, acc_ref = rest[2 * nq:]
    step_id = pl.program_id(2)
    tq = SEQ_TILE
    nrep = NSA_GROUP
    gps = NSA_GROUPS_PER_STEP
    dh = HEAD_DIM
    pad = jnp.zeros((KEY_FEATS - dh, tq), BF16)
    base = nq * step_id

    @pl.when(step_id == 0)
    def _():
        def to_rows(c, carry):
            for gg in range(gps):
                ksa = jnp.concatenate([ks_ref[0, c, gg * dh:(gg + 1) * dh, :], oh_ref[c]], axis=0)
                ksrm_ref[gg, c] = ksa.astype(F32).T.astype(BF16)
                kwa = jnp.concatenate([kw_ref[0, c, gg * dh:(gg + 1) * dh, :], pad], axis=0)
                kwrm_ref[gg, c] = kwa.astype(F32).T.astype(BF16)
            return carry

        lax.fori_loop(0, ks_ref.shape[1], to_rows, 0)

    q_sel, q_win, q_none = [], [], []
    for sub in range(nq):
        lanes = slice(sub * tq, (sub + 1) * tq)
        q_sel.append([]), q_win.append([]), q_none.append([])
        for gg in range(gps):
            mq = ((sel_ref[0, gg, :, lanes].astype(F32) - 1.0) * (-NEG_INF)).astype(BF16)
            for r in range(nrep):
                q = q_ref[0, sub, (gg * nrep + r) * dh:(gg * nrep + r + 1) * dh, :]
                q_sel[sub].append(jnp.concatenate([q, mq], axis=0))
                q_win[sub].append(jnp.concatenate([q, pad], axis=0))
                q_none[sub].append(jnp.concatenate([q, jnp.full_like(mq, NEG_INF)], axis=0))

    def jobs(sub, branch, cs, bias=None, first=False, queries=None):
        k_ref, v_ref = (ksrm_ref, vs_ref) if branch == 0 else (kwrm_ref, vw_ref)
        qa = queries if queries is not None else (q_sel if branch == 0 else q_win)[sub]

        def logits(gg, r):
            head = gg * nrep + r
            tiles = [jnp.dot(k_ref[gg, c], qa[head], preferred_element_type=F32) for c in cs]
            return tiles if bias is None else [s + bias(head) for s in tiles]

        def values(gg):
            return [_with_ones(v_ref[0, c, gg * dh:(gg + 1) * dh, :]) for c in cs]

        return [(((sub * gps + gg) * 2 + branch) * nrep + r, functools.partial(logits, gg, r),
                 functools.partial(values, gg), first) for gg in range(gps) for r in range(nrep)]

    near = []
    for sub in range(nq):
        diag = lambda head, sub=sub: tab_refs[sub][0, head, 1]
        near += jobs(sub, 0, [base + sub], diag, first=True) + jobs(sub, 1, [base + sub], diag, first=True)
    for sub in range(nq):
        prev = lambda head, sub=sub: tab_refs[sub][0, head, 0]
        c_prev = jnp.maximum(base + sub - 1, 0)
        near += jobs(sub, 0, [c_prev], prev) + jobs(sub, 1, [c_prev], prev)
    for sub in range(nq):
        edge = lambda head, sub=sub: far2_refs[sub][0]
        near += jobs(sub, 1, [jnp.maximum(base + sub - 2, 0)], edge)
    for sub in range(nq):
        for e in range(sub):
            if e == 0:
                masked = [jnp.where(step_id > 0, a, b) for a, b in zip(q_sel[sub], q_none[sub])]
                near += jobs(sub, 0, [jnp.maximum(base - 1, 0)], queries=masked)
            else:
                near += jobs(sub, 0, [base - 1 + e])
    _pipelined_updates(m_ref, acc_ref, near)

    _for_far_tiles(jnp.maximum(base - 1, 0), lambda groups: _pipelined_updates(
        m_ref, acc_ref, [job for cs in groups for sub in range(nq) for job in jobs(sub, 0, cs)],
        tiles_per_job=len(groups[0])))

    for sub in range(nq):
        lanes = slice(sub * tq, (sub + 1) * tq)
        outs = []
        for gg in range(gps):
            gates = jax.nn.sigmoid(gate_ref[0, gg, :, lanes])
            slot = (sub * gps + gg) * 2 * nrep
            for r in range(nrep):
                outs.append(gates[3 * r:3 * r + 1] * oc_ref[0, gg, r, :, lanes]
                            + gates[3 * r + 1:3 * r + 2] * _normalised(acc_ref, slot + r)
                            + gates[3 * r + 2:3 * r + 3] * _normalised(acc_ref, slot + nrep + r))
        o_ref[0, lanes, :] = jnp.concatenate(outs, axis=0).T.astype(BF16)


def _sel_win(qkv_t, sel_t, onehot_t, tab, far2, gates_t, oc_t):
    b, nt, channels, tq = qkv_t.shape
    g, r, dh = NSA_KV_HEADS, NSA_GROUP, HEAD_DIM
    gps, nq = NSA_GROUPS_PER_STEP, NSA_Q_TILES_PER_STEP
    hd = g * r * dh
    n_sel = sel_t.shape[2]
    kv_block = lambda which: pl.BlockSpec(
        (1, nt, gps * dh, tq), lambda i, j, k: (i, 0, (hd + which * g * dh) // (gps * dh) + j, 0))
    tab_block = lambda sub: pl.BlockSpec(
        (1, gps * r, 2, tq, tq), lambda i, j, k: (jnp.minimum(nq * k + sub, tab.shape[0] - 1), j, 0, 0, 0))
    far2_block = lambda sub: pl.BlockSpec(
        (1, tq, tq), lambda i, j, k: (jnp.minimum(nq * k + sub, far2.shape[0] - 1), 0, 0))
    slots = nq * gps * 2 * r
    return pl.pallas_call(
        _sel_win_kernel,
        grid=(b, g // gps, nt // nq),
        in_specs=[
            pl.BlockSpec((1, nq, gps * r * dh, tq), lambda i, j, k: (i, k, j, 0)),
            kv_block(0), kv_block(1), kv_block(2), kv_block(3),
            pl.BlockSpec((1, gps, n_sel, nq * tq), lambda i, j, k: (i, j, 0, k)),
            pl.BlockSpec((nt, n_sel, tq), lambda i, j, k: (0, 0, 0)),
            *[tab_block(sub) for sub in range(nq)],
            *[far2_block(sub) for sub in range(nq)],
            pl.BlockSpec((1, gps, FEAT_ROWS, nq * tq), lambda i, j, k: (i, j, 0, k)),
            pl.BlockSpec((1, gps, r, dh, nq * tq), lambda i, j, k: (i, j, 0, 0, k)),
        ],
        out_specs=pl.BlockSpec((1, nq * tq, gps * r * dh), lambda i, j, k: (i, k, j)),
        out_shape=jax.ShapeDtypeStruct((b, nt * tq, hd), BF16),
        scratch_shapes=[pltpu.VMEM((gps, nt, tq, KEY_FEATS), BF16), pltpu.VMEM((gps, nt, tq, KEY_FEATS), BF16),
                        pltpu.VMEM((slots, 1, tq), F32), pltpu.VMEM((slots, ACC_ROWS, tq), F32)],
        name="sel_win",
        compiler_params=_params("parallel", "parallel", "arbitrary"),
    )(qkv_t, qkv_t, qkv_t, qkv_t, qkv_t, sel_t, onehot_t, *([tab] * nq), *([far2] * nq), gates_t, oc_t)


N_DECAY_PIECES = 3


def _forget_cumsum_kernel(fl_ref, bf_ref, piece_ref):
    x = fl_ref[0] + bf_ref[...]
    y = jnp.minimum(x, 0.0) - jnp.log1p(jnp.exp(-jnp.abs(x)))
    t = y.shape[1]
    lane = lax.broadcasted_iota(jnp.int32, y.shape, 1)
    k = 1
    while k < t:
        y = y + jnp.where(lane >= k, pltpu.roll(y, k, axis=1), 0.0)
        k *= 2
    for i, piece in enumerate(_split3(-LOG2E * y)):
        piece_ref[0, i] = piece


def _forget_cumsum(fl_t, b_f):
    b, h, t = fl_t.shape
    return pl.pallas_call(
        _forget_cumsum_kernel,
        grid=(b,),
        in_specs=[pl.BlockSpec((1, h, t), lambda i: (i, 0, 0)), pl.BlockSpec((h, 1), lambda i: (0, 0))],
        out_specs=pl.BlockSpec((1, N_DECAY_PIECES, h, t), lambda i: (i, 0, 0, 0)),
        out_shape=jax.ShapeDtypeStruct((b, N_DECAY_PIECES, h, t), BF16),
        name="forget_cumsum",
        compiler_params=_params("parallel"),
    )(fl_t, b_f.reshape(h, 1))


FOX_HEADS_PER_STEP = 8
FOX_Q_TILES_PER_STEP = 4


def _fox_kernel(q_ref, k_ref, v_ref, kf_ref, tri_ref, o_ref, krm_ref, m_ref, acc_ref):
    step_id = pl.program_id(2)
    tq = SEQ_TILE
    hp = FOX_HEADS_PER_STEP
    nq = FOX_Q_TILES_PER_STEP
    n_pieces = N_DECAY_PIECES
    pad = jnp.zeros((KEY_FEATS - HEAD_DIM - FEAT_ROWS, tq), BF16)

    @pl.when(step_id == 0)
    def _():
        def to_rows(c, carry):
            for h in range(hp):
                ka = jnp.concatenate([k_ref[0, c, h * HEAD_DIM:(h + 1) * HEAD_DIM, :], kf_ref[0, c, h], pad], axis=0)
                krm_ref[h, c] = ka.astype(F32).T.astype(BF16)
            return carry

        lax.fori_loop(0, k_ref.shape[1], to_rows, 0)

    q_feat = jnp.where(lax.broadcasted_iota(jnp.int32, (FEAT_ROWS, tq), 0) < n_pieces, 1.0, 0.0).astype(BF16)
    qa = [[jnp.concatenate([q_ref[0, sub, h * HEAD_DIM:(h + 1) * HEAD_DIM, :], q_feat, pad], axis=0)
           for h in range(hp)] for sub in range(nq)]

    def jobs(sub, cs, first=False, causal=False):
        def logits(h):
            tiles = [jnp.dot(krm_ref[h, c], qa[sub][h], preferred_element_type=F32) for c in cs]
            return [s + tri_ref[...] for s in tiles] if causal else tiles

        def values(h):
            return [_with_ones(v_ref[0, c, h * HEAD_DIM:(h + 1) * HEAD_DIM, :]) for c in cs]

        return [(sub * hp + h, functools.partial(logits, h), functools.partial(values, h), first)
                for h in range(hp)]

    base = nq * step_id
    near = [job for sub in range(nq) for job in jobs(sub, [base + sub], first=True, causal=True)]
    near += [job for sub in range(nq) for c in range(sub) for job in jobs(sub, [base + c])]
    _pipelined_updates(m_ref, acc_ref, near)
    _for_far_tiles(base, lambda groups: _pipelined_updates(
        m_ref, acc_ref, [job for cs in groups for sub in range(nq) for job in jobs(sub, cs)],
        tiles_per_job=len(groups[0])))
    for sub in range(nq):
        o_t = jnp.concatenate([_normalised(acc_ref, sub * hp + h) for h in range(hp)], axis=0)
        o_ref[0, sub * tq:(sub + 1) * tq, :] = o_t.T.astype(BF16)


def _fox_attention(q_t, kv_t, kfeat_t, tri):
    b, nt, hd, tq = q_t.shape
    hp, dh, nq = FOX_HEADS_PER_STEP, HEAD_DIM, FOX_Q_TILES_PER_STEP
    n_groups = hd // (hp * dh)
    return pl.pallas_call(
        _fox_kernel,
        grid=(b, n_groups, nt // nq),
        in_specs=[
            pl.BlockSpec((1, nq, hp * dh, tq), lambda i, j, k: (i, k, j, 0)),
            pl.BlockSpec((1, nt, hp * dh, tq), lambda i, j, k: (i, 0, j, 0)),
            pl.BlockSpec((1, nt, hp * dh, tq), lambda i, j, k: (i, 0, n_groups + j, 0)),
            pl.BlockSpec((1, nt, hp, FEAT_ROWS, tq), lambda i, j, k: (i, 0, j, 0, 0)),
            pl.BlockSpec((tq, tq), lambda i, j, k: (0, 0)),
        ],
        out_specs=pl.BlockSpec((1, nq * tq, hp * dh), lambda i, j, k: (i, k, j)),
        out_shape=jax.ShapeDtypeStruct((b, nt * tq, hd), BF16),
        scratch_shapes=[pltpu.VMEM((hp, nt, tq, KEY_FEATS), BF16),
                        pltpu.VMEM((nq * hp, 1, tq), F32), pltpu.VMEM((nq * hp, ACC_ROWS, tq), F32)],
        name="fox_attention",
        compiler_params=_params("parallel", "parallel", "arbitrary"),
    )(q_t, kv_t, kv_t, kfeat_t, tri)


def _t5_bucket_np(dist):
    dist = np.maximum(dist, 0)
    max_exact = N_BUCKETS // 2
    ratio = np.maximum(dist, 1).astype(np.float32) / np.float32(max_exact)
    scaled = np.log(ratio) / np.float32(math.log(MAX_DISTANCE / max_exact))
    large = np.minimum(max_exact + (scaled * np.float32(N_BUCKETS - max_exact)).astype(np.int32), N_BUCKETS - 1)
    return np.where(dist < max_exact, dist, large).astype(np.int32)


_MASKED = MAX_DISTANCE + 1


def _dist_index(dist, valid):
    return np.where(valid, np.minimum(dist, MAX_DISTANCE), _MASKED).astype(np.int32)


def _toeplitz(f_ext, n):
    period = 2 * n
    flat = jnp.tile(f_ext, n)[..., :n * (period - 1)]
    return flat.reshape(f_ext.shape[:-1] + (n, period - 1))[..., :n]


def _key_query_delta(n):
    p = np.arange(2 * n)
    return np.where(p < n, p, p - 2 * n)


def _bias_tables(rel_bias, n_pad):
    bucket = _t5_bucket_np(np.arange(MAX_DISTANCE + 1))
    by_dist = jnp.concatenate([rel_bias[bucket], jnp.full((1, N_HEADS), NEG_INF, F32)], axis=0).T
    far_const = by_dist[:, MAX_DISTANCE:MAX_DISTANCE + 1]

    tq = SEQ_TILE
    u = np.arange(2 * n_pad) - n_pad
    dist = np.arange(tq)[None, :] - CMP_STRIDE * u[:, None] - (CMP_BLOCK - 1)
    varying = np.nonzero((dist.max(axis=1) >= 0) & (dist.min(axis=1) < MAX_DISTANCE))[0]
    lo, hi = int(varying[0]), int(varying[-1]) + 1
    strip = by_dist[:, _dist_index(dist[lo:hi], dist[lo:hi] >= 0)]
    tb = jnp.concatenate([
        jnp.broadcast_to(far_const[:, :, None], (N_HEADS, lo, tq)),
        strip,
        jnp.full((N_HEADS, 2 * n_pad - hi, tq), NEG_INF, F32)], axis=1)

    shifted = jnp.concatenate([by_dist[:, :-1] - far_const, by_dist[:, -1:]], axis=1)
    n = SEQ_TILE
    delta = _key_query_delta(n)
    prev = _toeplitz(shifted[:, _dist_index(delta + n, delta + n >= 0)], n)
    diag = _toeplitz(shifted[:, _dist_index(delta, delta >= 0)], n)
    tab = jnp.stack([prev, diag], axis=1)
    no_prev = jnp.stack([jnp.full_like(prev, NEG_INF), diag], axis=1)
    return LOG2E * tb, LOG2E * jnp.stack([no_prev, tab])


def _static_tiles(seq):
    n = SEQ_TILE
    key = np.arange(n)[:, None]
    query = np.arange(n)[None, :]
    tri = np.where(key <= query, 0.0, NEG_INF).astype(np.float32)
    far2 = np.where(query - key + 2 * n < WINDOW, 0.0, NEG_INF).astype(np.float32)
    assert WINDOW == 2 * n
    far2 = np.stack([np.full_like(far2, NEG_INF)] * 2 + [far2])
    n_sel = seq // SEL_BLOCK
    pos = np.arange(seq).reshape(seq // n, 1, n)
    onehot = (pos // SEL_BLOCK == np.arange(n_sel)[None, :, None]).astype(np.float32)
    return jnp.asarray(tri), jnp.asarray(far2), jnp.asarray(onehot, BF16)


def _overlap_t(n_pad, n_sel):
    n_cmp = n_pad - 1
    cmp_start = np.arange(n_pad) * CMP_STRIDE
    sel_start = np.arange(n_sel) * SEL_BLOCK
    ov = ((cmp_start[None, :] < sel_start[:, None] + SEL_BLOCK)
          & (cmp_start[None, :] + CMP_BLOCK > sel_start[:, None])
          & (np.arange(n_pad)[None, :] < n_cmp))
    return jnp.asarray(ov.astype(np.float32), BF16)


def _nsa_proj_weights(w_in):
    d = w_in.shape[0]
    hd, kvd = N_HEADS * HEAD_DIM, NSA_KV_HEADS * HEAD_DIM
    wq, wkc, wvc, wks, wvs, wkw, wvw, wgl = jnp.split(w_in, [hd + i * kvd for i in range(7)], axis=1)
    wt = jnp.concatenate([wq * Q_PRESCALE, wks, wvs, wkw, wvw], axis=1).T.astype(BF16)
    w2 = jnp.concatenate([wkc, wvc, wgl, jnp.zeros((d, -wgl.shape[1] % 128), w_in.dtype)], axis=1).astype(BF16)
    return wt, w2


def _nsa_mixer(qkv_t, f, cmp_pos, cmp_w1, cmp_w2, rel_bias, batch, seq):
    G, R, dh = NSA_KV_HEADS, NSA_GROUP, HEAD_DIM
    hd, kvd = N_HEADS * dh, G * dh
    n_gate = N_HEADS * 3
    gates_t = f[:, 2 * kvd:2 * kvd + n_gate].reshape(batch, seq, G, R * 3).transpose(0, 2, 3, 1)
    gates_t = jnp.pad(gates_t, ((0, 0), (0, 0), (0, FEAT_ROWS - R * 3), (0, 0)))

    nh = seq // CMP_STRIDE
    cmp = _compress(f, cmp_pos.reshape(2, 2, CMP_STRIDE * dh), cmp_w1.astype(BF16), cmp_w2.astype(BF16),
                    batch, seq)
    kcmp = cmp[:, 0]
    vcmp_t = cmp[:, 1].transpose(0, 1, 3, 2)

    n_sel = seq // SEL_BLOCK
    tb, tab = _bias_tables(rel_bias, nh)
    _, far2, onehot_t = _static_tiles(seq)
    oc_t, sel_t = _cmp_sel(qkv_t, kcmp, vcmp_t, tb, _overlap_t(nh, n_sel))
    o = _sel_win(qkv_t, sel_t, onehot_t, tab, far2, gates_t, oc_t)
    return o.reshape(batch * seq, hd)


def _kv_proj_weights(w_kvf):
    hd = N_HEADS * HEAD_DIM
    wf = w_kvf[:, 2 * hd:]
    w2 = jnp.concatenate([wf, jnp.zeros((wf.shape[0], -wf.shape[1] % 128), wf.dtype)], axis=1).astype(BF16)
    return w_kvf[:, :2 * hd].T.astype(BF16), w2


def _shared_kv(kv_t, f, b_f, batch, seq):
    fl_t = f[:, :N_HEADS].reshape(batch, seq, N_HEADS).transpose(0, 2, 1)
    pieces = _forget_cumsum(fl_t, b_f.astype(F32)).transpose(0, 2, 1, 3)
    pieces = jnp.pad(pieces, ((0, 0), (0, 0), (0, FEAT_ROWS - N_DECAY_PIECES), (0, 0)))
    kfeat_t = pieces.reshape(batch, N_HEADS, FEAT_ROWS, seq // SEQ_TILE, SEQ_TILE).transpose(0, 3, 1, 2, 4)
    return kv_t, kfeat_t


def _fox_mixer(q_t, kv, batch, seq):
    tri, _, _ = _static_tiles(seq)
    return _fox_attention(q_t, *kv, tri).reshape(batch * seq, N_HEADS * HEAD_DIM)


def kernel(x, norm_g, ffn_w_in, ffn_w_out, nsa_w_in, nsa_cmp_pos, nsa_cmp_w1, nsa_cmp_w2, nsa_w_out, rel_bias,
           kv_norm_g, fox_w_kvf, fox_b_f, fox_w_q, fox_w_out, final_g):
    batch, seq, d = x.shape
    depth = norm_g.shape[0]
    n_a = nsa_w_in.shape[0]
    assert 1 <= n_a <= depth
    xf = x.reshape(batch * seq, d)
    w_in = ffn_w_in.astype(BF16)
    w_out = ffn_w_out.astype(BF16)
    kv = None
    for l in range(depth):
        if l < n_a:
            mixer_proj = (norm_g[l, 1],) + _nsa_proj_weights(nsa_w_in[l])
            xf, qkv_t, f = _ffn(xf, norm_g[l, 0], w_in, w_out, (l, 0), batch, seq, proj=mixer_proj)
            o = _nsa_mixer(qkv_t, f, nsa_cmp_pos[l], nsa_cmp_w1[l], nsa_cmp_w2[l], rel_bias, batch, seq)
            w_mix = nsa_w_out[l]
        else:
            mixer_proj = (norm_g[l, 1], (fox_w_q[l - n_a] * Q_PRESCALE).T.astype(BF16), None)
            xf, q_t = _ffn(xf, norm_g[l, 0], w_in, w_out, (l, 0), batch, seq, proj=mixer_proj)
            o = _fox_mixer(q_t, kv, batch, seq)
            w_mix = fox_w_out[l - n_a]
        kv_proj = (kv_norm_g,) + _kv_proj_weights(fox_w_kvf) if l == n_a - 1 and n_a < depth else None
        outs = _ffn(xf, norm_g[l, 2], w_in, w_out, (l, 1), batch, seq, mixer=(o, w_mix.astype(BF16)),
                    final_g=final_g if l == depth - 1 else None, proj=kv_proj)
        xf = outs[0]
        if kv_proj is not None:
            kv = _shared_kv(outs[1], outs[2], fox_b_f, batch, seq)
    return xf.reshape(batch, seq, d)
```

```python
import functools
import math

import numpy as np
import jax
import jax.numpy as jnp
from jax import lax
from jax.experimental import pallas as pl
from jax.experimental.pallas import tpu as pltpu

N_HEADS = 16
HEAD_DIM = 64
NSA_KV_HEADS = 4
NSA_GROUP = N_HEADS // NSA_KV_HEADS
NSA_GROUPS_PER_STEP = 2
NSA_Q_TILES_PER_STEP = 2
CMP_BLOCK = 32
CMP_STRIDE = 16
SEL_BLOCK = 64
SEL_BLOCK_SHIFT = SEL_BLOCK.bit_length() - 1
assert 1 << SEL_BLOCK_SHIFT == SEL_BLOCK
SEL_TOPK = 16
WINDOW = 512
N_BUCKETS = 32
MAX_DISTANCE = 128
EPS = 1e-6
NEG_INF = -1e30
LOG2E = math.log2(math.e)
Q_PRESCALE = LOG2E * HEAD_DIM ** -0.5
SEL_FORCE = 1e4

F32 = jnp.float32
BF16 = jnp.bfloat16

VMEM_LIMIT_BYTES = 48 * 1024 * 1024

ROW_TILE = 512
FF_TILE = 256
SEQ_TILE = 256
FEAT_ROWS = 16
ACC_ROWS = HEAD_DIM + FEAT_ROWS
QK_LOOKAHEAD_TILES = 6
KEY_FEATS = 128

_NT = (((1,), (1,)), ((), ()))


def _params(*sem):
    return pltpu.CompilerParams(dimension_semantics=sem, vmem_limit_bytes=VMEM_LIMIT_BYTES)


def _rms(x, g):
    return x * lax.rsqrt(jnp.mean(x * x, axis=-1, keepdims=True) + EPS) * g


def _silu(a):
    return a * jax.nn.sigmoid(a)


PROJ_CHUNK = 512


def _project_tiles(y, g_ref, wt_ref, w2_ref, ot_ref, o2_ref):
    channels = ot_ref.shape[2]
    for j in range(ot_ref.shape[1]):
        rows = slice(j * SEQ_TILE, (j + 1) * SEQ_TILE)
        hn = _rms(y[rows, :], g_ref[...]).astype(BF16)
        for c in range(0, channels, PROJ_CHUNK):
            ot_ref[0, j, c:c + PROJ_CHUNK, :] = lax.dot_general(
                wt_ref[c:c + PROJ_CHUNK, :], hn, _NT, preferred_element_type=F32).astype(BF16)
        if w2_ref is not None:
            o2_ref[rows, :] = jnp.dot(hn, w2_ref[...], preferred_element_type=F32)


def _ffn_kernel(x_ref, g_ref, wi_ref, wo_ref, *rest, has_mixer, final_norm, n_proj):
    d_ff = wo_ref.shape[2]
    rest = list(rest)
    mix_ref, wm_ref = (rest.pop(0), rest.pop(0)) if has_mixer else (None, None)
    fg_ref = rest.pop(0) if final_norm else None
    pg_ref, pwt_ref = (rest.pop(0), rest.pop(0)) if n_proj else (None, None)
    pw2_ref = rest.pop(0) if n_proj == 2 else None
    o_ref = rest.pop(0)
    pot_ref = rest.pop(0) if n_proj else None
    po2_ref = rest.pop(0) if n_proj == 2 else None
    x = x_ref[...]
    if has_mixer:
        x = x + jnp.dot(mix_ref[...], wm_ref[...], preferred_element_type=F32)
    hn = _rms(x, g_ref[...]).astype(BF16)
    acc = None
    for c in range(0, d_ff, FF_TILE):
        a = jnp.dot(hn, wi_ref[0, 0, :, c:c + FF_TILE], preferred_element_type=F32)
        b = jnp.dot(hn, wi_ref[0, 0, :, d_ff + c:d_ff + c + FF_TILE], preferred_element_type=F32)
        part = jnp.dot((_silu(a) * b).astype(BF16), wo_ref[0, 0, c:c + FF_TILE, :], preferred_element_type=F32)
        acc = part if acc is None else acc + part
    y = x + 0.5 * acc
    if final_norm:
        y = _rms(y, fg_ref[...])
    o_ref[...] = y
    if n_proj:
        _project_tiles(y, pg_ref, pwt_ref, pw2_ref, pot_ref, po2_ref)


def _resident(shape, index=None):
    index = (0,) * len(shape) if index is None else index
    return pl.BlockSpec(shape, lambda *_: index, pipeline_mode=pl.Buffered(1))


def _ffn(x, g, w_in, w_out, which, batch, seq, mixer=None, final_g=None, proj=None):
    n, d = x.shape
    in_specs = [
        pl.BlockSpec((ROW_TILE, d), lambda i: (i, 0)),
        _resident((1, d)),
        _resident((1, 1) + w_in.shape[2:], which + (0, 0)),
        _resident((1, 1) + w_out.shape[2:], which + (0, 0)),
    ]
    args = [x, g.reshape(1, d), w_in, w_out]
    out_specs = [pl.BlockSpec((ROW_TILE, d), lambda i: (i, 0))]
    out_shape = [jax.ShapeDtypeStruct((n, d), F32)]
    if mixer is not None:
        o, w_mix = mixer
        in_specs += [pl.BlockSpec((ROW_TILE, o.shape[1]), lambda i: (i, 0)), _resident(w_mix.shape)]
        args += [o, w_mix]
    if final_g is not None:
        in_specs.append(_resident((1, d)))
        args.append(final_g.reshape(1, d))
    n_proj = 0
    if proj is not None:
        g2, wt, w2 = proj
        n_proj = 1 if w2 is None else 2
        channels = wt.shape[0]
        per_seq = seq // ROW_TILE
        in_specs += [_resident((1, d)), _resident(wt.shape)]
        args += [g2.reshape(1, d), wt]
        out_specs.append(pl.BlockSpec((1, ROW_TILE // SEQ_TILE, channels, SEQ_TILE),
                                      lambda i: (i // per_seq, i % per_seq, 0, 0)))
        out_shape.append(jax.ShapeDtypeStruct((batch, seq // SEQ_TILE, channels, SEQ_TILE), BF16))
        if w2 is not None:
            in_specs.append(_resident(w2.shape))
            args.append(w2)
            out_specs.append(pl.BlockSpec((ROW_TILE, w2.shape[1]), lambda i: (i, 0)))
            out_shape.append(jax.ShapeDtypeStruct((n, w2.shape[1]), F32))
    return pl.pallas_call(
        functools.partial(_ffn_kernel, has_mixer=mixer is not None, final_norm=final_g is not None, n_proj=n_proj),
        grid=(n // ROW_TILE,),
        in_specs=in_specs,
        out_specs=out_specs,
        out_shape=out_shape,
        name="ffn",
        compiler_params=_params("parallel"),
    )(*args)


def _compress_kernel(*refs):
    *f_refs, pos_ref, w1_ref, w2_ref, o_ref = refs
    nh = f_refs[0].shape[0] // CMP_STRIDE
    dh = HEAD_DIM
    per_chunk = f_refs[0].shape[1] // dh
    for chunk, f_ref in enumerate(f_refs):
        rows = [f_ref[pl.ds(j, nh, stride=CMP_STRIDE), :] for j in range(CMP_STRIDE)]
        for sub in range(per_chunk):
            kv, g = divmod(chunk * per_chunk + sub, NSA_KV_HEADS)
            x = jnp.concatenate([r[:, sub * dh:(sub + 1) * dh] for r in rows], axis=1)
            pos = pos_ref[kv]
            half = x.shape[1]
            top = jnp.dot((x + pos[0:1]).astype(BF16), w1_ref[kv, :half, :], preferred_element_type=F32)
            bot = jnp.dot((x + pos[1:2]).astype(BF16), w1_ref[kv, half:, :], preferred_element_type=F32)
            pre = top + pltpu.roll(bot, nh - 1, axis=0)
            hid = _silu(pre).astype(BF16)
            o_ref[0, kv, g] = jnp.dot(hid, w2_ref[kv], preferred_element_type=F32).astype(BF16)


def _compress(f, pos2, w1, w2, batch, seq):
    g, dh = NSA_KV_HEADS, HEAD_DIM
    nh = seq // CMP_STRIDE
    lanes = 128
    n_chunks = 2 * g * dh // lanes
    return pl.pallas_call(
        _compress_kernel,
        grid=(batch,),
        in_specs=[pl.BlockSpec((seq, lanes), functools.partial(lambda c, i: (i, c), c)) for c in range(n_chunks)] + [
            _resident(pos2.shape),
            _resident(w1.shape),
            _resident(w2.shape),
        ],
        out_specs=pl.BlockSpec((1, 2, g, nh, dh), lambda i: (i, 0, 0, 0, 0)),
        out_shape=jax.ShapeDtypeStruct((batch, 2, g, nh, dh), BF16),
        name="compress",
        compiler_params=_params("parallel"),
    )(*([f] * n_chunks), pos2, w1, w2)


def _split3(x):
    p1 = x.astype(BF16)
    r1 = x - p1.astype(F32)
    p2 = r1.astype(BF16)
    p3 = (r1 - p2.astype(F32)).astype(BF16)
    return p1, p2, p3


def _cmp_block_scores(gg, qi, n_rows, q_ref, kc_ref, vct_ref, tb_ref, ovt_ref, oct_ref):
    tq = SEQ_TILE
    nrep = NSA_GROUP
    dh = HEAD_DIM
    n_cmp_pad = kc_ref.shape[2]
    qt = jnp.concatenate([q_ref[0, 0, (gg * nrep + r) * dh:(gg * nrep + r + 1) * dh, :] for r in range(nrep)],
                         axis=1)
    s = jnp.dot(kc_ref[0, gg, :n_rows, :], qt, preferred_element_type=F32)
    off = pl.multiple_of(n_cmp_pad - (tq // CMP_STRIDE) * qi, 8)
    bias = jnp.concatenate([tb_ref[gg * nrep + r, pl.ds(off, n_rows), :] for r in range(nrep)], axis=1)
    s = s + bias
    m = jnp.max(s, axis=0, keepdims=True)
    e = jnp.exp2(s - m)
    assert tq & (tq - 1) == 0
    t = qi * tq + (lax.broadcasted_iota(jnp.int32, (1, nrep * tq), 1) & (tq - 1))
    p = e * jnp.where(t >= CMP_BLOCK - 1, 1.0 / jnp.sum(e, axis=0, keepdims=True), 0.0)

    oct = jnp.dot(vct_ref[0, gg, :, :n_rows], p.astype(BF16), preferred_element_type=F32)
    for r in range(nrep):
        oct_ref[0, gg, r] = oct[:, r * tq:(r + 1) * tq]

    psum = p[:, 0:tq]
    for r in range(1, nrep):
        psum = psum + p[:, r * tq:(r + 1) * tq]
    ovt = ovt_ref[:, :n_rows]
    imp = sum(jnp.dot(ovt, part, preferred_element_type=F32) for part in _split3(psum))

    n_sel = imp.shape[0]
    blk = lax.broadcasted_iota(jnp.int32, (n_sel, tq), 0)
    cur = jnp.right_shift(qi * tq + lax.broadcasted_iota(jnp.int32, (n_sel, tq), 1), SEL_BLOCK_SHIFT)
    forced = (blk == 0) | (blk == cur) | (blk == cur - 1)
    return jnp.where(blk > cur, -SEL_FORCE, imp + jnp.where(forced, SEL_FORCE, 0.0))


def _cmp_sel_kernel(q_ref, kc_ref, vct_ref, tb_ref, ovt_ref, oct_ref, sel_ref, score_ref, cnt_ref):
    qi = pl.program_id(2)
    tq = SEQ_TILE
    gps, n_cmp_pad = kc_ref.shape[1], kc_ref.shape[2]
    n_sel = score_ref.shape[1]

    half = n_cmp_pad // 2
    last_visible = (qi * tq + tq - CMP_BLOCK) // CMP_STRIDE
    for n_rows, cond in ((half, last_visible < half), (n_cmp_pad, last_visible >= half)):
        @pl.when(cond)
        def _():
            for gg in range(gps):
                score_ref[gg] = _cmp_block_scores(gg, qi, n_rows, q_ref, kc_ref, vct_ref, tb_ref, ovt_ref, oct_ref)

    scores = [score_ref[gg] for gg in range(gps)]

    rows = 8
    n_groups = n_sel // rows
    blk_in_group = lax.broadcasted_iota(jnp.int32, (rows, tq), 0)
    last_group = jnp.right_shift(qi * tq + tq - 1, SEL_BLOCK_SHIFT) // rows
    cnt_ref[...] = jnp.zeros_like(cnt_ref)

    def count(score, k, g):
        sg = score[g * rows:(g + 1) * rows]
        total = None
        for sp in range(k * rows, (k + 1) * rows):
            row = jnp.broadcast_to(score[sp:sp + 1, :], (rows, tq))
            if g > k:
                one = jnp.where(row >= sg, 1.0, 0.0)
            elif g < k:
                one = jnp.where(row > sg, 1.0, 0.0)
            else:
                tie = jnp.where(blk_in_group + g * rows > sp, 1.0, 0.0)
                one = jnp.where(row > sg, 1.0, jnp.where(row == sg, tie, 0.0))
            total = one if total is None else total + one
        return total

    for level in range(n_groups):
        @pl.when(level <= last_group)
        def _():
            for gg, score in enumerate(scores):
                for g in range(level + 1):
                    cnt_ref[gg, g * rows:(g + 1) * rows, :] += count(score, level, g)
                extra = None
                for k in range(level):
                    c = count(score, k, level)
                    extra = c if extra is None else extra + c
                if extra is not None:
                    cnt_ref[gg, level * rows:(level + 1) * rows, :] += extra

    sel_ref[0] = jnp.where(cnt_ref[...] < min(SEL_TOPK, n_sel), 1.0, 0.0).astype(BF16)


def _cmp_sel(qkv_t, kcmp, vcmp_t, tb, ovt):
    b, nt, _, _ = qkv_t.shape
    g, r, dh = NSA_KV_HEADS, NSA_GROUP, HEAD_DIM
    t = nt * SEQ_TILE
    n_pad = kcmp.shape[2]
    n_sel = ovt.shape[0]
    tq = SEQ_TILE
    gps = NSA_GROUPS_PER_STEP
    return pl.pallas_call(
        _cmp_sel_kernel,
        grid=(b, g // gps, t // tq),
        in_specs=[
            pl.BlockSpec((1, 1, gps * r * dh, tq), lambda i, j, k: (i, k, j, 0)),
            pl.BlockSpec((1, gps, n_pad, dh), lambda i, j, k: (i, j, 0, 0)),
            pl.BlockSpec((1, gps, dh, n_pad), lambda i, j, k: (i, j, 0, 0)),
            pl.BlockSpec((gps * r, 2 * n_pad, tq), lambda i, j, k: (j, 0, 0)),
            pl.BlockSpec((n_sel, n_pad), lambda i, j, k: (0, 0)),
        ],
        out_specs=[
            pl.BlockSpec((1, gps, r, dh, tq), lambda i, j, k: (i, j, 0, 0, k)),
            pl.BlockSpec((1, gps, n_sel, tq), lambda i, j, k: (i, j, 0, k)),
        ],
        out_shape=[
            jax.ShapeDtypeStruct((b, g, r, dh, t), F32),
            jax.ShapeDtypeStruct((b, g, n_sel, t), BF16),
        ],
        scratch_shapes=[pltpu.VMEM((gps, n_sel, tq), F32), pltpu.VMEM((gps, n_sel, tq), F32)],
        name="cmp_sel",
        compiler_params=_params("parallel", "parallel", "parallel"),
    )(qkv_t, kcmp, vcmp_t, tb, ovt)


def _ones_rows(width):
    return jnp.ones((FEAT_ROWS, width), BF16)


def _with_ones(v_t):
    return jnp.concatenate([v_t, _ones_rows(v_t.shape[1])], axis=0)


def _key_rows(k_ref, lead, cs):
    if len(cs) == 1:
        return k_ref[lead, cs[0]]
    return k_ref[lead, pl.ds(cs[0], len(cs))].reshape(len(cs) * SEQ_TILE, KEY_FEATS)


def _value_cols(v_tiles):
    return _with_ones(v_tiles[0] if len(v_tiles) == 1 else jnp.concatenate(v_tiles, axis=1))


def _online_update(m_ref, acc_ref, slot, tiles, first):
    mt = None
    for s, _ in tiles:
        cm = jnp.max(s, axis=0, keepdims=True)
        mt = cm if mt is None else jnp.maximum(mt, cm)
    m_old = None if first else m_ref[slot]
    m_new = mt if first else jnp.maximum(m_old, mt)
    pv = None
    for s, va in tiles:
        d = jnp.dot(va, jnp.exp2(s - m_new).astype(BF16), preferred_element_type=F32)
        pv = d if pv is None else pv + d
    acc_ref[slot] = pv if first else jnp.exp2(m_old - m_new) * acc_ref[slot] + pv
    m_ref[slot] = m_new


def _pipelined_updates(m_ref, acc_ref, jobs, tiles_per_job=1):
    lookahead = max(1, QK_LOOKAHEAD_TILES // tiles_per_job)
    pending = [job[1]() for job in jobs[:lookahead]]
    for i, (slot, _, values_fn, first) in enumerate(jobs):
        current = pending.pop(0)
        if i + lookahead < len(jobs):
            pending.append(jobs[i + lookahead][1]())
        _online_update(m_ref, acc_ref, slot, list(zip(current, values_fn())), first)


def _for_far_tiles(n_far, run):
    n_quads = n_far // 4

    def quad(i, carry):
        run([[4 * i, 4 * i + 1], [4 * i + 2, 4 * i + 3]])
        return carry

    lax.fori_loop(0, n_quads, quad, 0)

    @pl.when((n_far & 2) != 0)
    def _():
        run([[4 * n_quads, 4 * n_quads + 1]])

    @pl.when((n_far & 1) != 0)
    def _():
        run([[n_far - 1]])


def _normalised(acc_ref, slot):
    acc = acc_ref[slot]
    return acc[:HEAD_DIM] / acc[HEAD_DIM:HEAD_DIM + 1]


def _sel_win_kernel(q_ref, ks_ref, vs_ref, kw_ref, vw_ref, sel_ref, oh_ref, *rest):
    nq = NSA_Q_TILES_PER_STEP
    tab_refs, far2_refs = rest[:nq], rest[nq:2 * nq]
    gate_ref, oc_ref, o_ref, ksrm_ref, kwrm_ref, m_ref, acc_ref = rest[2 * nq:]
    step_id = pl.program_id(2)
    tq = SEQ_TILE
    nrep = NSA_GROUP
    gps = NSA_GROUPS_PER_STEP
    dh = HEAD_DIM
    pad = jnp.zeros((KEY_FEATS - dh, tq), BF16)
    base = nq * step_id

    @pl.when(step_id == 0)
    def _():
        def to_rows(c, carry):
            for gg in range(gps):
                ksa = jnp.concatenate([ks_ref[0, c, gg * dh:(gg + 1) * dh, :], oh_ref[c]], axis=0)
                ksrm_ref[gg, c] = ksa.astype(F32).T.astype(BF16)
                kwa = jnp.concatenate([kw_ref[0, c, gg * dh:(gg + 1) * dh, :], pad], axis=0)
                kwrm_ref[gg, c] = kwa.astype(F32).T.astype(BF16)
            return carry

        lax.fori_loop(0, ks_ref.shape[1], to_rows, 0)

    q_sel, q_win, q_none = [], [], []
    for sub in range(nq):
        lanes = slice(sub * tq, (sub + 1) * tq)
        q_sel.append([]), q_win.append([]), q_none.append([])
        for gg in range(gps):
            mq = ((sel_ref[0, gg, :, lanes].astype(F32) - 1.0) * (-NEG_INF)).astype(BF16)
            for r in range(nrep):
                q = q_ref[0, sub, (gg * nrep + r) * dh:(gg * nrep + r + 1) * dh, :]
                q_sel[sub].append(jnp.concatenate([q, mq], axis=0))
                q_win[sub].append(jnp.concatenate([q, pad], axis=0))
                q_none[sub].append(jnp.concatenate([q, jnp.full_like(mq, NEG_INF)], axis=0))

    def jobs(sub, branch, cs, bias=None, first=False, queries=None):
        k_ref, v_ref = (ksrm_ref, vs_ref) if branch == 0 else (kwrm_ref, vw_ref)
        qa = queries if queries is not None else (q_sel if branch == 0 else q_win)[sub]

        def logits(gg, r):
            head = gg * nrep + r
            tiles = [jnp.dot(_key_rows(k_ref, gg, cs), qa[head], preferred_element_type=F32)]
            return tiles if bias is None else [s + bias(head) for s in tiles]

        def values(gg):
            return [_value_cols([v_ref[0, c, gg * dh:(gg + 1) * dh, :] for c in cs])]

        return [(((sub * gps + gg) * 2 + branch) * nrep + r, functools.partial(logits, gg, r),
                 functools.partial(values, gg), first) for gg in range(gps) for r in range(nrep)]

    near = []
    for sub in range(nq):
        diag = lambda head, sub=sub: tab_refs[sub][0, head, 1]
        near += jobs(sub, 0, [base + sub], diag, first=True) + jobs(sub, 1, [base + sub], diag, first=True)
    for sub in range(nq):
        prev = lambda head, sub=sub: tab_refs[sub][0, head, 0]
        c_prev = jnp.maximum(base + sub - 1, 0)
        near += jobs(sub, 0, [c_prev], prev) + jobs(sub, 1, [c_prev], prev)
    for sub in range(nq):
        edge = lambda head, sub=sub: far2_refs[sub][0]
        near += jobs(sub, 1, [jnp.maximum(base + sub - 2, 0)], edge)
    for sub in range(nq):
        for e in range(sub):
            if e == 0:
                masked = [jnp.where(step_id > 0, a, b) for a, b in zip(q_sel[sub], q_none[sub])]
                near += jobs(sub, 0, [jnp.maximum(base - 1, 0)], queries=masked)
            else:
                near += jobs(sub, 0, [base - 1 + e])
    _pipelined_updates(m_ref, acc_ref, near)

    _for_far_tiles(jnp.maximum(base - 1, 0), lambda groups: _pipelined_updates(
        m_ref, acc_ref, [job for cs in groups for sub in range(nq) for job in jobs(sub, 0, cs)],
        tiles_per_job=len(groups[0])))

    for sub in range(nq):
        lanes = slice(sub * tq, (sub + 1) * tq)
        outs = []
        for gg in range(gps):
            gates = jax.nn.sigmoid(gate_ref[0, gg, :, lanes])
            slot = (sub * gps + gg) * 2 * nrep
            for r in range(nrep):
                outs.append(gates[3 * r:3 * r + 1] * oc_ref[0, gg, r, :, lanes]
                            + gates[3 * r + 1:3 * r + 2] * _normalised(acc_ref, slot + r)
                            + gates[3 * r + 2:3 * r + 3] * _normalised(acc_ref, slot + nrep + r))
        o_ref[0, lanes, :] = jnp.concatenate(outs, axis=0).T.astype(BF16)


def _sel_win(qkv_t, sel_t, onehot_t, tab, far2, gates_t, oc_t):
    b, nt, channels, tq = qkv_t.shape
    g, r, dh = NSA_KV_HEADS, NSA_GROUP, HEAD_DIM
    gps, nq = NSA_GROUPS_PER_STEP, NSA_Q_TILES_PER_STEP
    hd = g * r * dh
    n_sel = sel_t.shape[2]
    kv_block = lambda which: pl.BlockSpec(
        (1, nt, gps * dh, tq), lambda i, j, k: (i, 0, (hd + which * g * dh) // (gps * dh) + j, 0))
    tab_block = lambda sub: pl.BlockSpec(
        (1, gps * r, 2, tq, tq), lambda i, j, k: (jnp.minimum(nq * k + sub, tab.shape[0] - 1), j, 0, 0, 0))
    far2_block = lambda sub: pl.BlockSpec(
        (1, tq, tq), lambda i, j, k: (jnp.minimum(nq * k + sub, far2.shape[0] - 1), 0, 0))
    slots = nq * gps * 2 * r
    return pl.pallas_call(
        _sel_win_kernel,
        grid=(b, g // gps, nt // nq),
        in_specs=[
            pl.BlockSpec((1, nq, gps * r * dh, tq), lambda i, j, k: (i, k, j, 0)),
            kv_block(0), kv_block(1), kv_block(2), kv_block(3),
            pl.BlockSpec((1, gps, n_sel, nq * tq), lambda i, j, k: (i, j, 0, k)),
            pl.BlockSpec((nt, n_sel, tq), lambda i, j, k: (0, 0, 0)),
            *[tab_block(sub) for sub in range(nq)],
            *[far2_block(sub) for sub in range(nq)],
            pl.BlockSpec((1, gps, FEAT_ROWS, nq * tq), lambda i, j, k: (i, j, 0, k)),
            pl.BlockSpec((1, gps, r, dh, nq * tq), lambda i, j, k: (i, j, 0, 0, k)),
        ],
        out_specs=pl.BlockSpec((1, nq * tq, gps * r * dh), lambda i, j, k: (i, k, j)),
        out_shape=jax.ShapeDtypeStruct((b, nt * tq, hd), BF16),
        scratch_shapes=[pltpu.VMEM((gps, nt, tq, KEY_FEATS), BF16), pltpu.VMEM((gps, nt, tq, KEY_FEATS), BF16),
                        pltpu.VMEM((slots, 1, tq), F32), pltpu.VMEM((slots, ACC_ROWS, tq), F32)],
        name="sel_win",
        compiler_params=_params("parallel", "parallel", "arbitrary"),
    )(qkv_t, qkv_t, qkv_t, qkv_t, qkv_t, sel_t, onehot_t, *([tab] * nq), *([far2] * nq), gates_t, oc_t)


N_DECAY_PIECES = 3


def _forget_cumsum_kernel(fl_ref, bf_ref, piece_ref):
    x = fl_ref[0] + bf_ref[...]
    y = jnp.minimum(x, 0.0) - jnp.log1p(jnp.exp(-jnp.abs(x)))
    t = y.shape[1]
    lane = lax.broadcasted_iota(jnp.int32, y.shape, 1)
    k = 1
    while k < t:
        y = y + jnp.where(lane >= k, pltpu.roll(y, k, axis=1), 0.0)
        k *= 2
    for i, piece in enumerate(_split3(-LOG2E * y)):
        piece_ref[0, i] = piece


def _forget_cumsum(fl_t, b_f):
    b, h, t = fl_t.shape
    return pl.pallas_call(
        _forget_cumsum_kernel,
        grid=(b,),
        in_specs=[pl.BlockSpec((1, h, t), lambda i: (i, 0, 0)), pl.BlockSpec((h, 1), lambda i: (0, 0))],
        out_specs=pl.BlockSpec((1, N_DECAY_PIECES, h, t), lambda i: (i, 0, 0, 0)),
        out_shape=jax.ShapeDtypeStruct((b, N_DECAY_PIECES, h, t), BF16),
        name="forget_cumsum",
        compiler_params=_params("parallel"),
    )(fl_t, b_f.reshape(h, 1))


FOX_HEADS_PER_STEP = 8
FOX_Q_TILES_PER_STEP = 4


def _fox_kernel(q_ref, k_ref, v_ref, kf_ref, tri_ref, o_ref, krm_ref, m_ref, acc_ref):
    step_id = pl.program_id(2)
    tq = SEQ_TILE
    hp = FOX_HEADS_PER_STEP
    nq = FOX_Q_TILES_PER_STEP
    n_pieces = N_DECAY_PIECES
    pad = jnp.zeros((KEY_FEATS - HEAD_DIM - FEAT_ROWS, tq), BF16)

    @pl.when(step_id == 0)
    def _():
        def to_rows(c, carry):
            for h in range(hp):
                ka = jnp.concatenate([k_ref[0, c, h * HEAD_DIM:(h + 1) * HEAD_DIM, :], kf_ref[0, c, h], pad], axis=0)
                krm_ref[h, c] = ka.astype(F32).T.astype(BF16)
            return carry

        lax.fori_loop(0, k_ref.shape[1], to_rows, 0)

    q_feat = jnp.where(lax.broadcasted_iota(jnp.int32, (FEAT_ROWS, tq), 0) < n_pieces, 1.0, 0.0).astype(BF16)
    qa = [[jnp.concatenate([q_ref[0, sub, h * HEAD_DIM:(h + 1) * HEAD_DIM, :], q_feat, pad], axis=0)
           for h in range(hp)] for sub in range(nq)]

    def jobs(sub, cs, first=False, causal=False):
        def logits(h):
            tiles = [jnp.dot(_key_rows(krm_ref, h, cs), qa[sub][h], preferred_element_type=F32)]
            return [s + tri_ref[...] for s in tiles] if causal else tiles

        def values(h):
            return [_value_cols([v_ref[0, c, h * HEAD_DIM:(h + 1) * HEAD_DIM, :] for c in cs])]

        return [(sub * hp + h, functools.partial(logits, h), functools.partial(values, h), first)
                for h in range(hp)]

    base = nq * step_id
    near = [job for sub in range(nq) for job in jobs(sub, [base + sub], first=True, causal=True)]
    near += [job for sub in range(nq) for c in range(sub) for job in jobs(sub, [base + c])]
    _pipelined_updates(m_ref, acc_ref, near)
    _for_far_tiles(base, lambda groups: _pipelined_updates(
        m_ref, acc_ref, [job for cs in groups for sub in range(nq) for job in jobs(sub, cs)],
        tiles_per_job=len(groups[0])))
    for sub in range(nq):
        o_t = jnp.concatenate([_normalised(acc_ref, sub * hp + h) for h in range(hp)], axis=0)
        o_ref[0, sub * tq:(sub + 1) * tq, :] = o_t.T.astype(BF16)


def _fox_attention(q_t, kv_t, kfeat_t, tri):
    b, nt, hd, tq = q_t.shape
    hp, dh, nq = FOX_HEADS_PER_STEP, HEAD_DIM, FOX_Q_TILES_PER_STEP
    n_groups = hd // (hp * dh)
    return pl.pallas_call(
        _fox_kernel,
        grid=(b, n_groups, nt // nq),
        in_specs=[
            pl.BlockSpec((1, nq, hp * dh, tq), lambda i, j, k: (i, k, j, 0)),
            pl.BlockSpec((1, nt, hp * dh, tq), lambda i, j, k: (i, 0, j, 0)),
            pl.BlockSpec((1, nt, hp * dh, tq), lambda i, j, k: (i, 0, n_groups + j, 0)),
            pl.BlockSpec((1, nt, hp, FEAT_ROWS, tq), lambda i, j, k: (i, 0, j, 0, 0)),
            pl.BlockSpec((tq, tq), lambda i, j, k: (0, 0)),
        ],
        out_specs=pl.BlockSpec((1, nq * tq, hp * dh), lambda i, j, k: (i, k, j)),
        out_shape=jax.ShapeDtypeStruct((b, nt * tq, hd), BF16),
        scratch_shapes=[pltpu.VMEM((hp, nt, tq, KEY_FEATS), BF16),
                        pltpu.VMEM((nq * hp, 1, tq), F32), pltpu.VMEM((nq * hp, ACC_ROWS, tq), F32)],
        name="fox_attention",
        compiler_params=_params("parallel", "parallel", "arbitrary"),
    )(q_t, kv_t, kv_t, kfeat_t, tri)


def _t5_bucket_np(dist):
    dist = np.maximum(dist, 0)
    max_exact = N_BUCKETS // 2
    ratio = np.maximum(dist, 1).astype(np.float32) / np.float32(max_exact)
    scaled = np.log(ratio) / np.float32(math.log(MAX_DISTANCE / max_exact))
    large = np.minimum(max_exact + (scaled * np.float32(N_BUCKETS - max_exact)).astype(np.int32), N_BUCKETS - 1)
    return np.where(dist < max_exact, dist, large).astype(np.int32)


_MASKED = MAX_DISTANCE + 1


def _dist_index(dist, valid):
    return np.where(valid, np.minimum(dist, MAX_DISTANCE), _MASKED).astype(np.int32)


def _toeplitz(f_ext, n):
    period = 2 * n
    flat = jnp.tile(f_ext, n)[..., :n * (period - 1)]
    return flat.reshape(f_ext.shape[:-1] + (n, period - 1))[..., :n]


def _key_query_delta(n):
    p = np.arange(2 * n)
    return np.where(p < n, p, p - 2 * n)


def _bias_tables(rel_bias, n_pad):
    bucket = _t5_bucket_np(np.arange(MAX_DISTANCE + 1))
    by_dist = jnp.concatenate([rel_bias[bucket], jnp.full((1, N_HEADS), NEG_INF, F32)], axis=0).T
    far_const = by_dist[:, MAX_DISTANCE:MAX_DISTANCE + 1]

    tq = SEQ_TILE
    u = np.arange(2 * n_pad) - n_pad
    dist = np.arange(tq)[None, :] - CMP_STRIDE * u[:, None] - (CMP_BLOCK - 1)
    varying = np.nonzero((dist.max(axis=1) >= 0) & (dist.min(axis=1) < MAX_DISTANCE))[0]
    lo, hi = int(varying[0]), int(varying[-1]) + 1
    strip = by_dist[:, _dist_index(dist[lo:hi], dist[lo:hi] >= 0)]
    tb = jnp.concatenate([
        jnp.broadcast_to(far_const[:, :, None], (N_HEADS, lo, tq)),
        strip,
        jnp.full((N_HEADS, 2 * n_pad - hi, tq), NEG_INF, F32)], axis=1)

    shifted = jnp.concatenate([by_dist[:, :-1] - far_const, by_dist[:, -1:]], axis=1)
    n = SEQ_TILE
    delta = _key_query_delta(n)
    prev = _toeplitz(shifted[:, _dist_index(delta + n, delta + n >= 0)], n)
    diag = _toeplitz(shifted[:, _dist_index(delta, delta >= 0)], n)
    tab = jnp.stack([prev, diag], axis=1)
    no_prev = jnp.stack([jnp.full_like(prev, NEG_INF), diag], axis=1)
    return LOG2E * tb, LOG2E * jnp.stack([no_prev, tab])


def _static_tiles(seq):
    n = SEQ_TILE
    key = np.arange(n)[:, None]
    query = np.arange(n)[None, :]
    tri = np.where(key <= query, 0.0, NEG_INF).astype(np.float32)
    far2 = np.where(query - key + 2 * n < WINDOW, 0.0, NEG_INF).astype(np.float32)
    assert WINDOW == 2 * n
    far2 = np.stack([np.full_like(far2, NEG_INF)] * 2 + [far2])
    n_sel = seq // SEL_BLOCK
    pos = np.arange(seq).reshape(seq // n, 1, n)
    onehot = (pos // SEL_BLOCK == np.arange(n_sel)[None, :, None]).astype(np.float32)
    return jnp.asarray(tri), jnp.asarray(far2), jnp.asarray(onehot, BF16)


def _overlap_t(n_pad, n_sel):
    n_cmp = n_pad - 1
    cmp_start = np.arange(n_pad) * CMP_STRIDE
    sel_start = np.arange(n_sel) * SEL_BLOCK
    ov = ((cmp_start[None, :] < sel_start[:, None] + SEL_BLOCK)
          & (cmp_start[None, :] + CMP_BLOCK > sel_start[:, None])
          & (np.arange(n_pad)[None, :] < n_cmp))
    return jnp.asarray(ov.astype(np.float32), BF16)


def _nsa_proj_weights(w_in):
    d = w_in.shape[0]
    hd, kvd = N_HEADS * HEAD_DIM, NSA_KV_HEADS * HEAD_DIM
    wq, wkc, wvc, wks, wvs, wkw, wvw, wgl = jnp.split(w_in, [hd + i * kvd for i in range(7)], axis=1)
    wt = jnp.concatenate([wq * Q_PRESCALE, wks, wvs, wkw, wvw], axis=1).T.astype(BF16)
    w2 = jnp.concatenate([wkc, wvc, wgl, jnp.zeros((d, -wgl.shape[1] % 128), w_in.dtype)], axis=1).astype(BF16)
    return wt, w2


def _nsa_mixer(qkv_t, f, cmp_pos, cmp_w1, cmp_w2, rel_bias, batch, seq):
    G, R, dh = NSA_KV_HEADS, NSA_GROUP, HEAD_DIM
    hd, kvd = N_HEADS * dh, G * dh
    n_gate = N_HEADS * 3
    gates_t = f[:, 2 * kvd:2 * kvd + n_gate].reshape(batch, seq, G, R * 3).transpose(0, 2, 3, 1)
    gates_t = jnp.pad(gates_t, ((0, 0), (0, 0), (0, FEAT_ROWS - R * 3), (0, 0)))

    nh = seq // CMP_STRIDE
    cmp = _compress(f, cmp_pos.reshape(2, 2, CMP_STRIDE * dh), cmp_w1.astype(BF16), cmp_w2.astype(BF16),
                    batch, seq)
    kcmp = cmp[:, 0]
    vcmp_t = cmp[:, 1].transpose(0, 1, 3, 2)

    n_sel = seq // SEL_BLOCK
    tb, tab = _bias_tables(rel_bias, nh)
    _, far2, onehot_t = _static_tiles(seq)
    oc_t, sel_t = _cmp_sel(qkv_t, kcmp, vcmp_t, tb, _overlap_t(nh, n_sel))
    o = _sel_win(qkv_t, sel_t, onehot_t, tab, far2, gates_t, oc_t)
    return o.reshape(batch * seq, hd)


def _kv_proj_weights(w_kvf):
    hd = N_HEADS * HEAD_DIM
    wf = w_kvf[:, 2 * hd:]
    w2 = jnp.concatenate([wf, jnp.zeros((wf.shape[0], -wf.shape[1] % 128), wf.dtype)], axis=1).astype(BF16)
    return w_kvf[:, :2 * hd].T.astype(BF16), w2


def _shared_kv(kv_t, f, b_f, batch, seq):
    fl_t = f[:, :N_HEADS].reshape(batch, seq, N_HEADS).transpose(0, 2, 1)
    pieces = _forget_cumsum(fl_t, b_f.astype(F32)).transpose(0, 2, 1, 3)
    pieces = jnp.pad(pieces, ((0, 0), (0, 0), (0, FEAT_ROWS - N_DECAY_PIECES), (0, 0)))
    kfeat_t = pieces.reshape(batch, N_HEADS, FEAT_ROWS, seq // SEQ_TILE, SEQ_TILE).transpose(0, 3, 1, 2, 4)
    return kv_t, kfeat_t


def _fox_mixer(q_t, kv, batch, seq):
    tri, _, _ = _static_tiles(seq)
    return _fox_attention(q_t, *kv, tri).reshape(batch * seq, N_HEADS * HEAD_DIM)


def kernel(x, norm_g, ffn_w_in, ffn_w_out, nsa_w_in, nsa_cmp_pos, nsa_cmp_w1, nsa_cmp_w2, nsa_w_out, rel_bias,
           kv_norm_g, fox_w_kvf, fox_b_f, fox_w_q, fox_w_out, final_g):
    batch, seq, d = x.shape
    depth = norm_g.shape[0]
    n_a = nsa_w_in.shape[0]
    assert 1 <= n_a <= depth
    xf = x.reshape(batch * seq, d)
    w_in = ffn_w_in.astype(BF16)
    w_out = ffn_w_out.astype(BF16)
    kv = None
    for l in range(depth):
        if l < n_a:
            mixer_proj = (norm_g[l, 1],) + _nsa_proj_weights(nsa_w_in[l])
            xf, qkv_t, f = _ffn(xf, norm_g[l, 0], w_in, w_out, (l, 0), batch, seq, proj=mixer_proj)
            o = _nsa_mixer(qkv_t, f, nsa_cmp_pos[l], nsa_cmp_w1[l], nsa_cmp_w2[l], rel_bias, batch, seq)
            w_mix = nsa_w_out[l]
        else:
            mixer_proj = (norm_g[l, 1], (fox_w_q[l - n_a] * Q_PRESCALE).T.astype(BF16), None)
            xf, q_t = _ffn(xf, norm_g[l, 0], w_in, w_out, (l, 0), batch, seq, proj=mixer_proj)
            o = _fox_mixer(q_t, kv, batch, seq)
            w_mix = fox_w_out[l - n_a]
        kv_proj = (kv_norm_g,) + _kv_proj_weights(fox_w_kvf) if l == n_a - 1 and n_a < depth else None
        outs = _ffn(xf, norm_g[l, 2], w_in, w_out, (l, 1), batch, seq, mixer=(o, w_mix.astype(BF16)),
                    final_g=final_g if l == depth - 1 else None, proj=kv_proj)
        xf = outs[0]
        if kv_proj is not None:
            kv = _shared_kv(outs[1], outs[2], fox_b_f, batch, seq)
    return xf.reshape(batch, seq, d)
```
